```python
import math
import jax, jax.numpy as jnp
from jax import lax
import numpy as np

D_MODEL = 2048
BATCH = 4
SEQ = 2048
DEPTH = 1

DA_HEADS = 8
DA_HEAD_DIM = 64
D_ATTN = DA_HEADS * 2 * DA_HEAD_DIM
ROPE_THETA = 500000.0
ROT_DIM = DA_HEAD_DIM // 4
Q_BLOCK = 128
D_CONV = 1024
CONV_WIDTH = 31
NORM_EPS = 1e-6
SUBLN_EPS = 1e-5
LN_EPS = 1e-5
SPLITS = (D_ATTN, D_ATTN, D_ATTN, D_ATTN, 2 * D_CONV, D_CONV, D_MODEL, D_MODEL)
D_IN = D_ATTN * 4 + 2 * D_CONV + D_CONV + 2 * D_MODEL

kernel_name = 'hybrid_diffattn_conformer_gated_merge'


def lambda_init_fn(layer):
    return 0.8 - 0.6 * math.exp(-0.3 * layer)


def rms_norm(x, w, eps):
    xf = x.astype(jnp.float32)
    y = xf * lax.rsqrt(jnp.mean(xf * xf, axis=-1, keepdims=True) + eps)
    return (y * w.astype(jnp.float32)).astype(x.dtype)


def layer_norm(x, g, b, eps):
    xf = x.astype(jnp.float32)
    mu = jnp.mean(xf, axis=-1, keepdims=True)
    xc = xf - mu
    var = jnp.mean(xc * xc, axis=-1, keepdims=True)
    y = xc * lax.rsqrt(var + eps) * g.astype(jnp.float32) + b.astype(jnp.float32)
    return y.astype(x.dtype)


def partial_rotary(x, positions):
    half = ROT_DIM // 2
    inv_freq = ROPE_THETA ** (-jnp.arange(0, ROT_DIM, 2, dtype=jnp.float32) / ROT_DIM)
    ang = positions.astype(jnp.float32)[:, None] * inv_freq[None, :]
    cos = jnp.cos(ang)[None, :, None, None, :]
    sin = jnp.sin(ang)[None, :, None, None, :]
    xf = x.astype(jnp.float32)
    x1 = xf[..., :half]
    x2 = xf[..., half:ROT_DIM]
    r1 = (x1 * cos - x2 * sin).astype(x.dtype)
    r2 = (x2 * cos + x1 * sin).astype(x.dtype)
    return jnp.concatenate([r1, r2, x[..., ROT_DIM:]], axis=-1)


def diff_attention(q, k, v, lam):
    B, S, H, _, d = q.shape
    nb = S // Q_BLOCK
    scale = d ** -0.5
    qb = q.reshape(B, nb, Q_BLOCK, H, 2, d).transpose(1, 0, 2, 3, 4, 5)
    kpos = jnp.arange(S)
    neg = jnp.finfo(jnp.float32).min

    def block(args):
        qblk, i = args
        s = jnp.einsum('bqhcd,bkhcd->bhcqk', qblk, k,
                       preferred_element_type=jnp.float32) * scale
        qpos = i * Q_BLOCK + jnp.arange(Q_BLOCK)
        mask = qpos[:, None] >= kpos[None, :]
        s = jnp.where(mask, s, neg)
        p = jax.nn.softmax(s, axis=-1)
        a = p[:, :, 0] - lam * p[:, :, 1]
        return jnp.einsum('bhqk,bkhe->bqhe', a.astype(v.dtype), v)

    o = lax.map(block, (qb, jnp.arange(nb)))
    return o.transpose(1, 0, 2, 3, 4).reshape(B, S, H, 2 * d)


def causal_depthwise_conv(u, w, b):
    c = u.shape[-1]
    y = lax.conv_general_dilated(
        u, w[:, None, :].astype(u.dtype), window_strides=(1,),
        padding=[(CONV_WIDTH - 1, 0)], dimension_numbers=('NWC', 'WIO', 'NWC'),
        feature_group_count=c)
    return y + b.astype(u.dtype)


def setup_inputs(seed: int = 0) -> dict:
    key = jax.random.key(seed)
    ks = jax.random.split(key, 20)
    f32 = jnp.float32
    L = DEPTH

    def nrm(k, shape, scale):
        return jax.random.normal(k, shape, f32) * scale

    return {
        'x': jax.random.normal(ks[0], (BATCH, SEQ, D_MODEL), f32),
        'norm_pre_w': 1.0 + nrm(ks[1], (L, D_MODEL), 0.01),
        'w_in': nrm(ks[2], (L, D_MODEL, D_IN), D_MODEL ** -0.5),
        'lambda_q1': nrm(ks[3], (L, DA_HEAD_DIM), 0.1),
        'lambda_k1': nrm(ks[4], (L, DA_HEAD_DIM), 0.1),
        'lambda_q2': nrm(ks[5], (L, DA_HEAD_DIM), 0.1),
        'lambda_k2': nrm(ks[6], (L, DA_HEAD_DIM), 0.1),
        'subln_w': 1.0 + nrm(ks[7], (L, 2 * DA_HEAD_DIM), 0.01),
        'w_o_attn': nrm(ks[8], (L, D_ATTN, D_MODEL), D_ATTN ** -0.5),
        'b_glu': nrm(ks[9], (L, 2 * D_CONV), 0.01),
        'w_dw': nrm(ks[10], (L, CONV_WIDTH, D_CONV), CONV_WIDTH ** -0.5),
        'b_dw': nrm(ks[11], (L, D_CONV), 0.01),
        'ln_g': 1.0 + nrm(ks[12], (L, D_CONV), 0.01),
        'ln_b': nrm(ks[13], (L, D_CONV), 0.01),
        'w_pw2': nrm(ks[14], (L, D_CONV, D_MODEL), D_CONV ** -0.5),
        'b_pw2': nrm(ks[15], (L, D_MODEL), 0.01),
        'w_out': nrm(ks[16], (L, D_MODEL, D_MODEL), D_MODEL ** -0.5),
        'norm_post_w': 1.0 + nrm(ks[17], (L, D_MODEL), 0.01),
    }


def reference(x, norm_pre_w, w_in, lambda_q1, lambda_k1, lambda_q2, lambda_k2, subln_w,
              w_o_attn, b_glu, w_dw, b_dw, ln_g, ln_b, w_pw2, b_pw2, w_out, norm_post_w):
    B, S, _ = x.shape
    positions = jnp.arange(S)
    split_points = [int(p) for p in np.cumsum(SPLITS)[:-1]]
    for l in range(DEPTH):
        h = rms_norm(x, norm_pre_w[l], NORM_EPS)
        proj = h @ w_in[l]
        q, k, v, z_a, u_glu, z_c, ga, gc = jnp.split(proj, split_points, axis=-1)

        q = partial_rotary(q.reshape(B, S, DA_HEADS, 2, DA_HEAD_DIM), positions)
        k = partial_rotary(k.reshape(B, S, DA_HEADS, 2, DA_HEAD_DIM), positions)
        v = v.reshape(B, S, DA_HEADS, 2 * DA_HEAD_DIM)
        lam_init = lambda_init_fn(l)
        lam = (jnp.exp(jnp.sum(lambda_q1[l].astype(jnp.float32) * lambda_k1[l].astype(jnp.float32)))
               - jnp.exp(jnp.sum(lambda_q2[l].astype(jnp.float32) * lambda_k2[l].astype(jnp.float32)))
               + lam_init)
        o = diff_attention(q, k, v, lam)
        o = rms_norm(o, subln_w[l], SUBLN_EPS) * (1.0 - lam_init)
        o = o.reshape(B, S, D_ATTN) * jax.nn.silu(z_a)
        y_attn = o @ w_o_attn[l]

        u = u_glu + b_glu[l]
        u = u[..., :D_CONV] * jax.nn.sigmoid(u[..., D_CONV:])
        u = causal_depthwise_conv(u, w_dw[l], b_dw[l])
        u = layer_norm(u, ln_g[l], ln_b[l], LN_EPS)
        u = jax.nn.silu(u) * jax.nn.silu(z_c)
        y_conv = u @ w_pw2[l] + b_pw2[l]

        m = jax.nn.sigmoid(ga) * y_attn + jax.nn.sigmoid(gc) * y_conv
        out = m @ w_out[l]
        x = x + rms_norm(out, norm_post_w[l], NORM_EPS)
    return x
```

```python
import functools
import math

import jax
import jax.numpy as jnp
from jax import lax
from jax.experimental import pallas as pl
from jax.experimental.pallas import tpu as pltpu

D_MODEL = 2048
DA_HEADS = 8
DA_HEAD_DIM = 64
HEAD_W = 2 * DA_HEAD_DIM
D_ATTN = DA_HEADS * HEAD_W
ROPE_THETA = 500000.0
ROT_DIM = DA_HEAD_DIM // 4
D_CONV = 1024
CONV_WIDTH = 31
NORM_EPS = 1e-6
SUBLN_EPS = 1e-5
LN_EPS = 1e-5
LAMBDA_INIT = 0.8 - 0.6 * math.exp(-0.3 * 0)
D_IN = D_ATTN * 4 + 2 * D_CONV + D_CONV + 2 * D_MODEL

COL_BLK = 1024
Q_COL, K_COL, V_COL, ZA_COL, GLU_A_COL, GLU_G_COL, ZC_COL, GA_COL, GC_COL = 0, 1, 2, 3, 4, 5, 6, 7, 9
HEADS_PER_COL_BLK = COL_BLK // HEAD_W

PROJ_TM = 1024
PROJ_TN = COL_BLK
ATTN_TQ = 512
ATTN_TK = 512
CONV_T = 256
CONV_HALO = 32
CONV_ROWS = 64
MERGE_TM = 256
LANES = 128
MASK_VALUE = -1e30

VMEM_LIMIT_BYTES = 56 * 1024 * 1024

_BF16 = jnp.bfloat16
_F32 = jnp.float32


def _rotary(acc, cos, sa, sb):
    half = ROT_DIM // 2
    up = pltpu.roll(acc, LANES - half, 1)
    down = pltpu.roll(acc, half, 1)
    return acc * cos + up * sa + down * sb


def _proj_kernel(x_ref, nw_ref, w_ref, cos_ref, sa_ref, sb_ref, out_ref, h_ref):
    j = pl.program_id(1)

    @pl.when(j == 0)
    def _():
        rows = 256
        for r in range(0, PROJ_TM, rows):
            x = x_ref[r:r + rows, :]
            ms = jnp.mean(x * x, axis=-1, keepdims=True)
            h = x * lax.rsqrt(ms + NORM_EPS) * nw_ref[...]
            h_ref[r:r + rows, :] = h.astype(_BF16)

    acc = jnp.dot(h_ref[...], w_ref[...], preferred_element_type=_F32)

    def store_rotated(scale):
        cos, sa, sb = cos_ref[...], sa_ref[...], sb_ref[...]
        for hh in range(HEADS_PER_COL_BLK):
            sl = slice(hh * HEAD_W, (hh + 1) * HEAD_W)
            r = _rotary(acc[:, sl], cos, sa, sb)
            if scale != 1.0:
                r = r * scale
            out_ref[:, sl] = r.astype(_BF16)

    @pl.when(j == Q_COL)
    def _():
        store_rotated(DA_HEAD_DIM ** -0.5)

    @pl.when(j == K_COL)
    def _():
        store_rotated(1.0)

    @pl.when(j > K_COL)
    def _():
        out_ref[...] = acc.astype(_BF16)


def _proj(x2, norm_w, w_in_bf, cos_t, sa_t, sb_t, seq):
    m = x2.shape[0]
    seq_blocks = seq // PROJ_TM
    tab_spec = pl.BlockSpec((PROJ_TM, HEAD_W), lambda i, j: (i % seq_blocks, 0))
    return pl.pallas_call(
        _proj_kernel,
        out_shape=jax.ShapeDtypeStruct((m, D_IN), _BF16),
        grid=(m // PROJ_TM, D_IN // PROJ_TN),
        in_specs=[
            pl.BlockSpec((PROJ_TM, D_MODEL), lambda i, j: (i, 0)),
            pl.BlockSpec((1, D_MODEL), lambda i, j: (0, 0)),
            pl.BlockSpec((D_MODEL, PROJ_TN), lambda i, j: (0, j)),
            tab_spec, tab_spec, tab_spec,
        ],
        out_specs=pl.BlockSpec((PROJ_TM, PROJ_TN), lambda i, j: (i, j)),
        scratch_shapes=[pltpu.VMEM((PROJ_TM, D_MODEL), _BF16)],
        compiler_params=pltpu.CompilerParams(
            dimension_semantics=("arbitrary", "arbitrary"),
            vmem_limit_bytes=VMEM_LIMIT_BYTES),
        name="proj",
    )(x2, norm_w, w_in_bf, cos_t, sa_t, sb_t)


def _attn_kernel(q_ref, k_ref, v_ref, z_ref, lam_ref, sw_ref, o_ref, qq_ref, m_ref, l_ref, acc_ref):
    qi = pl.program_id(2)
    tq, tk = ATTN_TQ, ATTN_TK

    q = q_ref[...]
    lane = lax.broadcasted_iota(jnp.int32, (tq, HEAD_W), 1)
    zero = jnp.zeros_like(q)
    qq_ref[0:tq, :] = jnp.where(lane < DA_HEAD_DIM, q, zero)
    qq_ref[tq:2 * tq, :] = jnp.where(lane >= DA_HEAD_DIM, q, zero)
    m_ref[...] = jnp.full(m_ref.shape, MASK_VALUE, _F32)
    l_ref[...] = jnp.zeros(l_ref.shape, _F32)
    acc_ref[...] = jnp.zeros(acc_ref.shape, _F32)

    def step(kb, masked):
        k = k_ref[pl.ds(pl.multiple_of(kb * tk, tk), tk), :]
        v = v_ref[pl.ds(pl.multiple_of(kb * tk, tk), tk), :]
        s = lax.dot_general(qq_ref[...], k, (((1,), (1,)), ((), ())),
                            preferred_element_type=_F32)
        if masked:
            row = lax.broadcasted_iota(jnp.int32, (2 * tq, tk), 0)
            col = lax.broadcasted_iota(jnp.int32, (2 * tq, tk), 1)
            row = jnp.where(row >= tq, row - tq, row)
            s = jnp.where(row >= col, s, MASK_VALUE)
        m_prev = m_ref[...]
        m_new = jnp.maximum(m_prev, jnp.max(s, axis=1, keepdims=True))
        alpha = jnp.exp(m_prev - m_new)
        p = jnp.exp(s - m_new)
        l_ref[...] = alpha * l_ref[...] + jnp.sum(p, axis=1, keepdims=True)
        acc_ref[...] = alpha * acc_ref[...] + jnp.dot(p.astype(_BF16), v, preferred_element_type=_F32)
        m_ref[...] = m_new

    def body(kb, carry):
        step(kb, masked=False)
        return carry

    lax.fori_loop(0, qi, body, 0)
    step(qi, masked=True)

    lam_p = lam_ref[...]
    s1 = jnp.sum(lam_p[0:1] * lam_p[1:2], axis=1, keepdims=True)
    s2 = jnp.sum(lam_p[2:3] * lam_p[3:4], axis=1, keepdims=True)
    lam = jnp.exp(s1) - jnp.exp(s2) + LAMBDA_INIT

    o = acc_ref[...] / l_ref[...]
    o = o[0:tq] - lam * o[tq:2 * tq]
    o = o * lax.rsqrt(jnp.mean(o * o, axis=-1, keepdims=True) + SUBLN_EPS) * sw_ref[...]
    o = o * (1.0 - LAMBDA_INIT)
    z = z_ref[...].astype(_F32)
    o_ref[...] = (o * (z * jax.nn.sigmoid(z))).astype(_BF16)


def _attn(proj, lam_params, subln_w, batch, seq):
    m = proj.shape[0]
    nq = seq // ATTN_TQ
    assert ATTN_TQ == ATTN_TK
    hb = HEADS_PER_COL_BLK
    return pl.pallas_call(
        _attn_kernel,
        out_shape=jax.ShapeDtypeStruct((m, D_ATTN), _BF16),
        grid=(batch, DA_HEADS, nq),
        in_specs=[
            pl.BlockSpec((ATTN_TQ, HEAD_W), lambda b, h, qi: (b * nq + qi, Q_COL * hb + h)),
            pl.BlockSpec((seq, HEAD_W), lambda b, h, qi: (b, K_COL * hb + h)),
            pl.BlockSpec((seq, HEAD_W), lambda b, h, qi: (b, V_COL * hb + h)),
            pl.BlockSpec((ATTN_TQ, HEAD_W), lambda b, h, qi: (b * nq + qi, ZA_COL * hb + h)),
            pl.BlockSpec((4, DA_HEAD_DIM), lambda b, h, qi: (0, 0)),
            pl.BlockSpec((1, HEAD_W), lambda b, h, qi: (0, 0)),
        ],
        out_specs=pl.BlockSpec((ATTN_TQ, HEAD_W), lambda b, h, qi: (b * nq + qi, h)),
        scratch_shapes=[
            pltpu.VMEM((2 * ATTN_TQ, HEAD_W), _BF16),
            pltpu.VMEM((2 * ATTN_TQ, 1), _F32),
            pltpu.VMEM((2 * ATTN_TQ, 1), _F32),
            pltpu.VMEM((2 * ATTN_TQ, HEAD_W), _F32),
        ],
        compiler_params=pltpu.CompilerParams(
            dimension_semantics=("arbitrary", "arbitrary", "arbitrary"),
            vmem_limit_bytes=VMEM_LIMIT_BYTES),
        name="attn",
    )(proj, proj, proj, proj, lam_params, subln_w)


def _conv_kernel(a_ref, g_ref, ah_ref, gh_ref, z_ref, bglu_ref, wdw_ref, bdw_ref, lng_ref, lnb_ref,
                 o_ref, stage_ref, y_ref):
    t = pl.program_id(1)
    ba = bglu_ref[:, 0:D_CONV]
    bg = bglu_ref[:, D_CONV:2 * D_CONV]

    def glu(a, g):
        return (a.astype(_F32) + ba) * jax.nn.sigmoid(g.astype(_F32) + bg)

    halo = glu(ah_ref[...], gh_ref[...])
    stage_ref[0:CONV_HALO, :] = jnp.where(t > 0, halo, jnp.zeros_like(halo))
    stage_ref[CONV_HALO:CONV_HALO + CONV_T, :] = glu(a_ref[...], g_ref[...])

    first = CONV_HALO - (CONV_WIDTH - 1)
    for c in range(D_CONV // LANES):
        ls = slice(c * LANES, (c + 1) * LANES)
        for r in range(0, CONV_T, CONV_ROWS):
            acc = jnp.broadcast_to(bdw_ref[:, ls], (CONV_ROWS, LANES))
            for j in range(CONV_WIDTH):
                acc = acc + stage_ref[r + first + j:r + first + j + CONV_ROWS, ls] * wdw_ref[j:j + 1, ls]
            y_ref[r:r + CONV_ROWS, ls] = acc

    y = y_ref[...]
    mu = jnp.mean(y, axis=-1, keepdims=True)
    yc = y - mu
    var = jnp.mean(yc * yc, axis=-1, keepdims=True)
    u = yc * lax.rsqrt(var + LN_EPS) * lng_ref[...] + lnb_ref[...]
    z = z_ref[...].astype(_F32)
    o_ref[...] = ((u * jax.nn.sigmoid(u)) * (z * jax.nn.sigmoid(z))).astype(_BF16)


def _conv(proj, b_glu, w_dw, b_dw, ln_g, ln_b, batch, seq):
    m = proj.shape[0]
    nt = seq // CONV_T
    halo_per_tile = CONV_T // CONV_HALO

    def halo_idx(col):
        return lambda b, t: (jnp.maximum((b * nt + t) * halo_per_tile - 1, 0), col * (COL_BLK // D_CONV))

    def vec(n):
        return pl.BlockSpec((1, n), lambda b, t: (0, 0))

    return pl.pallas_call(
        _conv_kernel,
        out_shape=jax.ShapeDtypeStruct((m, D_CONV), _BF16),
        grid=(batch, nt),
        in_specs=[
            pl.BlockSpec((CONV_T, D_CONV), lambda b, t: (b * nt + t, GLU_A_COL)),
            pl.BlockSpec((CONV_T, D_CONV), lambda b, t: (b * nt + t, GLU_G_COL)),
            pl.BlockSpec((CONV_HALO, D_CONV), halo_idx(GLU_A_COL)),
            pl.BlockSpec((CONV_HALO, D_CONV), halo_idx(GLU_G_COL)),
            pl.BlockSpec((CONV_T, D_CONV), lambda b, t: (b * nt + t, ZC_COL)),
            vec(2 * D_CONV),
            pl.BlockSpec((CONV_WIDTH, D_CONV), lambda b, t: (0, 0)),
            vec(D_CONV), vec(D_CONV), vec(D_CONV),
        ],
        out_specs=pl.BlockSpec((CONV_T, D_CONV), lambda b, t: (b * nt + t, 0)),
        scratch_shapes=[
            pltpu.VMEM((CONV_HALO + CONV_T, D_CONV), _F32),
            pltpu.VMEM((CONV_T, D_CONV), _F32),
        ],
        compiler_params=pltpu.CompilerParams(
            dimension_semantics=("arbitrary", "arbitrary"),
            vmem_limit_bytes=VMEM_LIMIT_BYTES),
        name="conv",
    )(proj, proj, proj, proj, proj, b_glu, w_dw, b_dw, ln_g, ln_b)


def _merge_kernel(x_ref, o_ref, u_ref, ga0_ref, ga1_ref, gc0_ref, gc1_ref, wo_ref, wp_ref, bp_ref,
                  wout_ref, nw_ref, out_ref):
    ya = jnp.dot(o_ref[...], wo_ref[...], preferred_element_type=_F32)
    yc = jnp.dot(u_ref[...], wp_ref[...], preferred_element_type=_F32) + bp_ref[...]
    ga = jnp.concatenate([ga0_ref[...], ga1_ref[...]], axis=1).astype(_F32)
    gc = jnp.concatenate([gc0_ref[...], gc1_ref[...]], axis=1).astype(_F32)
    mix = jax.nn.sigmoid(ga) * ya + jax.nn.sigmoid(gc) * yc
    out = jnp.dot(mix.astype(_BF16), wout_ref[...], preferred_element_type=_F32)
    ms = jnp.mean(out * out, axis=-1, keepdims=True)
    out_ref[...] = x_ref[...] + out * lax.rsqrt(ms + NORM_EPS) * nw_ref[...]


def _merge(x2, o, u, proj, w_o_bf, w_pw2_bf, b_pw2, w_out_bf, norm_w):
    m = x2.shape[0]

    def const(shape):
        return pl.BlockSpec(shape, lambda i: (0, 0), pipeline_mode=pl.Buffered(1))

    def gate(col):
        return pl.BlockSpec((MERGE_TM, COL_BLK), lambda i: (i, col))

    return pl.pallas_call(
        _merge_kernel,
        out_shape=jax.ShapeDtypeStruct((m, D_MODEL), _F32),
        grid=(m // MERGE_TM,),
        in_specs=[
            pl.BlockSpec((MERGE_TM, D_MODEL), lambda i: (i, 0)),
            pl.BlockSpec((MERGE_TM, D_ATTN), lambda i: (i, 0)),
            pl.BlockSpec((MERGE_TM, D_CONV), lambda i: (i, 0)),
            gate(GA_COL), gate(GA_COL + 1), gate(GC_COL), gate(GC_COL + 1),
            const((D_ATTN, D_MODEL)),
            const((D_CONV, D_MODEL)),
            const((1, D_MODEL)),
            const((D_MODEL, D_MODEL)),
            const((1, D_MODEL)),
        ],
        out_specs=pl.BlockSpec((MERGE_TM, D_MODEL), lambda i: (i, 0)),
        compiler_params=pltpu.CompilerParams(
            dimension_semantics=("arbitrary",),
            vmem_limit_bytes=VMEM_LIMIT_BYTES),
        name="merge",
    )(x2, o, u, proj, proj, proj, proj, w_o_bf, w_pw2_bf, b_pw2, w_out_bf, norm_w)


def _rotary_tables(seq):
    half = ROT_DIM // 2
    inv_freq = ROPE_THETA ** (-jnp.arange(0, ROT_DIM, 2, dtype=_F32) / ROT_DIM)
    ang = jnp.arange(seq).astype(_F32)[:, None] * inv_freq[None, :]
    cos, sin = jnp.cos(ang), jnp.sin(ang)
    ones = jnp.ones((seq, DA_HEAD_DIM - ROT_DIM), _F32)
    zeros = jnp.zeros((seq, DA_HEAD_DIM - ROT_DIM), _F32)
    zh = jnp.zeros((seq, half), _F32)
    cos_c = jnp.concatenate([cos, cos, ones], axis=1)
    sa_c = jnp.concatenate([-sin, zh, zeros], axis=1)
    sb_c = jnp.concatenate([zh, sin, zeros], axis=1)
    tile2 = lambda a: jnp.concatenate([a, a], axis=1)
    return tile2(cos_c), tile2(sa_c), tile2(sb_c)


def kernel(x, norm_pre_w, w_in, lambda_q1, lambda_k1, lambda_q2, lambda_k2, subln_w, w_o_attn, b_glu,
           w_dw, b_dw, ln_g, ln_b, w_pw2, b_pw2, w_out, norm_post_w):
    batch, seq, _ = x.shape
    assert norm_pre_w.shape[0] == 1, "single layer"
    assert seq % PROJ_TM == 0 and seq % ATTN_TQ == 0 and seq % CONV_T == 0
    x2 = x.reshape(batch * seq, D_MODEL)
    cos_t, sa_t, sb_t = _rotary_tables(seq)
    lam_params = jnp.concatenate([lambda_q1, lambda_k1, lambda_q2, lambda_k2], axis=0).astype(_F32)

    proj = _proj(x2, norm_pre_w, w_in[0].astype(_BF16), cos_t, sa_t, sb_t, seq)
    o = _attn(proj, lam_params, subln_w, batch, seq)
    u = _conv(proj, b_glu, w_dw[0], b_dw, ln_g, ln_b, batch, seq)
    out = _merge(x2, o, u, proj, w_o_attn[0].astype(_BF16), w_pw2[0].astype(_BF16), b_pw2,
                 w_out[0].astype(_BF16), norm_post_w)
    return out.reshape(batch, seq, D_MODEL)
```

```python
import functools
import math

import jax
import jax.numpy as jnp
from jax import lax
from jax.experimental import pallas as pl
from jax.experimental.pallas import tpu as pltpu

D_MODEL = 2048
DA_HEADS = 8
DA_HEAD_DIM = 64
HEAD_W = 2 * DA_HEAD_DIM
D_ATTN = DA_HEADS * HEAD_W
ROPE_THETA = 500000.0
ROT_DIM = DA_HEAD_DIM // 4
D_CONV = 1024
CONV_WIDTH = 31
NORM_EPS = 1e-6
SUBLN_EPS = 1e-5
LN_EPS = 1e-5
LAMBDA_INIT = 0.8 - 0.6 * math.exp(-0.3 * 0)
D_IN = D_ATTN * 4 + 2 * D_CONV + D_CONV + 2 * D_MODEL

COL_BLK = 1024
Q_COL, K_COL, V_COL, ZA_COL, GLU_A_COL, GLU_G_COL, ZC_COL, GA_COL, GC_COL = 0, 1, 2, 3, 4, 5, 6, 7, 9
HEADS_PER_COL_BLK = COL_BLK // HEAD_W

PROJ_TM = 1024
PROJ_TN = COL_BLK
ATTN_KT = 256
ATTN_QCH = 2
ATTN_TQ = ATTN_QCH * ATTN_KT
CONV_T = 256
CONV_HALO = 32
CONV_ROWS = 64
MERGE_TM = 256
LANES = 128
MASK_VALUE = -1e30

VMEM_LIMIT_BYTES = 56 * 1024 * 1024

_BF16 = jnp.bfloat16
_F32 = jnp.float32


def _rotary(acc, cos, sa, sb):
    half = ROT_DIM // 2
    up = pltpu.roll(acc, LANES - half, 1)
    down = pltpu.roll(acc, half, 1)
    return acc * cos + up * sa + down * sb


def _proj_kernel(x_ref, nw_ref, w_ref, cos_ref, sa_ref, sb_ref, out_ref, h_ref):
    j = pl.program_id(1)

    @pl.when(j == 0)
    def _():
        rows = 256
        for r in range(0, PROJ_TM, rows):
            x = x_ref[r:r + rows, :]
            ms = jnp.mean(x * x, axis=-1, keepdims=True)
            h = x * lax.rsqrt(ms + NORM_EPS) * nw_ref[...]
            h_ref[r:r + rows, :] = h.astype(_BF16)

    acc = jnp.dot(h_ref[...], w_ref[...], preferred_element_type=_F32)

    def store_rotated(scale):
        cos, sa, sb = cos_ref[...], sa_ref[...], sb_ref[...]
        for hh in range(HEADS_PER_COL_BLK):
            sl = slice(hh * HEAD_W, (hh + 1) * HEAD_W)
            r = _rotary(acc[:, sl], cos, sa, sb)
            if scale != 1.0:
                r = r * scale
            out_ref[:, sl] = r.astype(_BF16)

    @pl.when(j == Q_COL)
    def _():
        store_rotated(DA_HEAD_DIM ** -0.5)

    @pl.when(j == K_COL)
    def _():
        store_rotated(1.0)

    @pl.when(j > K_COL)
    def _():
        out_ref[...] = acc.astype(_BF16)


def _proj(x2, norm_w, w_in_bf, cos_t, sa_t, sb_t, seq):
    m = x2.shape[0]
    seq_blocks = seq // PROJ_TM
    tab_spec = pl.BlockSpec((PROJ_TM, HEAD_W), lambda i, j: (i % seq_blocks, 0))
    return pl.pallas_call(
        _proj_kernel,
        out_shape=jax.ShapeDtypeStruct((m, D_IN), _BF16),
        grid=(m // PROJ_TM, D_IN // PROJ_TN),
        in_specs=[
            pl.BlockSpec((PROJ_TM, D_MODEL), lambda i, j: (i, 0)),
            pl.BlockSpec((1, D_MODEL), lambda i, j: (0, 0)),
            pl.BlockSpec((D_MODEL, PROJ_TN), lambda i, j: (0, j)),
            tab_spec, tab_spec, tab_spec,
        ],
        out_specs=pl.BlockSpec((PROJ_TM, PROJ_TN), lambda i, j: (i, j)),
        scratch_shapes=[pltpu.VMEM((PROJ_TM, D_MODEL), _BF16)],
        compiler_params=pltpu.CompilerParams(
            dimension_semantics=("arbitrary", "arbitrary"),
            vmem_limit_bytes=VMEM_LIMIT_BYTES),
        name="proj",
    )(x2, norm_w, w_in_bf, cos_t, sa_t, sb_t)


def _attn_kernel(q_ref, k_ref, v_ref, z_ref, lam_ref, sw_ref, o_ref, qqt_ref, vt_ref, s_ref, m_ref, l_ref,
                 acc_ref):
    qi = pl.program_id(2)
    tq, kt = ATTN_TQ, ATTN_KT
    nchunk = 2 * ATTN_QCH

    @pl.when(qi == 0)
    def _():
        for j in range(vt_ref.shape[0]):
            vt_ref[j] = v_ref[j * kt:(j + 1) * kt, :].astype(_F32).T.astype(_BF16)

    qt = q_ref[...].astype(_F32).T
    row = lax.broadcasted_iota(jnp.int32, (HEAD_W, tq), 0)
    qqt_ref[:, 0:tq] = jnp.where(row < DA_HEAD_DIM, qt, 0.0).astype(_BF16)
    qqt_ref[:, tq:2 * tq] = jnp.where(row >= DA_HEAD_DIM, qt, 0.0).astype(_BF16)
    m_ref[...] = jnp.full(m_ref.shape, MASK_VALUE, _F32)
    l_ref[...] = jnp.zeros(l_ref.shape, _F32)
    acc_ref[...] = jnp.zeros(acc_ref.shape, _F32)

    def scores(j, slot, col_ranges):
        k_sub = k_ref[pl.ds(pl.multiple_of(j * kt, kt), kt), :]
        for c0, w in col_ranges:
            s_ref[slot, :, c0:c0 + w] = jnp.dot(k_sub, qqt_ref[:, c0:c0 + w], preferred_element_type=_F32)

    def softmax_pv(j, slot, col_ranges):
        vt_sub = vt_ref[j]
        for c0, w, masked in col_ranges:
            cs = slice(c0, c0 + w)
            s = s_ref[slot, :, cs]
            if masked:
                kv = lax.broadcasted_iota(jnp.int32, (kt, w), 0)
                qq = lax.broadcasted_iota(jnp.int32, (kt, w), 1)
                s = jnp.where(kv <= qq, s, MASK_VALUE)
            m_prev = m_ref[:, cs]
            m_new = jnp.maximum(m_prev, jnp.max(s, axis=0, keepdims=True))
            alpha = jnp.exp(m_prev - m_new)
            p = jnp.exp(s - m_new)
            l_ref[:, cs] = alpha * l_ref[:, cs] + jnp.sum(p, axis=0, keepdims=True)
            acc_ref[:, cs] = alpha * acc_ref[:, cs] + jnp.dot(vt_sub, p.astype(_BF16),
                                                              preferred_element_type=_F32)
            m_ref[:, cs] = m_new

    all_cols = [(0, 2 * tq)]
    full = [(0, 2 * tq, False)]

    scores(0, 0, all_cols)

    def body(i, carry):
        j = 2 * i
        scores(j + 1, 1, all_cols)
        softmax_pv(j, 0, full)
        scores(j + 2, 0, all_cols)
        softmax_pv(j + 1, 1, full)
        return carry

    nfull = ATTN_QCH * qi
    lax.fori_loop(0, nfull // 2, body, 0)
    for d in range(ATTN_QCH):
        chunks = [c for c in range(nchunk) if c % ATTN_QCH >= d]
        if d + 1 < ATTN_QCH:
            nxt = [(c * kt, kt) for c in range(nchunk) if c % ATTN_QCH >= d + 1]
            scores(nfull + d + 1, (d + 1) % 2, nxt)
        softmax_pv(nfull + d, d % 2, [(c * kt, kt, c % ATTN_QCH == d) for c in chunks])

    lam_p = lam_ref[...]
    s1 = jnp.sum(lam_p[0:1] * lam_p[1:2], axis=1, keepdims=True)
    s2 = jnp.sum(lam_p[2:3] * lam_p[3:4], axis=1, keepdims=True)
    lam = jnp.exp(s1) - jnp.exp(s2) + LAMBDA_INIT

    ot = acc_ref[...] / l_ref[...]
    ot = ot[:, 0:tq] - lam * ot[:, tq:2 * tq]
    ot = ot * lax.rsqrt(jnp.mean(ot * ot, axis=0, keepdims=True) + SUBLN_EPS)
    o = ot.T * sw_ref[...] * (1.0 - LAMBDA_INIT)
    z = z_ref[...].astype(_F32)
    o_ref[...] = (o * (z * jax.nn.sigmoid(z))).astype(_BF16)


def _attn(proj, lam_params, subln_w, batch, seq):
    m = proj.shape[0]
    nq = seq // ATTN_TQ
    assert ATTN_TQ == ATTN_QCH * ATTN_KT and seq % ATTN_KT == 0 and ATTN_QCH % 2 == 0
    hb = HEADS_PER_COL_BLK
    return pl.pallas_call(
        _attn_kernel,
        out_shape=jax.ShapeDtypeStruct((m, D_ATTN), _BF16),
        grid=(batch, DA_HEADS, nq),
        in_specs=[
            pl.BlockSpec((ATTN_TQ, HEAD_W), lambda b, h, qi: (b * nq + qi, Q_COL * hb + h)),
            pl.BlockSpec((seq, HEAD_W), lambda b, h, qi: (b, K_COL * hb + h)),
            pl.BlockSpec((seq, HEAD_W), lambda b, h, qi: (b, V_COL * hb + h)),
            pl.BlockSpec((ATTN_TQ, HEAD_W), lambda b, h, qi: (b * nq + qi, ZA_COL * hb + h)),
            pl.BlockSpec((4, DA_HEAD_DIM), lambda b, h, qi: (0, 0)),
            pl.BlockSpec((1, HEAD_W), lambda b, h, qi: (0, 0)),
        ],
        out_specs=pl.BlockSpec((ATTN_TQ, HEAD_W), lambda b, h, qi: (b * nq + qi, h)),
        scratch_shapes=[
            pltpu.VMEM((HEAD_W, 2 * ATTN_TQ), _BF16),
            pltpu.VMEM((seq // ATTN_KT, HEAD_W, ATTN_KT), _BF16),
            pltpu.VMEM((2, ATTN_KT, 2 * ATTN_TQ), _F32),
            pltpu.VMEM((1, 2 * ATTN_TQ), _F32),
            pltpu.VMEM((1, 2 * ATTN_TQ), _F32),
            pltpu.VMEM((HEAD_W, 2 * ATTN_TQ), _F32),
        ],
        compiler_params=pltpu.CompilerParams(
            dimension_semantics=("arbitrary", "arbitrary", "arbitrary"),
            vmem_limit_bytes=VMEM_LIMIT_BYTES),
        name="attn",
    )(proj, proj, proj, proj, lam_params, subln_w)


def _conv_kernel(a_ref, g_ref, ah_ref, gh_ref, z_ref, bglu_ref, wdw_ref, bdw_ref, lng_ref, lnb_ref,
                 o_ref, stage_ref, y_ref):
    t = pl.program_id(1)
    ba = bglu_ref[:, 0:D_CONV]
    bg = bglu_ref[:, D_CONV:2 * D_CONV]

    def glu(a, g):
        return (a.astype(_F32) + ba) * jax.nn.sigmoid(g.astype(_F32) + bg)

    halo = glu(ah_ref[...], gh_ref[...])
    stage_ref[0:CONV_HALO, :] = jnp.where(t > 0, halo, jnp.zeros_like(halo))
    stage_ref[CONV_HALO:CONV_HALO + CONV_T, :] = glu(a_ref[...], g_ref[...])

    first = CONV_HALO - (CONV_WIDTH - 1)
    for c in range(D_CONV // LANES):
        ls = slice(c * LANES, (c + 1) * LANES)
        for r in range(0, CONV_T, CONV_ROWS):
            acc = jnp.broadcast_to(bdw_ref[:, ls], (CONV_ROWS, LANES))
            for j in range(CONV_WIDTH):
                acc = acc + stage_ref[r + first + j:r + first + j + CONV_ROWS, ls] * wdw_ref[j:j + 1, ls]
            y_ref[r:r + CONV_ROWS, ls] = acc

    y = y_ref[...]
    mu = jnp.mean(y, axis=-1, keepdims=True)
    yc = y - mu
    var = jnp.mean(yc * yc, axis=-1, keepdims=True)
    u = yc * lax.rsqrt(var + LN_EPS) * lng_ref[...] + lnb_ref[...]
    z = z_ref[...].astype(_F32)
    o_ref[...] = ((u * jax.nn.sigmoid(u)) * (z * jax.nn.sigmoid(z))).astype(_BF16)


def _conv(proj, b_glu, w_dw, b_dw, ln_g, ln_b, batch, seq):
    m = proj.shape[0]
    nt = seq // CONV_T
    halo_per_tile = CONV_T // CONV_HALO

    def halo_idx(col):
        return lambda b, t: (jnp.maximum((b * nt + t) * halo_per_tile - 1, 0), col * (COL_BLK // D_CONV))

    def vec(n):
        return pl.BlockSpec((1, n), lambda b, t: (0, 0))

    return pl.pallas_call(
        _conv_kernel,
        out_shape=jax.ShapeDtypeStruct((m, D_CONV), _BF16),
        grid=(batch, nt),
        in_specs=[
            pl.BlockSpec((CONV_T, D_CONV), lambda b, t: (b * nt + t, GLU_A_COL)),
            pl.BlockSpec((CONV_T, D_CONV), lambda b, t: (b * nt + t, GLU_G_COL)),
            pl.BlockSpec((CONV_HALO, D_CONV), halo_idx(GLU_A_COL)),
            pl.BlockSpec((CONV_HALO, D_CONV), halo_idx(GLU_G_COL)),
            pl.BlockSpec((CONV_T, D_CONV), lambda b, t: (b * nt + t, ZC_COL)),
            vec(2 * D_CONV),
            pl.BlockSpec((CONV_WIDTH, D_CONV), lambda b, t: (0, 0)),
            vec(D_CONV), vec(D_CONV), vec(D_CONV),
        ],
        out_specs=pl.BlockSpec((CONV_T, D_CONV), lambda b, t: (b * nt + t, 0)),
        scratch_shapes=[
            pltpu.VMEM((CONV_HALO + CONV_T, D_CONV), _F32),
            pltpu.VMEM((CONV_T, D_CONV), _F32),
        ],
        compiler_params=pltpu.CompilerParams(
            dimension_semantics=("arbitrary", "arbitrary"),
            vmem_limit_bytes=VMEM_LIMIT_BYTES),
        name="conv",
    )(proj, proj, proj, proj, proj, b_glu, w_dw, b_dw, ln_g, ln_b)


def _merge_kernel(x_ref, o_ref, u_ref, ga0_ref, ga1_ref, gc0_ref, gc1_ref, wo_ref, wp_ref, bp_ref,
                  wout_ref, nw_ref, out_ref):
    ya = jnp.dot(o_ref[...], wo_ref[...], preferred_element_type=_F32)
    yc = jnp.dot(u_ref[...], wp_ref[...], preferred_element_type=_F32) + bp_ref[...]
    ga = jnp.concatenate([ga0_ref[...], ga1_ref[...]], axis=1).astype(_F32)
    gc = jnp.concatenate([gc0_ref[...], gc1_ref[...]], axis=1).astype(_F32)
    mix = jax.nn.sigmoid(ga) * ya + jax.nn.sigmoid(gc) * yc
    out = jnp.dot(mix.astype(_BF16), wout_ref[...], preferred_element_type=_F32)
    ms = jnp.mean(out * out, axis=-1, keepdims=True)
    out_ref[...] = x_ref[...] + out * lax.rsqrt(ms + NORM_EPS) * nw_ref[...]


def _merge(x2, o, u, proj, w_o_bf, w_pw2_bf, b_pw2, w_out_bf, norm_w):
    m = x2.shape[0]

    def const(shape):
        return pl.BlockSpec(shape, lambda i: (0, 0), pipeline_mode=pl.Buffered(1))

    def gate(col):
        return pl.BlockSpec((MERGE_TM, COL_BLK), lambda i: (i, col))

    return pl.pallas_call(
        _merge_kernel,
        out_shape=jax.ShapeDtypeStruct((m, D_MODEL), _F32),
        grid=(m // MERGE_TM,),
        in_specs=[
            pl.BlockSpec((MERGE_TM, D_MODEL), lambda i: (i, 0)),
            pl.BlockSpec((MERGE_TM, D_ATTN), lambda i: (i, 0)),
            pl.BlockSpec((MERGE_TM, D_CONV), lambda i: (i, 0)),
            gate(GA_COL), gate(GA_COL + 1), gate(GC_COL), gate(GC_COL + 1),
            const((D_ATTN, D_MODEL)),
            const((D_CONV, D_MODEL)),
            const((1, D_MODEL)),
            const((D_MODEL, D_MODEL)),
            const((1, D_MODEL)),
        ],
        out_specs=pl.BlockSpec((MERGE_TM, D_MODEL), lambda i: (i, 0)),
        compiler_params=pltpu.CompilerParams(
            dimension_semantics=("arbitrary",),
            vmem_limit_bytes=VMEM_LIMIT_BYTES),
        name="merge",
    )(x2, o, u, proj, proj, proj, proj, w_o_bf, w_pw2_bf, b_pw2, w_out_bf, norm_w)


def _rotary_tables(seq):
    half = ROT_DIM // 2
    inv_freq = ROPE_THETA ** (-jnp.arange(0, ROT_DIM, 2, dtype=_F32) / ROT_DIM)
    ang = jnp.arange(seq).astype(_F32)[:, None] * inv_freq[None, :]
    cos, sin = jnp.cos(ang), jnp.sin(ang)
    ones = jnp.ones((seq, DA_HEAD_DIM - ROT_DIM), _F32)
    zeros = jnp.zeros((seq, DA_HEAD_DIM - ROT_DIM), _F32)
    zh = jnp.zeros((seq, half), _F32)
    cos_c = jnp.concatenate([cos, cos, ones], axis=1)
    sa_c = jnp.concatenate([-sin, zh, zeros], axis=1)
    sb_c = jnp.concatenate([zh, sin, zeros], axis=1)
    tile2 = lambda a: jnp.concatenate([a, a], axis=1)
    return tile2(cos_c), tile2(sa_c), tile2(sb_c)


def kernel(x, norm_pre_w, w_in, lambda_q1, lambda_k1, lambda_q2, lambda_k2, subln_w, w_o_attn, b_glu,
           w_dw, b_dw, ln_g, ln_b, w_pw2, b_pw2, w_out, norm_post_w):
    batch, seq, _ = x.shape
    assert norm_pre_w.shape[0] == 1, "single layer"
    assert seq % PROJ_TM == 0 and seq % ATTN_TQ == 0 and seq % CONV_T == 0
    x2 = x.reshape(batch * seq, D_MODEL)
    cos_t, sa_t, sb_t = _rotary_tables(seq)
    lam_params = jnp.concatenate([lambda_q1, lambda_k1, lambda_q2, lambda_k2], axis=0).astype(_F32)

    proj = _proj(x2, norm_pre_w, w_in[0].astype(_BF16), cos_t, sa_t, sb_t, seq)
    o = _attn(proj, lam_params, subln_w, batch, seq)
    u = _conv(proj, b_glu, w_dw[0], b_dw, ln_g, ln_b, batch, seq)
    out = _merge(x2, o, u, proj, w_o_attn[0].astype(_BF16), w_pw2[0].astype(_BF16), b_pw2,
                 w_out[0].astype(_BF16), norm_post_w)
    return out.reshape(batch, seq, D_MODEL)
```

```python
import functools
import math

import jax
import jax.numpy as jnp
from jax import lax
from jax.experimental import pallas as pl
from jax.experimental.pallas import tpu as pltpu

D_MODEL = 2048
DA_HEADS = 8
DA_HEAD_DIM = 64
HEAD_W = 2 * DA_HEAD_DIM
D_ATTN = DA_HEADS * HEAD_W
ROPE_THETA = 500000.0
ROT_DIM = DA_HEAD_DIM // 4
D_CONV = 1024
CONV_WIDTH = 31
NORM_EPS = 1e-6
SUBLN_EPS = 1e-5
LN_EPS = 1e-5
LAMBDA_INIT = 0.8 - 0.6 * math.exp(-0.3 * 0)
D_IN = D_ATTN * 4 + 2 * D_CONV + D_CONV + 2 * D_MODEL

COL_BLK = 1024
Q_COL, K_COL, V_COL, ZA_COL, GLU_A_COL, GLU_G_COL, ZC_COL, GA_COL, GC_COL = 0, 1, 2, 3, 4, 5, 6, 7, 9
HEADS_PER_COL_BLK = COL_BLK // HEAD_W

PROJ_TM = 1024
PROJ_TN = COL_BLK
PROJ_ROWS = 256
Q_SCALE = DA_HEAD_DIM ** -0.5 * math.log2(math.e)
ATTN_KT = 256
ATTN_QCH = 2
ATTN_TQ = ATTN_QCH * ATTN_KT
CONV_T = 256
CONV_HALO = 32
CONV_ROWS = 64
MERGE_TM = 256
LANES = 128
SUBLANES = 8
MASK_VALUE = -1e30

VMEM_LIMIT_BYTES = 56 * 1024 * 1024

_BF16 = jnp.bfloat16
_F32 = jnp.float32


def _rotary(acc, cos, sa, sb):
    half = ROT_DIM // 2
    up = pltpu.roll(acc, LANES - half, 1)
    down = pltpu.roll(acc, half, 1)
    return acc * cos + up * sa + down * sb


def _proj_kernel(x_ref, nw_ref, w_ref, cos_ref, sa_ref, sb_ref, out_ref, h_ref):
    j = pl.program_id(1)
    row_chunks = [slice(r, r + PROJ_ROWS) for r in range(0, PROJ_TM, PROJ_ROWS)]

    def normed(rs):
        x = x_ref[rs, :]
        ms = jnp.mean(x * x, axis=-1, keepdims=True)
        return (x * lax.rsqrt(ms + NORM_EPS) * nw_ref[...]).astype(_BF16)

    def store_rotated(acc, rs, scale):
        cos, sa, sb = cos_ref[rs, :], sa_ref[rs, :], sb_ref[rs, :]
        for hh in range(HEADS_PER_COL_BLK):
            sl = slice(hh * HEAD_W, (hh + 1) * HEAD_W)
            r = _rotary(acc[:, sl], cos, sa, sb)
            if scale != 1.0:
                r = r * scale
            out_ref[rs, sl] = r.astype(_BF16)

    @pl.when(j == Q_COL)
    def _():
        for rs in row_chunks:
            h = normed(rs)
            h_ref[rs, :] = h
            store_rotated(jnp.dot(h, w_ref[...], preferred_element_type=_F32), rs, Q_SCALE)

    @pl.when(j == K_COL)
    def _():
        for rs in row_chunks:
            store_rotated(jnp.dot(h_ref[rs, :], w_ref[...], preferred_element_type=_F32), rs, 1.0)

    @pl.when(j > K_COL)
    def _():
        out_ref[...] = jnp.dot(h_ref[...], w_ref[...], preferred_element_type=_F32).astype(_BF16)


def _proj(x2, norm_w, w_in_bf, cos_t, sa_t, sb_t, seq):
    m = x2.shape[0]
    seq_blocks = seq // PROJ_TM
    tab_spec = pl.BlockSpec((PROJ_TM, HEAD_W), lambda i, j: (i % seq_blocks, 0))
    return pl.pallas_call(
        _proj_kernel,
        out_shape=jax.ShapeDtypeStruct((m, D_IN), _BF16),
        grid=(m // PROJ_TM, D_IN // PROJ_TN),
        in_specs=[
            pl.BlockSpec((PROJ_TM, D_MODEL), lambda i, j: (i, 0)),
            pl.BlockSpec((1, D_MODEL), lambda i, j: (0, 0)),
            pl.BlockSpec((D_MODEL, PROJ_TN), lambda i, j: (0, j)),
            tab_spec, tab_spec, tab_spec,
        ],
        out_specs=pl.BlockSpec((PROJ_TM, PROJ_TN), lambda i, j: (i, j)),
        scratch_shapes=[pltpu.VMEM((PROJ_TM, D_MODEL), _BF16)],
        compiler_params=pltpu.CompilerParams(
            dimension_semantics=("arbitrary", "arbitrary"),
            vmem_limit_bytes=VMEM_LIMIT_BYTES),
        name="proj",
    )(x2, norm_w, w_in_bf, cos_t, sa_t, sb_t)


def _attn_kernel(q_ref, k_ref, v_ref, z_ref, lam_ref, sw_ref, o_ref, qqt_ref, vt_ref, s_ref, m_ref, l_ref,
                 acc_ref):
    qi = pl.program_id(2)
    tq, kt = ATTN_TQ, ATTN_KT
    nchunk = 2 * ATTN_QCH

    @pl.when(qi == 0)
    def _():
        for j in range(vt_ref.shape[0]):
            vt_ref[j] = v_ref[j * kt:(j + 1) * kt, :].astype(_F32).T.astype(_BF16)

    qt = q_ref[...].astype(_F32).T
    row = lax.broadcasted_iota(jnp.int32, (HEAD_W, tq), 0)
    qqt_ref[:, 0:tq] = jnp.where(row < DA_HEAD_DIM, qt, 0.0).astype(_BF16)
    qqt_ref[:, tq:2 * tq] = jnp.where(row >= DA_HEAD_DIM, qt, 0.0).astype(_BF16)
    m_ref[...] = jnp.full(m_ref.shape, MASK_VALUE, _F32)
    l_ref[...] = jnp.zeros(l_ref.shape, _F32)
    acc_ref[...] = jnp.zeros(acc_ref.shape, _F32)

    def scores(j, slot, col_ranges):
        k_sub = k_ref[pl.ds(pl.multiple_of(j * kt, kt), kt), :]
        for c0, w in col_ranges:
            s_ref[slot, :, c0:c0 + w] = jnp.dot(k_sub, qqt_ref[:, c0:c0 + w], preferred_element_type=_F32)

    def softmax_pv(j, slot, col_ranges):
        vt_sub = vt_ref[j]
        for c0, w, masked in col_ranges:
            cs = slice(c0, c0 + w)
            s = s_ref[slot, :, cs]
            if masked:
                kv = lax.broadcasted_iota(jnp.int32, (kt, w), 0)
                qq = lax.broadcasted_iota(jnp.int32, (kt, w), 1)
                s = jnp.where(kv <= qq, s, MASK_VALUE)
            m_prev = m_ref[:, cs]
            m_new = jnp.maximum(m_prev, jnp.max(s, axis=0, keepdims=True))
            alpha = jnp.exp2(m_prev - m_new)
            p = jnp.exp2(s - m_new)
            l_ref[:, cs] = alpha * l_ref[:, cs] + jnp.sum(p, axis=0, keepdims=True)
            acc_ref[:, cs] = alpha * acc_ref[:, cs] + jnp.dot(vt_sub, p.astype(_BF16),
                                                              preferred_element_type=_F32)
            m_ref[:, cs] = m_new

    all_cols = [(0, 2 * tq)]
    full = [(0, 2 * tq, False)]

    scores(0, 0, all_cols)

    def body(i, carry):
        j = 2 * i
        scores(j + 1, 1, all_cols)
        softmax_pv(j, 0, full)
        scores(j + 2, 0, all_cols)
        softmax_pv(j + 1, 1, full)
        return carry

    nfull = ATTN_QCH * qi
    lax.fori_loop(0, nfull // 2, body, 0)
    for d in range(ATTN_QCH):
        chunks = [c for c in range(nchunk) if c % ATTN_QCH >= d]
        if d + 1 < ATTN_QCH:
            nxt = [(c * kt, kt) for c in range(nchunk) if c % ATTN_QCH >= d + 1]
            scores(nfull + d + 1, (d + 1) % 2, nxt)
        softmax_pv(nfull + d, d % 2, [(c * kt, kt, c % ATTN_QCH == d) for c in chunks])

    lam_p = lam_ref[...]
    s1 = jnp.sum(lam_p[0:1] * lam_p[1:2], axis=1, keepdims=True)
    s2 = jnp.sum(lam_p[2:3] * lam_p[3:4], axis=1, keepdims=True)
    lam = jnp.exp(s1) - jnp.exp(s2) + LAMBDA_INIT

    ot = acc_ref[...] / l_ref[...]
    ot = ot[:, 0:tq] - lam * ot[:, tq:2 * tq]
    ot = ot * lax.rsqrt(jnp.mean(ot * ot, axis=0, keepdims=True) + SUBLN_EPS)
    o = ot.T * sw_ref[...] * (1.0 - LAMBDA_INIT)
    z = z_ref[...].astype(_F32)
    o_ref[...] = (o * (z * jax.nn.sigmoid(z))).astype(_BF16)


def _attn(proj, lam_params, subln_w, batch, seq):
    m = proj.shape[0]
    nq = seq // ATTN_TQ
    assert ATTN_TQ == ATTN_QCH * ATTN_KT and seq % ATTN_KT == 0 and ATTN_QCH % 2 == 0
    hb = HEADS_PER_COL_BLK
    return pl.pallas_call(
        _attn_kernel,
        out_shape=jax.ShapeDtypeStruct((m, D_ATTN), _BF16),
        grid=(batch, DA_HEADS, nq),
        in_specs=[
            pl.BlockSpec((ATTN_TQ, HEAD_W), lambda b, h, qi: (b * nq + qi, Q_COL * hb + h)),
            pl.BlockSpec((seq, HEAD_W), lambda b, h, qi: (b, K_COL * hb + h)),
            pl.BlockSpec((seq, HEAD_W), lambda b, h, qi: (b, V_COL * hb + h)),
            pl.BlockSpec((ATTN_TQ, HEAD_W), lambda b, h, qi: (b * nq + qi, ZA_COL * hb + h)),
            pl.BlockSpec((4, DA_HEAD_DIM), lambda b, h, qi: (0, 0)),
            pl.BlockSpec((1, HEAD_W), lambda b, h, qi: (0, 0)),
        ],
        out_specs=pl.BlockSpec((ATTN_TQ, HEAD_W), lambda b, h, qi: (b * nq + qi, h)),
        scratch_shapes=[
            pltpu.VMEM((HEAD_W, 2 * ATTN_TQ), _BF16),
            pltpu.VMEM((seq // ATTN_KT, HEAD_W, ATTN_KT), _BF16),
            pltpu.VMEM((2, ATTN_KT, 2 * ATTN_TQ), _F32),
            pltpu.VMEM((1, 2 * ATTN_TQ), _F32),
            pltpu.VMEM((1, 2 * ATTN_TQ), _F32),
            pltpu.VMEM((HEAD_W, 2 * ATTN_TQ), _F32),
        ],
        compiler_params=pltpu.CompilerParams(
            dimension_semantics=("arbitrary", "arbitrary", "arbitrary"),
            vmem_limit_bytes=VMEM_LIMIT_BYTES),
        name="attn",
    )(proj, proj, proj, proj, lam_params, subln_w)


def _conv_kernel(a_ref, g_ref, ah_ref, gh_ref, z_ref, bglu_ref, wdw_ref, bdw_ref, lng_ref, lnb_ref,
                 o_ref, stage_ref, shift_ref, y_ref):
    t = pl.program_id(1)
    ba = bglu_ref[:, 0:D_CONV]
    bg = bglu_ref[:, D_CONV:2 * D_CONV]

    def glu(a, g):
        return (a.astype(_F32) + ba) * jax.nn.sigmoid(g.astype(_F32) + bg)

    halo = glu(ah_ref[...], gh_ref[...])
    halo = jnp.where(t > 0, halo, jnp.zeros_like(halo))
    cur = glu(a_ref[...], g_ref[...])
    nblk = D_CONV // LANES
    for c in range(nblk):
        ls = slice(c * LANES, (c + 1) * LANES)
        stage_ref[c, 0:CONV_HALO, :] = halo[:, ls]
        stage_ref[c, CONV_HALO:CONV_HALO + CONV_T, :] = cur[:, ls]

    first = CONV_HALO - (CONV_WIDTH - 1)
    shift_rows = shift_ref.shape[1]
    row_starts = range(0, CONV_T, CONV_ROWS)

    def channel_block(c, carry):
        for s in range(1, SUBLANES):
            shift_ref[s - 1] = stage_ref[c, s:s + shift_rows, :]
        accs = [jnp.broadcast_to(bdw_ref[c], (CONV_ROWS, LANES)) for _ in row_starts]
        for j in range(CONV_WIDTH):
            q, s = divmod(first + j, SUBLANES)
            w_j = jnp.broadcast_to(wdw_ref[c, j:j + 1, :], (CONV_ROWS, LANES))
            for i, r in enumerate(row_starts):
                lo = r + q * SUBLANES
                if s == 0:
                    win = stage_ref[c, lo:lo + CONV_ROWS, :]
                else:
                    win = shift_ref[s - 1, lo:lo + CONV_ROWS, :]
                accs[i] = accs[i] + win * w_j
        for i, r in enumerate(row_starts):
            y_ref[c, r:r + CONV_ROWS, :] = accs[i]
        return carry

    lax.fori_loop(0, nblk, channel_block, 0)

    y = jnp.concatenate([y_ref[c] for c in range(nblk)], axis=1)
    mu = jnp.mean(y, axis=-1, keepdims=True)
    yc = y - mu
    var = jnp.mean(yc * yc, axis=-1, keepdims=True)
    u = yc * lax.rsqrt(var + LN_EPS) * lng_ref[...] + lnb_ref[...]
    z = z_ref[...].astype(_F32)
    o_ref[...] = ((u * jax.nn.sigmoid(u)) * (z * jax.nn.sigmoid(z))).astype(_BF16)


def _conv(proj, b_glu, w_dw, b_dw, ln_g, ln_b, batch, seq):
    m = proj.shape[0]
    nt = seq // CONV_T
    halo_per_tile = CONV_T // CONV_HALO
    nblk = D_CONV // LANES
    w_blk = w_dw.reshape(CONV_WIDTH, nblk, LANES).transpose(1, 0, 2)
    b_blk = b_dw.reshape(nblk, 1, LANES)

    def halo_idx(col):
        return lambda b, t: (jnp.maximum((b * nt + t) * halo_per_tile - 1, 0), col * (COL_BLK // D_CONV))

    def vec(n):
        return pl.BlockSpec((1, n), lambda b, t: (0, 0))

    return pl.pallas_call(
        _conv_kernel,
        out_shape=jax.ShapeDtypeStruct((m, D_CONV), _BF16),
        grid=(batch, nt),
        in_specs=[
            pl.BlockSpec((CONV_T, D_CONV), lambda b, t: (b * nt + t, GLU_A_COL)),
            pl.BlockSpec((CONV_T, D_CONV), lambda b, t: (b * nt + t, GLU_G_COL)),
            pl.BlockSpec((CONV_HALO, D_CONV), halo_idx(GLU_A_COL)),
            pl.BlockSpec((CONV_HALO, D_CONV), halo_idx(GLU_G_COL)),
            pl.BlockSpec((CONV_T, D_CONV), lambda b, t: (b * nt + t, ZC_COL)),
            vec(2 * D_CONV),
            pl.BlockSpec((nblk, CONV_WIDTH, LANES), lambda b, t: (0, 0, 0)),
            pl.BlockSpec((nblk, 1, LANES), lambda b, t: (0, 0, 0)),
            vec(D_CONV), vec(D_CONV),
        ],
        out_specs=pl.BlockSpec((CONV_T, D_CONV), lambda b, t: (b * nt + t, 0)),
        scratch_shapes=[
            pltpu.VMEM((nblk, CONV_HALO + CONV_T, LANES), _F32),
            pltpu.VMEM((SUBLANES - 1, CONV_HALO + CONV_T - SUBLANES, LANES), _F32),
            pltpu.VMEM((nblk, CONV_T, LANES), _F32),
        ],
        compiler_params=pltpu.CompilerParams(
            dimension_semantics=("arbitrary", "arbitrary"),
            vmem_limit_bytes=VMEM_LIMIT_BYTES),
        name="conv",
    )(proj, proj, proj, proj, proj, b_glu, w_blk, b_blk, ln_g, ln_b)


def _merge_kernel(x_ref, o_ref, u_ref, ga0_ref, ga1_ref, gc0_ref, gc1_ref, wo_ref, wp_ref, bp_ref,
                  wout_ref, nw_ref, out_ref):
    ya = jnp.dot(o_ref[...], wo_ref[...], preferred_element_type=_F32)
    yc = jnp.dot(u_ref[...], wp_ref[...], preferred_element_type=_F32) + bp_ref[...]
    ga = jnp.concatenate([ga0_ref[...], ga1_ref[...]], axis=1).astype(_F32)
    gc = jnp.concatenate([gc0_ref[...], gc1_ref[...]], axis=1).astype(_F32)
    mix = jax.nn.sigmoid(ga) * ya + jax.nn.sigmoid(gc) * yc
    out = jnp.dot(mix.astype(_BF16), wout_ref[...], preferred_element_type=_F32)
    ms = jnp.mean(out * out, axis=-1, keepdims=True)
    out_ref[...] = x_ref[...] + out * lax.rsqrt(ms + NORM_EPS) * nw_ref[...]


def _merge(x2, o, u, proj, w_o_bf, w_pw2_bf, b_pw2, w_out_bf, norm_w):
    m = x2.shape[0]

    def const(shape):
        return pl.BlockSpec(shape, lambda i: (0, 0), pipeline_mode=pl.Buffered(1))

    def gate(col):
        return pl.BlockSpec((MERGE_TM, COL_BLK), lambda i: (i, col))

    return pl.pallas_call(
        _merge_kernel,
        out_shape=jax.ShapeDtypeStruct((m, D_MODEL), _F32),
        grid=(m // MERGE_TM,),
        in_specs=[
            pl.BlockSpec((MERGE_TM, D_MODEL), lambda i: (i, 0)),
            pl.BlockSpec((MERGE_TM, D_ATTN), lambda i: (i, 0)),
            pl.BlockSpec((MERGE_TM, D_CONV), lambda i: (i, 0)),
            gate(GA_COL), gate(GA_COL + 1), gate(GC_COL), gate(GC_COL + 1),
            const((D_ATTN, D_MODEL)),
            const((D_CONV, D_MODEL)),
            const((1, D_MODEL)),
            const((D_MODEL, D_MODEL)),
            const((1, D_MODEL)),
        ],
        out_specs=pl.BlockSpec((MERGE_TM, D_MODEL), lambda i: (i, 0)),
        compiler_params=pltpu.CompilerParams(
            dimension_semantics=("arbitrary",),
            vmem_limit_bytes=VMEM_LIMIT_BYTES),
        name="merge",
    )(x2, o, u, proj, proj, proj, proj, w_o_bf, w_pw2_bf, b_pw2, w_out_bf, norm_w)


def _rotary_tables(seq):
    half = ROT_DIM // 2
    inv_freq = ROPE_THETA ** (-jnp.arange(0, ROT_DIM, 2, dtype=_F32) / ROT_DIM)
    ang = jnp.arange(seq).astype(_F32)[:, None] * inv_freq[None, :]
    cos, sin = jnp.cos(ang), jnp.sin(ang)
    ones = jnp.ones((seq, DA_HEAD_DIM - ROT_DIM), _F32)
    zeros = jnp.zeros((seq, DA_HEAD_DIM - ROT_DIM), _F32)
    zh = jnp.zeros((seq, half), _F32)
    cos_c = jnp.concatenate([cos, cos, ones], axis=1)
    sa_c = jnp.concatenate([-sin, zh, zeros], axis=1)
    sb_c = jnp.concatenate([zh, sin, zeros], axis=1)
    tile2 = lambda a: jnp.concatenate([a, a], axis=1)
    return tile2(cos_c), tile2(sa_c), tile2(sb_c)


def kernel(x, norm_pre_w, w_in, lambda_q1, lambda_k1, lambda_q2, lambda_k2, subln_w, w_o_attn, b_glu,
           w_dw, b_dw, ln_g, ln_b, w_pw2, b_pw2, w_out, norm_post_w):
    batch, seq, _ = x.shape
    assert norm_pre_w.shape[0] == 1, "single layer"
    assert seq % PROJ_TM == 0 and seq % ATTN_TQ == 0 and seq % CONV_T == 0
    x2 = x.reshape(batch * seq, D_MODEL)
    cos_t, sa_t, sb_t = _rotary_tables(seq)
    lam_params = jnp.concatenate([lambda_q1, lambda_k1, lambda_q2, lambda_k2], axis=0).astype(_F32)

    proj = _proj(x2, norm_pre_w, w_in[0].astype(_BF16), cos_t, sa_t, sb_t, seq)
    o = _attn(proj, lam_params, subln_w, batch, seq)
    u = _conv(proj, b_glu, w_dw[0], b_dw, ln_g, ln_b, batch, seq)
    out = _merge(x2, o, u, proj, w_o_attn[0].astype(_BF16), w_pw2[0].astype(_BF16), b_pw2,
                 w_out[0].astype(_BF16), norm_post_w)
    return out.reshape(batch, seq, D_MODEL)
```

```python
import functools
import math

import jax
import jax.numpy as jnp
from jax import lax
from jax.experimental import pallas as pl
from jax.experimental.pallas import tpu as pltpu

D_MODEL = 2048
DA_HEADS = 8
DA_HEAD_DIM = 64
HEAD_W = 2 * DA_HEAD_DIM
D_ATTN = DA_HEADS * HEAD_W
ROPE_THETA = 500000.0
ROT_DIM = DA_HEAD_DIM // 4
D_CONV = 1024
CONV_WIDTH = 31
NORM_EPS = 1e-6
SUBLN_EPS = 1e-5
LN_EPS = 1e-5
LAMBDA_INIT = 0.8 - 0.6 * math.exp(-0.3 * 0)
D_IN = D_ATTN * 4 + 2 * D_CONV + D_CONV + 2 * D_MODEL

COL_BLK = 1024
Q_COL, K_COL, V_COL, ZA_COL, GLU_A_COL, GLU_G_COL, ZC_COL, GA_COL, GC_COL = 0, 1, 2, 3, 4, 5, 6, 7, 9
HEADS_PER_COL_BLK = COL_BLK // HEAD_W

PROJ_TM = 1024
PROJ_TN = COL_BLK
PROJ_ROWS = 256
Q_SCALE = DA_HEAD_DIM ** -0.5 * math.log2(math.e)
ATTN_KT = 256
ATTN_QCH = 2
ATTN_TQ = ATTN_QCH * ATTN_KT
CONV_T = 256
CONV_HALO = 32
CONV_ROWS = 64
OUT_BLK = 256
TAIL_CHUNKS = 8
LANES = 128
SUBLANES = 8
MASK_VALUE = -1e30

VMEM_LIMIT_BYTES = 56 * 1024 * 1024

_BF16 = jnp.bfloat16
_F32 = jnp.float32


def _sigmoid(x):
    return 0.5 * jnp.tanh(0.5 * x) + 0.5


def _rotary(acc, cos, sa, sb):
    half = ROT_DIM // 2
    up = pltpu.roll(acc, LANES - half, 1)
    down = pltpu.roll(acc, half, 1)
    return acc * cos + up * sa + down * sb


def _proj_kernel(x_ref, nw_ref, w_ref, cos_ref, sa_ref, sb_ref, out_ref, h_ref):
    j = pl.program_id(1)
    row_chunks = [slice(r, r + PROJ_ROWS) for r in range(0, PROJ_TM, PROJ_ROWS)]

    def normed(rs):
        x = x_ref[rs, :]
        ms = jnp.mean(x * x, axis=-1, keepdims=True)
        return (x * lax.rsqrt(ms + NORM_EPS) * nw_ref[...]).astype(_BF16)

    def store_rotated(acc, rs, scale):
        cos, sa, sb = cos_ref[rs, :], sa_ref[rs, :], sb_ref[rs, :]
        for hh in range(HEADS_PER_COL_BLK):
            sl = slice(hh * HEAD_W, (hh + 1) * HEAD_W)
            r = _rotary(acc[:, sl], cos, sa, sb)
            if scale != 1.0:
                r = r * scale
            out_ref[rs, sl] = r.astype(_BF16)

    @pl.when(j == Q_COL)
    def _():
        for rs in row_chunks:
            h = normed(rs)
            h_ref[rs, :] = h
            store_rotated(jnp.dot(h, w_ref[...], preferred_element_type=_F32), rs, Q_SCALE)

    @pl.when(j == K_COL)
    def _():
        for rs in row_chunks:
            store_rotated(jnp.dot(h_ref[rs, :], w_ref[...], preferred_element_type=_F32), rs, 1.0)

    @pl.when(j > K_COL)
    def _():
        out_ref[...] = jnp.dot(h_ref[...], w_ref[...], preferred_element_type=_F32).astype(_BF16)


def _proj(x2, norm_w, w_in_bf, cos_t, sa_t, sb_t, seq):
    m = x2.shape[0]
    seq_blocks = seq // PROJ_TM
    tab_spec = pl.BlockSpec((PROJ_TM, HEAD_W), lambda i, j: (i % seq_blocks, 0))
    return pl.pallas_call(
        _proj_kernel,
        out_shape=jax.ShapeDtypeStruct((m, D_IN), _BF16),
        grid=(m // PROJ_TM, D_IN // PROJ_TN),
        in_specs=[
            pl.BlockSpec((PROJ_TM, D_MODEL), lambda i, j: (i, 0)),
            pl.BlockSpec((1, D_MODEL), lambda i, j: (0, 0)),
            pl.BlockSpec((D_MODEL, PROJ_TN), lambda i, j: (0, j)),
            tab_spec, tab_spec, tab_spec,
        ],
        out_specs=pl.BlockSpec((PROJ_TM, PROJ_TN), lambda i, j: (i, j)),
        scratch_shapes=[pltpu.VMEM((PROJ_TM, D_MODEL), _BF16)],
        compiler_params=pltpu.CompilerParams(
            dimension_semantics=("arbitrary", "arbitrary"),
            vmem_limit_bytes=VMEM_LIMIT_BYTES),
        name="proj",
    )(x2, norm_w, w_in_bf, cos_t, sa_t, sb_t)


def _attn_kernel(q_ref, k_ref, v_ref, z_ref, lam_ref, sw_ref, o_ref, qqt_ref, vt_ref, s_ref, m_ref, l_ref,
                 acc_ref):
    qi = pl.program_id(2)
    tq, kt = ATTN_TQ, ATTN_KT
    nchunk = 2 * ATTN_QCH

    @pl.when(qi == 0)
    def _():
        for j in range(vt_ref.shape[0]):
            vt_ref[j] = v_ref[j * kt:(j + 1) * kt, :].astype(_F32).T.astype(_BF16)

    qt = q_ref[...].astype(_F32).T
    row = lax.broadcasted_iota(jnp.int32, (HEAD_W, tq), 0)
    qqt_ref[:, 0:tq] = jnp.where(row < DA_HEAD_DIM, qt, 0.0).astype(_BF16)
    qqt_ref[:, tq:2 * tq] = jnp.where(row >= DA_HEAD_DIM, qt, 0.0).astype(_BF16)
    m_ref[...] = jnp.full(m_ref.shape, MASK_VALUE, _F32)
    l_ref[...] = jnp.zeros(l_ref.shape, _F32)
    acc_ref[...] = jnp.zeros(acc_ref.shape, _F32)

    def scores(j, slot, col_ranges):
        k_sub = k_ref[pl.ds(pl.multiple_of(j * kt, kt), kt), :]
        for c0, w in col_ranges:
            s_ref[slot, :, c0:c0 + w] = jnp.dot(k_sub, qqt_ref[:, c0:c0 + w], preferred_element_type=_F32)

    def softmax_pv(j, slot, col_ranges):
        vt_sub = vt_ref[j]
        for c0, w, masked in col_ranges:
            cs = slice(c0, c0 + w)
            s = s_ref[slot, :, cs]
            if masked:
                kv = lax.broadcasted_iota(jnp.int32, (kt, w), 0)
                qq = lax.broadcasted_iota(jnp.int32, (kt, w), 1)
                s = jnp.where(kv <= qq, s, MASK_VALUE)
            m_prev = m_ref[:, cs]
            m_new = jnp.maximum(m_prev, jnp.max(s, axis=0, keepdims=True))
            alpha = jnp.exp2(m_prev - m_new)
            p = jnp.exp2(s - m_new)
            l_ref[:, cs] = alpha * l_ref[:, cs] + jnp.sum(p, axis=0, keepdims=True)
            acc_ref[:, cs] = alpha * acc_ref[:, cs] + jnp.dot(vt_sub, p.astype(_BF16),
                                                              preferred_element_type=_F32)
            m_ref[:, cs] = m_new

    all_cols = [(0, 2 * tq)]
    full = [(0, 2 * tq, False)]

    scores(0, 0, all_cols)

    def body(i, carry):
        j = 2 * i
        scores(j + 1, 1, all_cols)
        softmax_pv(j, 0, full)
        scores(j + 2, 0, all_cols)
        softmax_pv(j + 1, 1, full)
        return carry

    nfull = ATTN_QCH * qi
    lax.fori_loop(0, nfull // 2, body, 0)
    for d in range(ATTN_QCH):
        chunks = [c for c in range(nchunk) if c % ATTN_QCH >= d]
        if d + 1 < ATTN_QCH:
            nxt = [(c * kt, kt) for c in range(nchunk) if c % ATTN_QCH >= d + 1]
            scores(nfull + d + 1, (d + 1) % 2, nxt)
        softmax_pv(nfull + d, d % 2, [(c * kt, kt, c % ATTN_QCH == d) for c in chunks])

    lam_p = lam_ref[...]
    s1 = jnp.sum(lam_p[0:1] * lam_p[1:2], axis=1, keepdims=True)
    s2 = jnp.sum(lam_p[2:3] * lam_p[3:4], axis=1, keepdims=True)
    lam = jnp.exp(s1) - jnp.exp(s2) + LAMBDA_INIT

    ot = acc_ref[...] / l_ref[...]
    ot = ot[:, 0:tq] - lam * ot[:, tq:2 * tq]
    ot = ot * lax.rsqrt(jnp.mean(ot * ot, axis=0, keepdims=True) + SUBLN_EPS)
    o = ot.T * sw_ref[...] * (1.0 - LAMBDA_INIT)
    z = z_ref[...].astype(_F32)
    o_ref[...] = (o * (z * _sigmoid(z))).astype(_BF16)


def _attn(proj, lam_params, subln_w, batch, seq):
    m = proj.shape[0]
    nq = seq // ATTN_TQ
    assert ATTN_TQ == ATTN_QCH * ATTN_KT and seq % ATTN_KT == 0 and ATTN_QCH % 2 == 0
    hb = HEADS_PER_COL_BLK
    return pl.pallas_call(
        _attn_kernel,
        out_shape=jax.ShapeDtypeStruct((m, D_ATTN), _BF16),
        grid=(batch, DA_HEADS, nq),
        in_specs=[
            pl.BlockSpec((ATTN_TQ, HEAD_W), lambda b, h, qi: (b * nq + qi, Q_COL * hb + h)),
            pl.BlockSpec((seq, HEAD_W), lambda b, h, qi: (b, K_COL * hb + h)),
            pl.BlockSpec((seq, HEAD_W), lambda b, h, qi: (b, V_COL * hb + h)),
            pl.BlockSpec((ATTN_TQ, HEAD_W), lambda b, h, qi: (b * nq + qi, ZA_COL * hb + h)),
            pl.BlockSpec((4, DA_HEAD_DIM), lambda b, h, qi: (0, 0)),
            pl.BlockSpec((1, HEAD_W), lambda b, h, qi: (0, 0)),
        ],
        out_specs=pl.BlockSpec((ATTN_TQ, HEAD_W), lambda b, h, qi: (b * nq + qi, h)),
        scratch_shapes=[
            pltpu.VMEM((HEAD_W, 2 * ATTN_TQ), _BF16),
            pltpu.VMEM((seq // ATTN_KT, HEAD_W, ATTN_KT), _BF16),
            pltpu.VMEM((2, ATTN_KT, 2 * ATTN_TQ), _F32),
            pltpu.VMEM((1, 2 * ATTN_TQ), _F32),
            pltpu.VMEM((1, 2 * ATTN_TQ), _F32),
            pltpu.VMEM((HEAD_W, 2 * ATTN_TQ), _F32),
        ],
        compiler_params=pltpu.CompilerParams(
            dimension_semantics=("arbitrary", "arbitrary", "arbitrary"),
            vmem_limit_bytes=VMEM_LIMIT_BYTES),
        name="attn",
    )(proj, proj, proj, proj, lam_params, subln_w)


def _tail_kernel(a_ref, g_ref, ah_ref, gh_ref, z_ref, bglu_ref, wdw_ref, bdw_ref, lng_ref, lnb_ref,
                 o_ref, ga0_ref, ga1_ref, gc0_ref, gc1_ref, wo_ref, wp_ref, bp_ref, wout_ref, x_ref, nw_ref,
                 out_ref, stage_ref, shift_ref, y_ref, u_ref, unext_ref, mix_ref, raw_ref, *, tiles_per_seq):
    i = pl.program_id(0)
    n_tiles = pl.num_programs(0) - 2
    t = lax.rem(jnp.minimum(i, n_tiles - 1), tiles_per_seq)

    @pl.when(i == 0)
    def _():
        u_ref[...] = jnp.zeros(u_ref.shape, u_ref.dtype)
        mix_ref[...] = jnp.zeros(mix_ref.shape, mix_ref.dtype)

    ba = bglu_ref[:, 0:D_CONV]
    bg = bglu_ref[:, D_CONV:2 * D_CONV]

    def glu(a, g):
        return (a.astype(_F32) + ba) * _sigmoid(g.astype(_F32) + bg)

    halo = glu(ah_ref[...], gh_ref[...])
    halo = jnp.where(t > 0, halo, jnp.zeros_like(halo))
    cur = glu(a_ref[...], g_ref[...])
    nblk = D_CONV // LANES
    for c in range(nblk):
        ls = slice(c * LANES, (c + 1) * LANES)
        stage_ref[c, 0:CONV_HALO, :] = halo[:, ls]
        stage_ref[c, CONV_HALO:CONV_HALO + CONV_T, :] = cur[:, ls]

    first = CONV_HALO - (CONV_WIDTH - 1)
    shift_rows = shift_ref.shape[1]
    row_starts = range(0, CONV_T, CONV_ROWS)

    def channel_block(c, carry):
        for s in range(1, SUBLANES):
            shift_ref[s - 1] = stage_ref[c, s:s + shift_rows, :]
        taps_by_shift = {}
        for j in range(CONV_WIDTH):
            q, s = divmod(first + j, SUBLANES)
            taps_by_shift.setdefault(s, []).append((j, q))
        accs = [jnp.broadcast_to(bdw_ref[c], (CONV_ROWS, LANES)) for _ in row_starts]
        for s, taps in taps_by_shift.items():
            q_lo = min(q for _, q in taps)
            span = CONV_ROWS + (max(q for _, q in taps) - q_lo) * SUBLANES
            w_rows = [jnp.broadcast_to(wdw_ref[c, j:j + 1, :], (CONV_ROWS, LANES)) for j, _ in taps]
            for k, r in enumerate(row_starts):
                lo = r + q_lo * SUBLANES
                if s == 0:
                    ext = stage_ref[c, lo:lo + span, :]
                else:
                    ext = shift_ref[s - 1, lo:lo + span, :]
                for (j, q), w_j in zip(taps, w_rows):
                    off = (q - q_lo) * SUBLANES
                    accs[k] = accs[k] + ext[off:off + CONV_ROWS, :] * w_j
        for k, r in enumerate(row_starts):
            y_ref[c, r:r + CONV_ROWS, :] = accs[k]
        raw_ref[c] = jnp.dot(mix_ref[...], wout_ref[c], preferred_element_type=_F32)
        return carry

    lax.fori_loop(0, nblk, channel_block, 0)

    cw = D_MODEL // TAIL_CHUNKS
    rw = CONV_T // TAIL_CHUNKS
    gate_refs = [(ga0_ref, gc0_ref), (ga1_ref, gc1_ref)]
    for p in range(TAIL_CHUNKS):
        cs = slice(p * cw, (p + 1) * cw)
        ga_ref, gc_ref = gate_refs[p * cw // COL_BLK]
        gs = slice(p * cw % COL_BLK, p * cw % COL_BLK + cw)
        ya = jnp.dot(o_ref[...], wo_ref[:, cs], preferred_element_type=_F32)
        yc = jnp.dot(u_ref[...], wp_ref[:, cs], preferred_element_type=_F32) + bp_ref[:, cs]

        rs = slice(p * rw, (p + 1) * rw)
        y = jnp.concatenate([y_ref[c, rs, :] for c in range(nblk)], axis=1)
        mu = jnp.mean(y, axis=-1, keepdims=True)
        yd = y - mu
        var = jnp.mean(yd * yd, axis=-1, keepdims=True)
        u = yd * lax.rsqrt(var + LN_EPS) * lng_ref[...] + lnb_ref[...]
        z = z_ref[rs, :].astype(_F32)
        unext_ref[rs, :] = ((u * _sigmoid(u)) * (z * _sigmoid(z))).astype(_BF16)

        out = jnp.concatenate([raw_ref[c, rs, :] for c in range(nblk)], axis=1)
        ms = jnp.mean(out * out, axis=-1, keepdims=True)
        out_ref[rs, :] = x_ref[rs, :] + out * lax.rsqrt(ms + NORM_EPS) * nw_ref[...]

        ga = _sigmoid(ga_ref[:, gs].astype(_F32))
        gc = _sigmoid(gc_ref[:, gs].astype(_F32))
        mix_ref[:, cs] = (ga * ya + gc * yc).astype(_BF16)

    u_ref[...] = unext_ref[...]


def _tail(x2, o, proj, b_glu, w_dw, b_dw, ln_g, ln_b, w_o_bf, w_pw2_bf, b_pw2, w_out_bf, norm_w, seq):
    m = x2.shape[0]
    n_tiles = m // CONV_T
    halo_per_tile = CONV_T // CONV_HALO
    nblk = D_CONV // LANES
    assert D_MODEL // nblk == OUT_BLK
    w_blk = w_dw.reshape(CONV_WIDTH, nblk, LANES).transpose(1, 0, 2)
    b_blk = b_dw.reshape(nblk, 1, LANES)
    w_out_blk = w_out_bf.reshape(D_MODEL, nblk, OUT_BLK).transpose(1, 0, 2)

    conv_tile = lambda i: jnp.minimum(i, n_tiles - 1)
    mix_tile = lambda i: jnp.clip(i - 1, 0, n_tiles - 1)
    out_tile = lambda i: jnp.maximum(i - 2, 0)

    def proj_cols(tile, col):
        return pl.BlockSpec((CONV_T, COL_BLK), lambda i: (tile(i), col))

    def halo(col):
        return pl.BlockSpec((CONV_HALO, COL_BLK),
                            lambda i: (jnp.maximum(conv_tile(i) * halo_per_tile - 1, 0), col))

    def whole(shape, buffered_once=False):
        zeros = (0,) * len(shape)
        if buffered_once:
            return pl.BlockSpec(shape, lambda i: zeros, pipeline_mode=pl.Buffered(1))
        return pl.BlockSpec(shape, lambda i: zeros)

    return pl.pallas_call(
        functools.partial(_tail_kernel, tiles_per_seq=seq // CONV_T),
        out_shape=jax.ShapeDtypeStruct((m, D_MODEL), _F32),
        grid=(n_tiles + 2,),
        in_specs=[
            proj_cols(conv_tile, GLU_A_COL), proj_cols(conv_tile, GLU_G_COL),
            halo(GLU_A_COL), halo(GLU_G_COL),
            proj_cols(conv_tile, ZC_COL),
            whole((1, 2 * D_CONV)),
            whole((nblk, CONV_WIDTH, LANES)), whole((nblk, 1, LANES)),
            whole((1, D_CONV)), whole((1, D_CONV)),
            pl.BlockSpec((CONV_T, D_ATTN), lambda i: (mix_tile(i), 0)),
            proj_cols(mix_tile, GA_COL), proj_cols(mix_tile, GA_COL + 1),
            proj_cols(mix_tile, GC_COL), proj_cols(mix_tile, GC_COL + 1),
            whole((D_ATTN, D_MODEL), True), whole((D_CONV, D_MODEL), True), whole((1, D_MODEL)),
            whole((nblk, D_MODEL, OUT_BLK), True),
            pl.BlockSpec((CONV_T, D_MODEL), lambda i: (out_tile(i), 0)),
            whole((1, D_MODEL)),
        ],
        out_specs=pl.BlockSpec((CONV_T, D_MODEL), lambda i: (out_tile(i), 0)),
        scratch_shapes=[
            pltpu.VMEM((nblk, CONV_HALO + CONV_T, LANES), _F32),
            pltpu.VMEM((SUBLANES - 1, CONV_HALO + CONV_T - SUBLANES, LANES), _F32),
            pltpu.VMEM((nblk, CONV_T, LANES), _F32),
            pltpu.VMEM((CONV_T, D_CONV), _BF16),
            pltpu.VMEM((CONV_T, D_CONV), _BF16),
            pltpu.VMEM((CONV_T, D_MODEL), _BF16),
            pltpu.VMEM((nblk, CONV_T, OUT_BLK), _F32),
        ],
        compiler_params=pltpu.CompilerParams(
            dimension_semantics=("arbitrary",),
            vmem_limit_bytes=VMEM_LIMIT_BYTES),
        name="tail",
    )(proj, proj, proj, proj, proj, b_glu, w_blk, b_blk, ln_g, ln_b,
      o, proj, proj, proj, proj, w_o_bf, w_pw2_bf, b_pw2, w_out_blk,
      x2, norm_w)


def _rotary_tables(seq):
    half = ROT_DIM // 2
    inv_freq = ROPE_THETA ** (-jnp.arange(0, ROT_DIM, 2, dtype=_F32) / ROT_DIM)
    ang = jnp.arange(seq).astype(_F32)[:, None] * inv_freq[None, :]
    cos, sin = jnp.cos(ang), jnp.sin(ang)
    ones = jnp.ones((seq, DA_HEAD_DIM - ROT_DIM), _F32)
    zeros = jnp.zeros((seq, DA_HEAD_DIM - ROT_DIM), _F32)
    zh = jnp.zeros((seq, half), _F32)
    cos_c = jnp.concatenate([cos, cos, ones], axis=1)
    sa_c = jnp.concatenate([-sin, zh, zeros], axis=1)
    sb_c = jnp.concatenate([zh, sin, zeros], axis=1)
    tile2 = lambda a: jnp.concatenate([a, a], axis=1)
    return tile2(cos_c), tile2(sa_c), tile2(sb_c)


def kernel(x, norm_pre_w, w_in, lambda_q1, lambda_k1, lambda_q2, lambda_k2, subln_w, w_o_attn, b_glu,
           w_dw, b_dw, ln_g, ln_b, w_pw2, b_pw2, w_out, norm_post_w):
    batch, seq, _ = x.shape
    assert norm_pre_w.shape[0] == 1, "single layer"
    assert seq % PROJ_TM == 0 and seq % ATTN_TQ == 0 and seq % CONV_T == 0
    x2 = x.reshape(batch * seq, D_MODEL)
    cos_t, sa_t, sb_t = _rotary_tables(seq)
    lam_params = jnp.concatenate([lambda_q1, lambda_k1, lambda_q2, lambda_k2], axis=0).astype(_F32)

    proj = _proj(x2, norm_pre_w, w_in[0].astype(_BF16), cos_t, sa_t, sb_t, seq)
    o = _attn(proj, lam_params, subln_w, batch, seq)
    out = _tail(x2, o, proj, b_glu, w_dw[0], b_dw, ln_g, ln_b, w_o_attn[0].astype(_BF16),
                w_pw2[0].astype(_BF16), b_pw2, w_out[0].astype(_BF16), norm_post_w, seq)
    return out.reshape(batch, seq, D_MODEL)
```

```python
import math

import jax
import jax.numpy as jnp
from jax import lax
from jax.experimental import pallas as pl
from jax.experimental.pallas import tpu as pltpu

D_MODEL = 2048
DA_HEADS = 8
DA_HEAD_DIM = 64
HEAD_W = 2 * DA_HEAD_DIM
D_ATTN = DA_HEADS * HEAD_W
ROPE_THETA = 500000.0
ROT_DIM = DA_HEAD_DIM // 4
D_CONV = 1024
CONV_WIDTH = 31
NORM_EPS = 1e-6
SUBLN_EPS = 1e-5
LN_EPS = 1e-5
LAMBDA_INIT = 0.8 - 0.6 * math.exp(-0.3 * 0)
D_IN = D_ATTN * 4 + 2 * D_CONV + D_CONV + 2 * D_MODEL

COL_BLK = 1024
Q_COL, K_COL, V_COL, ZA_COL, GLU_A_COL, GLU_G_COL, ZC_COL, GA_COL, GC_COL = 0, 1, 2, 3, 4, 5, 6, 7, 9
HEADS_PER_COL_BLK = COL_BLK // HEAD_W

PROJ_TM = 1024
PROJ_TN = COL_BLK
PROJ_ROWS = 256
Q_SCALE = DA_HEAD_DIM ** -0.5 * math.log2(math.e)
ATTN_KT = 256
ATTN_QCH = 2
ATTN_TQ = ATTN_QCH * ATTN_KT
CONV_T = 256
CONV_HALO = 32
CONV_ROWS = 64
MERGE_TM = 256
LANES = 128
SUBLANES = 8
MASK_VALUE = -1e30

VMEM_LIMIT_BYTES = 56 * 1024 * 1024

_BF16 = jnp.bfloat16
_F32 = jnp.float32


def _sigmoid(x):
    return 0.5 * jnp.tanh(0.5 * x) + 0.5


def _rotary(acc, cos, sa, sb):
    half = ROT_DIM // 2
    up = pltpu.roll(acc, LANES - half, 1)
    down = pltpu.roll(acc, half, 1)
    return acc * cos + up * sa + down * sb


def _proj_kernel(x_ref, nw_ref, w_ref, cos_ref, sa_ref, sb_ref, out_ref, h_ref):
    j = pl.program_id(1)
    row_chunks = [slice(r, r + PROJ_ROWS) for r in range(0, PROJ_TM, PROJ_ROWS)]

    def normed(rs):
        x = x_ref[rs, :]
        ms = jnp.mean(x * x, axis=-1, keepdims=True)
        return (x * lax.rsqrt(ms + NORM_EPS) * nw_ref[...]).astype(_BF16)

    def store_rotated(acc, rs, scale):
        cos, sa, sb = cos_ref[rs, :], sa_ref[rs, :], sb_ref[rs, :]
        for hh in range(HEADS_PER_COL_BLK):
            sl = slice(hh * HEAD_W, (hh + 1) * HEAD_W)
            r = _rotary(acc[:, sl], cos, sa, sb)
            if scale != 1.0:
                r = r * scale
            out_ref[rs, sl] = r.astype(_BF16)

    @pl.when(j == Q_COL)
    def _():
        for rs in row_chunks:
            h = normed(rs)
            h_ref[rs, :] = h
            store_rotated(jnp.dot(h, w_ref[...], preferred_element_type=_F32), rs, Q_SCALE)

    @pl.when(j == K_COL)
    def _():
        for rs in row_chunks:
            store_rotated(jnp.dot(h_ref[rs, :], w_ref[...], preferred_element_type=_F32), rs, 1.0)

    @pl.when(j > K_COL)
    def _():
        out_ref[...] = jnp.dot(h_ref[...], w_ref[...], preferred_element_type=_F32).astype(_BF16)


def _proj(x2, norm_w, w_in_bf, cos_t, sa_t, sb_t, seq):
    m = x2.shape[0]
    seq_blocks = seq // PROJ_TM
    tab_spec = pl.BlockSpec((PROJ_TM, HEAD_W), lambda i, j: (i % seq_blocks, 0))
    return pl.pallas_call(
        _proj_kernel,
        out_shape=jax.ShapeDtypeStruct((m, D_IN), _BF16),
        grid=(m // PROJ_TM, D_IN // PROJ_TN),
        in_specs=[
            pl.BlockSpec((PROJ_TM, D_MODEL), lambda i, j: (i, 0)),
            pl.BlockSpec((1, D_MODEL), lambda i, j: (0, 0)),
            pl.BlockSpec((D_MODEL, PROJ_TN), lambda i, j: (0, j)),
            tab_spec, tab_spec, tab_spec,
        ],
        out_specs=pl.BlockSpec((PROJ_TM, PROJ_TN), lambda i, j: (i, j)),
        scratch_shapes=[pltpu.VMEM((PROJ_TM, D_MODEL), _BF16)],
        compiler_params=pltpu.CompilerParams(
            dimension_semantics=("arbitrary", "arbitrary"),
            vmem_limit_bytes=VMEM_LIMIT_BYTES),
        name="proj",
    )(x2, norm_w, w_in_bf, cos_t, sa_t, sb_t)


def _attn_kernel(q_ref, k_ref, v_ref, z_ref, lam_ref, sw_ref, o_ref, qqt_ref, vt_ref, s_ref, m_ref, l_ref,
                 acc_ref):
    tq, kt = ATTN_TQ, ATTN_KT
    nchunk = 2 * ATTN_QCH
    seq = q_ref.shape[0]

    for j in range(vt_ref.shape[0]):
        vt_ref[j] = v_ref[j * kt:(j + 1) * kt, :].astype(_F32).T.astype(_BF16)

    lam_p = lam_ref[...]
    s1 = jnp.sum(lam_p[0:1] * lam_p[1:2], axis=1, keepdims=True)
    s2 = jnp.sum(lam_p[2:3] * lam_p[3:4], axis=1, keepdims=True)
    lam = jnp.exp(s1) - jnp.exp(s2) + LAMBDA_INIT

    def scores(j, slot, col_ranges):
        k_sub = k_ref[pl.ds(pl.multiple_of(j * kt, kt), kt), :]
        for c0, w in col_ranges:
            s_ref[slot, :, c0:c0 + w] = jnp.dot(k_sub, qqt_ref[:, c0:c0 + w], preferred_element_type=_F32)

    def softmax_pv(j, slot, col_ranges):
        vt_sub = vt_ref[j]
        for c0, w, masked in col_ranges:
            cs = slice(c0, c0 + w)
            s = s_ref[slot, :, cs]
            if masked:
                kv = lax.broadcasted_iota(jnp.int32, (kt, w), 0)
                qq = lax.broadcasted_iota(jnp.int32, (kt, w), 1)
                s = jnp.where(kv <= qq, s, MASK_VALUE)
            m_prev = m_ref[:, cs]
            m_new = jnp.maximum(m_prev, jnp.max(s, axis=0, keepdims=True))
            alpha = jnp.exp2(m_prev - m_new)
            p = jnp.exp2(s - m_new)
            l_ref[:, cs] = alpha * l_ref[:, cs] + jnp.sum(p, axis=0, keepdims=True)
            acc_ref[:, cs] = alpha * acc_ref[:, cs] + jnp.dot(vt_sub, p.astype(_BF16),
                                                              preferred_element_type=_F32)
            m_ref[:, cs] = m_new

    all_cols = [(0, 2 * tq)]
    full = [(0, 2 * tq, False)]

    def query_tile(qi, carry):
        rows = pl.ds(pl.multiple_of(qi * tq, tq), tq)

        qt = q_ref[rows, :].astype(_F32).T
        row = lax.broadcasted_iota(jnp.int32, (HEAD_W, tq), 0)
        qqt_ref[:, 0:tq] = jnp.where(row < DA_HEAD_DIM, qt, 0.0).astype(_BF16)
        qqt_ref[:, tq:2 * tq] = jnp.where(row >= DA_HEAD_DIM, qt, 0.0).astype(_BF16)
        m_ref[...] = jnp.full(m_ref.shape, MASK_VALUE, _F32)
        l_ref[...] = jnp.zeros(l_ref.shape, _F32)
        acc_ref[...] = jnp.zeros(acc_ref.shape, _F32)

        scores(0, 0, all_cols)

        def body(i, c):
            j = 2 * i
            scores(j + 1, 1, all_cols)
            softmax_pv(j, 0, full)
            scores(j + 2, 0, all_cols)
            softmax_pv(j + 1, 1, full)
            return c

        nfull = ATTN_QCH * qi
        lax.fori_loop(0, nfull // 2, body, 0)
        for d in range(ATTN_QCH):
            chunks = [c for c in range(nchunk) if c % ATTN_QCH >= d]
            if d + 1 < ATTN_QCH:
                nxt = [(c * kt, kt) for c in range(nchunk) if c % ATTN_QCH >= d + 1]
                scores(nfull + d + 1, (d + 1) % 2, nxt)
            softmax_pv(nfull + d, d % 2, [(c * kt, kt, c % ATTN_QCH == d) for c in chunks])

        ot = acc_ref[...] / l_ref[...]
        ot = ot[:, 0:tq] - lam * ot[:, tq:2 * tq]
        ot = ot * lax.rsqrt(jnp.mean(ot * ot, axis=0, keepdims=True) + SUBLN_EPS)
        o = ot.T * sw_ref[...] * (1.0 - LAMBDA_INIT)
        z = z_ref[rows, :].astype(_F32)
        o_ref[rows, :] = (o * (z * _sigmoid(z))).astype(_BF16)
        return carry

    lax.fori_loop(0, seq // tq, query_tile, 0)


def _attn(proj, lam_params, subln_w, batch, seq):
    m = proj.shape[0]
    assert ATTN_TQ == ATTN_QCH * ATTN_KT and seq % ATTN_TQ == 0 and ATTN_QCH % 2 == 0
    hb = HEADS_PER_COL_BLK

    def head_cols(col):
        return pl.BlockSpec((seq, HEAD_W), lambda b, h: (b, col * hb + h))

    return pl.pallas_call(
        _attn_kernel,
        out_shape=jax.ShapeDtypeStruct((m, D_ATTN), _BF16),
        grid=(batch, DA_HEADS),
        in_specs=[
            head_cols(Q_COL), head_cols(K_COL), head_cols(V_COL), head_cols(ZA_COL),
            pl.BlockSpec((4, DA_HEAD_DIM), lambda b, h: (0, 0)),
            pl.BlockSpec((1, HEAD_W), lambda b, h: (0, 0)),
        ],
        out_specs=pl.BlockSpec((seq, HEAD_W), lambda b, h: (b, h)),
        scratch_shapes=[
            pltpu.VMEM((HEAD_W, 2 * ATTN_TQ), _BF16),
            pltpu.VMEM((seq // ATTN_KT, HEAD_W, ATTN_KT), _BF16),
            pltpu.VMEM((2, ATTN_KT, 2 * ATTN_TQ), _F32),
            pltpu.VMEM((1, 2 * ATTN_TQ), _F32),
            pltpu.VMEM((1, 2 * ATTN_TQ), _F32),
            pltpu.VMEM((HEAD_W, 2 * ATTN_TQ), _F32),
        ],
        compiler_params=pltpu.CompilerParams(
            dimension_semantics=("arbitrary", "arbitrary"),
            vmem_limit_bytes=VMEM_LIMIT_BYTES),
        name="attn",
    )(proj, proj, proj, proj, lam_params, subln_w)


def _conv_kernel(a_ref, g_ref, ah_ref, gh_ref, z_ref, bglu_ref, wdw_ref, bdw_ref, lng_ref, lnb_ref,
                 o_ref, stage_ref, shift_ref, y_ref):
    t = pl.program_id(1)
    ba = bglu_ref[:, 0:D_CONV]
    bg = bglu_ref[:, D_CONV:2 * D_CONV]

    def glu(a, g):
        return (a.astype(_F32) + ba) * _sigmoid(g.astype(_F32) + bg)

    halo = glu(ah_ref[...], gh_ref[...])
    halo = jnp.where(t > 0, halo, jnp.zeros_like(halo))
    cur = glu(a_ref[...], g_ref[...])
    nblk = D_CONV // LANES
    for c in range(nblk):
        ls = slice(c * LANES, (c + 1) * LANES)
        stage_ref[c, 0:CONV_HALO, :] = halo[:, ls]
        stage_ref[c, CONV_HALO:CONV_HALO + CONV_T, :] = cur[:, ls]

    first = CONV_HALO - (CONV_WIDTH - 1)
    shift_rows = shift_ref.shape[1]
    row_starts = range(0, CONV_T, CONV_ROWS)

    def channel_block(c, carry):
        for s in range(1, SUBLANES):
            shift_ref[s - 1] = stage_ref[c, s:s + shift_rows, :]
        accs = [jnp.broadcast_to(bdw_ref[c], (CONV_ROWS, LANES)) for _ in row_starts]
        for j in range(CONV_WIDTH):
            q, s = divmod(first + j, SUBLANES)
            w_j = jnp.broadcast_to(wdw_ref[c, j:j + 1, :], (CONV_ROWS, LANES))
            for k, r in enumerate(row_starts):
                lo = r + q * SUBLANES
                if s == 0:
                    win = stage_ref[c, lo:lo + CONV_ROWS, :]
                else:
                    win = shift_ref[s - 1, lo:lo + CONV_ROWS, :]
                accs[k] = accs[k] + win * w_j
        for k, r in enumerate(row_starts):
            y_ref[c, r:r + CONV_ROWS, :] = accs[k]
        return carry

    lax.fori_loop(0, nblk, channel_block, 0)

    y = jnp.concatenate([y_ref[c] for c in range(nblk)], axis=1)
    mu = jnp.mean(y, axis=-1, keepdims=True)
    yc = y - mu
    var = jnp.mean(yc * yc, axis=-1, keepdims=True)
    u = yc * lax.rsqrt(var + LN_EPS) * lng_ref[...] + lnb_ref[...]
    z = z_ref[...].astype(_F32)
    o_ref[...] = ((u * _sigmoid(u)) * (z * _sigmoid(z))).astype(_BF16)


def _conv(proj, b_glu, w_dw, b_dw, ln_g, ln_b, batch, seq):
    m = proj.shape[0]
    nt = seq // CONV_T
    halo_per_tile = CONV_T // CONV_HALO
    nblk = D_CONV // LANES
    w_blk = w_dw.reshape(CONV_WIDTH, nblk, LANES).transpose(1, 0, 2)
    b_blk = b_dw.reshape(nblk, 1, LANES)

    def halo_idx(col):
        return lambda b, t: (jnp.maximum((b * nt + t) * halo_per_tile - 1, 0), col * (COL_BLK // D_CONV))

    def vec(n):
        return pl.BlockSpec((1, n), lambda b, t: (0, 0))

    return pl.pallas_call(
        _conv_kernel,
        out_shape=jax.ShapeDtypeStruct((m, D_CONV), _BF16),
        grid=(batch, nt),
        in_specs=[
            pl.BlockSpec((CONV_T, D_CONV), lambda b, t: (b * nt + t, GLU_A_COL)),
            pl.BlockSpec((CONV_T, D_CONV), lambda b, t: (b * nt + t, GLU_G_COL)),
            pl.BlockSpec((CONV_HALO, D_CONV), halo_idx(GLU_A_COL)),
            pl.BlockSpec((CONV_HALO, D_CONV), halo_idx(GLU_G_COL)),
            pl.BlockSpec((CONV_T, D_CONV), lambda b, t: (b * nt + t, ZC_COL)),
            vec(2 * D_CONV),
            pl.BlockSpec((nblk, CONV_WIDTH, LANES), lambda b, t: (0, 0, 0)),
            pl.BlockSpec((nblk, 1, LANES), lambda b, t: (0, 0, 0)),
            vec(D_CONV), vec(D_CONV),
        ],
        out_specs=pl.BlockSpec((CONV_T, D_CONV), lambda b, t: (b * nt + t, 0)),
        scratch_shapes=[
            pltpu.VMEM((nblk, CONV_HALO + CONV_T, LANES), _F32),
            pltpu.VMEM((SUBLANES - 1, CONV_HALO + CONV_T - SUBLANES, LANES), _F32),
            pltpu.VMEM((nblk, CONV_T, LANES), _F32),
        ],
        compiler_params=pltpu.CompilerParams(
            dimension_semantics=("arbitrary", "arbitrary"),
            vmem_limit_bytes=VMEM_LIMIT_BYTES),
        name="conv",
    )(proj, proj, proj, proj, proj, b_glu, w_blk, b_blk, ln_g, ln_b)


def _merge_kernel(x_ref, o_ref, u_ref, ga0_ref, ga1_ref, gc0_ref, gc1_ref, wo_ref, wp_ref, bp_ref,
                  wout_ref, nw_ref, out_ref):
    ya = jnp.dot(o_ref[...], wo_ref[...], preferred_element_type=_F32)
    yc = jnp.dot(u_ref[...], wp_ref[...], preferred_element_type=_F32) + bp_ref[...]
    ga = jnp.concatenate([ga0_ref[...], ga1_ref[...]], axis=1).astype(_F32)
    gc = jnp.concatenate([gc0_ref[...], gc1_ref[...]], axis=1).astype(_F32)
    mix = _sigmoid(ga) * ya + _sigmoid(gc) * yc
    out = jnp.dot(mix.astype(_BF16), wout_ref[...], preferred_element_type=_F32)
    ms = jnp.mean(out * out, axis=-1, keepdims=True)
    out_ref[...] = x_ref[...] + out * lax.rsqrt(ms + NORM_EPS) * nw_ref[...]


def _merge(x2, o, u, proj, w_o_bf, w_pw2_bf, b_pw2, w_out_bf, norm_w):
    m = x2.shape[0]

    def const(shape):
        return pl.BlockSpec(shape, lambda i: (0, 0), pipeline_mode=pl.Buffered(1))

    def gate(col):
        return pl.BlockSpec((MERGE_TM, COL_BLK), lambda i: (i, col))

    return pl.pallas_call(
        _merge_kernel,
        out_shape=jax.ShapeDtypeStruct((m, D_MODEL), _F32),
        grid=(m // MERGE_TM,),
        in_specs=[
            pl.BlockSpec((MERGE_TM, D_MODEL), lambda i: (i, 0)),
            pl.BlockSpec((MERGE_TM, D_ATTN), lambda i: (i, 0)),
            pl.BlockSpec((MERGE_TM, D_CONV), lambda i: (i, 0)),
            gate(GA_COL), gate(GA_COL + 1), gate(GC_COL), gate(GC_COL + 1),
            const((D_ATTN, D_MODEL)),
            const((D_CONV, D_MODEL)),
            const((1, D_MODEL)),
            const((D_MODEL, D_MODEL)),
            const((1, D_MODEL)),
        ],
        out_specs=pl.BlockSpec((MERGE_TM, D_MODEL), lambda i: (i, 0)),
        compiler_params=pltpu.CompilerParams(
            dimension_semantics=("arbitrary",),
            vmem_limit_bytes=VMEM_LIMIT_BYTES),
        name="merge",
    )(x2, o, u, proj, proj, proj, proj, w_o_bf, w_pw2_bf, b_pw2, w_out_bf, norm_w)


def _rotary_tables(seq):
    half = ROT_DIM // 2
    inv_freq = ROPE_THETA ** (-jnp.arange(0, ROT_DIM, 2, dtype=_F32) / ROT_DIM)
    ang = jnp.arange(seq).astype(_F32)[:, None] * inv_freq[None, :]
    cos, sin = jnp.cos(ang), jnp.sin(ang)
    ones = jnp.ones((seq, DA_HEAD_DIM - ROT_DIM), _F32)
    zeros = jnp.zeros((seq, DA_HEAD_DIM - ROT_DIM), _F32)
    zh = jnp.zeros((seq, half), _F32)
    cos_c = jnp.concatenate([cos, cos, ones], axis=1)
    sa_c = jnp.concatenate([-sin, zh, zeros], axis=1)
    sb_c = jnp.concatenate([zh, sin, zeros], axis=1)
    tile2 = lambda a: jnp.concatenate([a, a], axis=1)
    return tile2(cos_c), tile2(sa_c), tile2(sb_c)


def kernel(x, norm_pre_w, w_in, lambda_q1, lambda_k1, lambda_q2, lambda_k2, subln_w, w_o_attn, b_glu,
           w_dw, b_dw, ln_g, ln_b, w_pw2, b_pw2, w_out, norm_post_w):
    batch, seq, _ = x.shape
    assert norm_pre_w.shape[0] == 1, "single layer"
    assert seq % PROJ_TM == 0 and seq % ATTN_TQ == 0 and seq % CONV_T == 0
    x2 = x.reshape(batch * seq, D_MODEL)
    cos_t, sa_t, sb_t = _rotary_tables(seq)
    lam_params = jnp.concatenate([lambda_q1, lambda_k1, lambda_q2, lambda_k2], axis=0).astype(_F32)

    proj = _proj(x2, norm_pre_w, w_in[0].astype(_BF16), cos_t, sa_t, sb_t, seq)
    o = _attn(proj, lam_params, subln_w, batch, seq)
    u = _conv(proj, b_glu, w_dw[0], b_dw, ln_g, ln_b, batch, seq)
    out = _merge(x2, o, u, proj, w_o_attn[0].astype(_BF16), w_pw2[0].astype(_BF16), b_pw2,
                 w_out[0].astype(_BF16), norm_post_w)
    return out.reshape(batch, seq, D_MODEL)
```

```python
import math

import jax
import jax.numpy as jnp
from jax import lax
from jax.experimental import pallas as pl
from jax.experimental.pallas import tpu as pltpu

D_MODEL = 2048
DA_HEADS = 8
DA_HEAD_DIM = 64
HEAD_W = 2 * DA_HEAD_DIM
D_ATTN = DA_HEADS * HEAD_W
ROPE_THETA = 500000.0
ROT_DIM = DA_HEAD_DIM // 4
D_CONV = 1024
CONV_WIDTH = 31
NORM_EPS = 1e-6
SUBLN_EPS = 1e-5
LN_EPS = 1e-5
LAMBDA_INIT = 0.8 - 0.6 * math.exp(-0.3 * 0)
D_IN = D_ATTN * 4 + 2 * D_CONV + D_CONV + 2 * D_MODEL

COL_BLK = 1024
Q_COL, K_COL, V_COL, ZA_COL, GLU_A_COL, GLU_G_COL, ZC_COL, GA_COL, GC_COL = 0, 1, 2, 3, 4, 5, 6, 7, 9
HEADS_PER_COL_BLK = COL_BLK // HEAD_W

PROJ_TM = 1024
PROJ_TN = COL_BLK
PROJ_ROWS = 256
Q_SCALE = DA_HEAD_DIM ** -0.5 * math.log2(math.e)
ATTN_KT = 256
ATTN_QCH = 4
ATTN_TQ = ATTN_QCH * ATTN_KT
CONV_T = 256
CONV_HALO = 32
CONV_ROWS = 64
MERGE_TM = 256
LANES = 128
SUBLANES = 8
MASK_VALUE = -1e30

VMEM_LIMIT_BYTES = 56 * 1024 * 1024

_BF16 = jnp.bfloat16
_F32 = jnp.float32


def _sigmoid(x):
    return 0.5 * jnp.tanh(0.5 * x) + 0.5


def _rotary(acc, cos, sa, sb):
    half = ROT_DIM // 2
    up = pltpu.roll(acc, LANES - half, 1)
    down = pltpu.roll(acc, half, 1)
    return acc * cos + up * sa + down * sb


def _proj_kernel(x_ref, nw_ref, w_ref, cos_ref, sa_ref, sb_ref, out_ref, h_ref):
    j = pl.program_id(1)
    row_chunks = [slice(r, r + PROJ_ROWS) for r in range(0, PROJ_TM, PROJ_ROWS)]

    def normed(rs):
        x = x_ref[rs, :]
        ms = jnp.mean(x * x, axis=-1, keepdims=True)
        return (x * lax.rsqrt(ms + NORM_EPS) * nw_ref[...]).astype(_BF16)

    def store_rotated(acc, rs, scale):
        cos, sa, sb = cos_ref[rs, :], sa_ref[rs, :], sb_ref[rs, :]
        for hh in range(HEADS_PER_COL_BLK):
            sl = slice(hh * HEAD_W, (hh + 1) * HEAD_W)
            r = _rotary(acc[:, sl], cos, sa, sb)
            if scale != 1.0:
                r = r * scale
            out_ref[rs, sl] = r.astype(_BF16)

    @pl.when(j == Q_COL)
    def _():
        for rs in row_chunks:
            h = normed(rs)
            h_ref[rs, :] = h
            store_rotated(jnp.dot(h, w_ref[...], preferred_element_type=_F32), rs, Q_SCALE)

    @pl.when(j == K_COL)
    def _():
        for rs in row_chunks:
            store_rotated(jnp.dot(h_ref[rs, :], w_ref[...], preferred_element_type=_F32), rs, 1.0)

    @pl.when(j > K_COL)
    def _():
        out_ref[...] = jnp.dot(h_ref[...], w_ref[...], preferred_element_type=_F32).astype(_BF16)


def _proj(x2, norm_w, w_in_bf, cos_t, sa_t, sb_t, seq):
    m = x2.shape[0]
    seq_blocks = seq // PROJ_TM
    tab_spec = pl.BlockSpec((PROJ_TM, HEAD_W), lambda i, j: (i % seq_blocks, 0))
    return pl.pallas_call(
        _proj_kernel,
        out_shape=jax.ShapeDtypeStruct((m, D_IN), _BF16),
        grid=(m // PROJ_TM, D_IN // PROJ_TN),
        in_specs=[
            pl.BlockSpec((PROJ_TM, D_MODEL), lambda i, j: (i, 0)),
            pl.BlockSpec((1, D_MODEL), lambda i, j: (0, 0)),
            pl.BlockSpec((D_MODEL, PROJ_TN), lambda i, j: (0, j)),
            tab_spec, tab_spec, tab_spec,
        ],
        out_specs=pl.BlockSpec((PROJ_TM, PROJ_TN), lambda i, j: (i, j)),
        scratch_shapes=[pltpu.VMEM((PROJ_TM, D_MODEL), _BF16)],
        compiler_params=pltpu.CompilerParams(
            dimension_semantics=("arbitrary", "arbitrary"),
            vmem_limit_bytes=VMEM_LIMIT_BYTES),
        name="proj",
    )(x2, norm_w, w_in_bf, cos_t, sa_t, sb_t)


def _attn_kernel(q_ref, k_ref, v_ref, z_ref, lam_ref, sw_ref, o_ref, qqt_ref, vt_ref, s_ref, m_ref, l_ref,
                 acc_ref):
    tq, kt = ATTN_TQ, ATTN_KT
    nchunk = 2 * ATTN_QCH
    seq = q_ref.shape[0]

    for j in range(vt_ref.shape[0]):
        vt_ref[j] = v_ref[j * kt:(j + 1) * kt, :].astype(_F32).T.astype(_BF16)

    lam_p = lam_ref[...]
    s1 = jnp.sum(lam_p[0:1] * lam_p[1:2], axis=1, keepdims=True)
    s2 = jnp.sum(lam_p[2:3] * lam_p[3:4], axis=1, keepdims=True)
    lam = jnp.exp(s1) - jnp.exp(s2) + LAMBDA_INIT

    def scores(j, slot, col_ranges):
        k_sub = k_ref[pl.ds(pl.multiple_of(j * kt, kt), kt), :]
        for c0, w in col_ranges:
            s_ref[slot, :, c0:c0 + w] = jnp.dot(k_sub, qqt_ref[:, c0:c0 + w], preferred_element_type=_F32)

    def softmax_pv(j, slot, col_ranges):
        vt_sub = vt_ref[j]
        for c0, w, masked in col_ranges:
            cs = slice(c0, c0 + w)
            s = s_ref[slot, :, cs]
            if masked:
                kv = lax.broadcasted_iota(jnp.int32, (kt, w), 0)
                qq = lax.broadcasted_iota(jnp.int32, (kt, w), 1)
                s = jnp.where(kv <= qq, s, MASK_VALUE)
            m_prev = m_ref[:, cs]
            m_new = jnp.maximum(m_prev, jnp.max(s, axis=0, keepdims=True))
            alpha = jnp.exp2(m_prev - m_new)
            p = jnp.exp2(s - m_new)
            l_ref[:, cs] = alpha * l_ref[:, cs] + jnp.sum(p, axis=0, keepdims=True)
            acc_ref[:, cs] = alpha * acc_ref[:, cs] + jnp.dot(vt_sub, p.astype(_BF16),
                                                              preferred_element_type=_F32)
            m_ref[:, cs] = m_new

    all_cols = [(0, 2 * tq)]
    full = [(0, 2 * tq, False)]

    def query_tile(qi, carry):
        rows = pl.ds(pl.multiple_of(qi * tq, tq), tq)

        qt = q_ref[rows, :].astype(_F32).T
        row = lax.broadcasted_iota(jnp.int32, (HEAD_W, tq), 0)
        qqt_ref[:, 0:tq] = jnp.where(row < DA_HEAD_DIM, qt, 0.0).astype(_BF16)
        qqt_ref[:, tq:2 * tq] = jnp.where(row >= DA_HEAD_DIM, qt, 0.0).astype(_BF16)
        m_ref[...] = jnp.full(m_ref.shape, MASK_VALUE, _F32)
        l_ref[...] = jnp.zeros(l_ref.shape, _F32)
        acc_ref[...] = jnp.zeros(acc_ref.shape, _F32)

        scores(0, 0, all_cols)

        def body(i, c):
            j = 2 * i
            scores(j + 1, 1, all_cols)
            softmax_pv(j, 0, full)
            scores(j + 2, 0, all_cols)
            softmax_pv(j + 1, 1, full)
            return c

        nfull = ATTN_QCH * qi
        lax.fori_loop(0, nfull // 2, body, 0)
        for d in range(ATTN_QCH):
            chunks = [c for c in range(nchunk) if c % ATTN_QCH >= d]
            if d + 1 < ATTN_QCH:
                nxt = [(c * kt, kt) for c in range(nchunk) if c % ATTN_QCH >= d + 1]
                scores(nfull + d + 1, (d + 1) % 2, nxt)
            softmax_pv(nfull + d, d % 2, [(c * kt, kt, c % ATTN_QCH == d) for c in chunks])

        ot = acc_ref[...] / l_ref[...]
        ot = ot[:, 0:tq] - lam * ot[:, tq:2 * tq]
        ot = ot * lax.rsqrt(jnp.mean(ot * ot, axis=0, keepdims=True) + SUBLN_EPS)
        o = ot.T * sw_ref[...] * (1.0 - LAMBDA_INIT)
        z = z_ref[rows, :].astype(_F32)
        o_ref[rows, :] = (o * (z * _sigmoid(z))).astype(_BF16)
        return carry

    lax.fori_loop(0, seq // tq, query_tile, 0)


def _attn(proj, lam_params, subln_w, batch, seq):
    m = proj.shape[0]
    assert ATTN_TQ == ATTN_QCH * ATTN_KT and seq % ATTN_TQ == 0 and ATTN_QCH % 2 == 0
    hb = HEADS_PER_COL_BLK

    def head_cols(col):
        return pl.BlockSpec((seq, HEAD_W), lambda b, h: (b, col * hb + h))

    return pl.pallas_call(
        _attn_kernel,
        out_shape=jax.ShapeDtypeStruct((m, D_ATTN), _BF16),
        grid=(batch, DA_HEADS),
        in_specs=[
            head_cols(Q_COL), head_cols(K_COL), head_cols(V_COL), head_cols(ZA_COL),
            pl.BlockSpec((4, DA_HEAD_DIM), lambda b, h: (0, 0)),
            pl.BlockSpec((1, HEAD_W), lambda b, h: (0, 0)),
        ],
        out_specs=pl.BlockSpec((seq, HEAD_W), lambda b, h: (b, h)),
        scratch_shapes=[
            pltpu.VMEM((HEAD_W, 2 * ATTN_TQ), _BF16),
            pltpu.VMEM((seq // ATTN_KT, HEAD_W, ATTN_KT), _BF16),
            pltpu.VMEM((2, ATTN_KT, 2 * ATTN_TQ), _F32),
            pltpu.VMEM((1, 2 * ATTN_TQ), _F32),
            pltpu.VMEM((1, 2 * ATTN_TQ), _F32),
            pltpu.VMEM((HEAD_W, 2 * ATTN_TQ), _F32),
        ],
        compiler_params=pltpu.CompilerParams(
            dimension_semantics=("arbitrary", "arbitrary"),
            vmem_limit_bytes=VMEM_LIMIT_BYTES),
        name="attn",
    )(proj, proj, proj, proj, lam_params, subln_w)


def _conv_kernel(a_ref, g_ref, ah_ref, gh_ref, z_ref, bglu_ref, wdw_ref, bdw_ref, lng_ref, lnb_ref,
                 o_ref, stage_ref, shift_ref, y_ref):
    t = pl.program_id(1)
    ba = bglu_ref[:, 0:D_CONV]
    bg = bglu_ref[:, D_CONV:2 * D_CONV]

    def glu(a, g):
        return (a.astype(_F32) + ba) * _sigmoid(g.astype(_F32) + bg)

    halo = glu(ah_ref[...], gh_ref[...])
    halo = jnp.where(t > 0, halo, jnp.zeros_like(halo))
    cur = glu(a_ref[...], g_ref[...])
    nblk = D_CONV // LANES
    for c in range(nblk):
        ls = slice(c * LANES, (c + 1) * LANES)
        stage_ref[c, 0:CONV_HALO, :] = halo[:, ls]
        stage_ref[c, CONV_HALO:CONV_HALO + CONV_T, :] = cur[:, ls]

    first = CONV_HALO - (CONV_WIDTH - 1)
    shift_rows = shift_ref.shape[1]
    row_starts = range(0, CONV_T, CONV_ROWS)

    def channel_block(c, carry):
        for s in range(1, SUBLANES):
            shift_ref[s - 1] = stage_ref[c, s:s + shift_rows, :]
        accs = [jnp.broadcast_to(bdw_ref[c], (CONV_ROWS, LANES)) for _ in row_starts]
        for j in range(CONV_WIDTH):
            q, s = divmod(first + j, SUBLANES)
            w_j = jnp.broadcast_to(wdw_ref[c, j:j + 1, :], (CONV_ROWS, LANES))
            for k, r in enumerate(row_starts):
                lo = r + q * SUBLANES
                if s == 0:
                    win = stage_ref[c, lo:lo + CONV_ROWS, :]
                else:
                    win = shift_ref[s - 1, lo:lo + CONV_ROWS, :]
                accs[k] = accs[k] + win * w_j
        for k, r in enumerate(row_starts):
            y_ref[c, r:r + CONV_ROWS, :] = accs[k]
        return carry

    lax.fori_loop(0, nblk, channel_block, 0)

    y = jnp.concatenate([y_ref[c] for c in range(nblk)], axis=1)
    mu = jnp.mean(y, axis=-1, keepdims=True)
    yc = y - mu
    var = jnp.mean(yc * yc, axis=-1, keepdims=True)
    u = yc * lax.rsqrt(var + LN_EPS) * lng_ref[...] + lnb_ref[...]
    z = z_ref[...].astype(_F32)
    o_ref[...] = ((u * _sigmoid(u)) * (z * _sigmoid(z))).astype(_BF16)


def _conv(proj, b_glu, w_dw, b_dw, ln_g, ln_b, batch, seq):
    m = proj.shape[0]
    nt = seq // CONV_T
    halo_per_tile = CONV_T // CONV_HALO
    nblk = D_CONV // LANES
    w_blk = w_dw.reshape(CONV_WIDTH, nblk, LANES).transpose(1, 0, 2)
    b_blk = b_dw.reshape(nblk, 1, LANES)

    def halo_idx(col):
        return lambda b, t: (jnp.maximum((b * nt + t) * halo_per_tile - 1, 0), col * (COL_BLK // D_CONV))

    def vec(n):
        return pl.BlockSpec((1, n), lambda b, t: (0, 0))

    return pl.pallas_call(
        _conv_kernel,
        out_shape=jax.ShapeDtypeStruct((m, D_CONV), _BF16),
        grid=(batch, nt),
        in_specs=[
            pl.BlockSpec((CONV_T, D_CONV), lambda b, t: (b * nt + t, GLU_A_COL)),
            pl.BlockSpec((CONV_T, D_CONV), lambda b, t: (b * nt + t, GLU_G_COL)),
            pl.BlockSpec((CONV_HALO, D_CONV), halo_idx(GLU_A_COL)),
            pl.BlockSpec((CONV_HALO, D_CONV), halo_idx(GLU_G_COL)),
            pl.BlockSpec((CONV_T, D_CONV), lambda b, t: (b * nt + t, ZC_COL)),
            vec(2 * D_CONV),
            pl.BlockSpec((nblk, CONV_WIDTH, LANES), lambda b, t: (0, 0, 0)),
            pl.BlockSpec((nblk, 1, LANES), lambda b, t: (0, 0, 0)),
            vec(D_CONV), vec(D_CONV),
        ],
        out_specs=pl.BlockSpec((CONV_T, D_CONV), lambda b, t: (b * nt + t, 0)),
        scratch_shapes=[
            pltpu.VMEM((nblk, CONV_HALO + CONV_T, LANES), _F32),
            pltpu.VMEM((SUBLANES - 1, CONV_HALO + CONV_T - SUBLANES, LANES), _F32),
            pltpu.VMEM((nblk, CONV_T, LANES), _F32),
        ],
        compiler_params=pltpu.CompilerParams(
            dimension_semantics=("arbitrary", "arbitrary"),
            vmem_limit_bytes=VMEM_LIMIT_BYTES),
        name="conv",
    )(proj, proj, proj, proj, proj, b_glu, w_blk, b_blk, ln_g, ln_b)


def _merge_kernel(x_ref, o_ref, u_ref, ga0_ref, ga1_ref, gc0_ref, gc1_ref, wo_ref, wp_ref, bp_ref,
                  wout_ref, nw_ref, out_ref):
    ya = jnp.dot(o_ref[...], wo_ref[...], preferred_element_type=_F32)
    yc = jnp.dot(u_ref[...], wp_ref[...], preferred_element_type=_F32) + bp_ref[...]
    ga = jnp.concatenate([ga0_ref[...], ga1_ref[...]], axis=1).astype(_F32)
    gc = jnp.concatenate([gc0_ref[...], gc1_ref[...]], axis=1).astype(_F32)
    mix = _sigmoid(ga) * ya + _sigmoid(gc) * yc
    out = jnp.dot(mix.astype(_BF16), wout_ref[...], preferred_element_type=_F32)
    ms = jnp.mean(out * out, axis=-1, keepdims=True)
    out_ref[...] = x_ref[...] + out * lax.rsqrt(ms + NORM_EPS) * nw_ref[...]


def _merge(x2, o, u, proj, w_o_bf, w_pw2_bf, b_pw2, w_out_bf, norm_w):
    m = x2.shape[0]

    def const(shape):
        return pl.BlockSpec(shape, lambda i: (0, 0), pipeline_mode=pl.Buffered(1))

    def gate(col):
        return pl.BlockSpec((MERGE_TM, COL_BLK), lambda i: (i, col))

    return pl.pallas_call(
        _merge_kernel,
        out_shape=jax.ShapeDtypeStruct((m, D_MODEL), _F32),
        grid=(m // MERGE_TM,),
        in_specs=[
            pl.BlockSpec((MERGE_TM, D_MODEL), lambda i: (i, 0)),
            pl.BlockSpec((MERGE_TM, D_ATTN), lambda i: (i, 0)),
            pl.BlockSpec((MERGE_TM, D_CONV), lambda i: (i, 0)),
            gate(GA_COL), gate(GA_COL + 1), gate(GC_COL), gate(GC_COL + 1),
            const((D_ATTN, D_MODEL)),
            const((D_CONV, D_MODEL)),
            const((1, D_MODEL)),
            const((D_MODEL, D_MODEL)),
            const((1, D_MODEL)),
        ],
        out_specs=pl.BlockSpec((MERGE_TM, D_MODEL), lambda i: (i, 0)),
        compiler_params=pltpu.CompilerParams(
            dimension_semantics=("arbitrary",),
            vmem_limit_bytes=VMEM_LIMIT_BYTES),
        name="merge",
    )(x2, o, u, proj, proj, proj, proj, w_o_bf, w_pw2_bf, b_pw2, w_out_bf, norm_w)


def _rotary_tables(seq):
    half = ROT_DIM // 2
    inv_freq = ROPE_THETA ** (-jnp.arange(0, ROT_DIM, 2, dtype=_F32) / ROT_DIM)
    ang = jnp.arange(seq).astype(_F32)[:, None] * inv_freq[None, :]
    cos, sin = jnp.cos(ang), jnp.sin(ang)
    ones = jnp.ones((seq, DA_HEAD_DIM - ROT_DIM), _F32)
    zeros = jnp.zeros((seq, DA_HEAD_DIM - ROT_DIM), _F32)
    zh = jnp.zeros((seq, half), _F32)
    cos_c = jnp.concatenate([cos, cos, ones], axis=1)
    sa_c = jnp.concatenate([-sin, zh, zeros], axis=1)
    sb_c = jnp.concatenate([zh, sin, zeros], axis=1)
    tile2 = lambda a: jnp.concatenate([a, a], axis=1)
    return tile2(cos_c), tile2(sa_c), tile2(sb_c)


def kernel(x, norm_pre_w, w_in, lambda_q1, lambda_k1, lambda_q2, lambda_k2, subln_w, w_o_attn, b_glu,
           w_dw, b_dw, ln_g, ln_b, w_pw2, b_pw2, w_out, norm_post_w):
    batch, seq, _ = x.shape
    assert norm_pre_w.shape[0] == 1, "single layer"
    assert seq % PROJ_TM == 0 and seq % ATTN_TQ == 0 and seq % CONV_T == 0
    x2 = x.reshape(batch * seq, D_MODEL)
    cos_t, sa_t, sb_t = _rotary_tables(seq)
    lam_params = jnp.concatenate([lambda_q1, lambda_k1, lambda_q2, lambda_k2], axis=0).astype(_F32)

    proj = _proj(x2, norm_pre_w, w_in[0].astype(_BF16), cos_t, sa_t, sb_t, seq)
    o = _attn(proj, lam_params, subln_w, batch, seq)
    u = _conv(proj, b_glu, w_dw[0], b_dw, ln_g, ln_b, batch, seq)
    out = _merge(x2, o, u, proj, w_o_attn[0].astype(_BF16), w_pw2[0].astype(_BF16), b_pw2,
                 w_out[0].astype(_BF16), norm_post_w)
    return out.reshape(batch, seq, D_MODEL)
```

```python
import math

import jax
import jax.numpy as jnp
from jax import lax
from jax.experimental import pallas as pl
from jax.experimental.pallas import tpu as pltpu

D_MODEL = 2048
DA_HEADS = 8
DA_HEAD_DIM = 64
HEAD_W = 2 * DA_HEAD_DIM
D_ATTN = DA_HEADS * HEAD_W
ROPE_THETA = 500000.0
ROT_DIM = DA_HEAD_DIM // 4
D_CONV = 1024
CONV_WIDTH = 31
NORM_EPS = 1e-6
SUBLN_EPS = 1e-5
LN_EPS = 1e-5
LAMBDA_INIT = 0.8 - 0.6 * math.exp(-0.3 * 0)
D_IN = D_ATTN * 4 + 2 * D_CONV + D_CONV + 2 * D_MODEL

COL_BLK = 1024
Q_COL, K_COL, V_COL, ZA_COL, GLU_A_COL, GLU_G_COL, ZC_COL, GA_COL, GC_COL = 0, 1, 2, 3, 4, 5, 6, 7, 9
HEADS_PER_COL_BLK = COL_BLK // HEAD_W

PROJ_TM = 1024
PROJ_TN = COL_BLK
PROJ_ROWS = 256
Q_SCALE = DA_HEAD_DIM ** -0.5 * math.log2(math.e)
ATTN_KT = 256
ATTN_QCH = 4
ATTN_TQ = ATTN_QCH * ATTN_KT
CONV_T = 256
CONV_HALO = 32
CONV_ROWS = 64
MERGE_TM = 256
LANES = 128
SUBLANES = 8
MASK_VALUE = -1e30

VMEM_LIMIT_BYTES = 56 * 1024 * 1024

_BF16 = jnp.bfloat16
_F32 = jnp.float32


def _sigmoid(x):
    return 0.5 * jnp.tanh(0.5 * x) + 0.5


def _rotary(acc, cos, sa, sb):
    half = ROT_DIM // 2
    up = pltpu.roll(acc, LANES - half, 1)
    down = pltpu.roll(acc, half, 1)
    return acc * cos + up * sa + down * sb


def _proj_kernel(x_ref, nw_ref, w_ref, cos_ref, sa_ref, sb_ref, out_ref, h_ref):
    j = pl.program_id(1)
    row_chunks = [slice(r, r + PROJ_ROWS) for r in range(0, PROJ_TM, PROJ_ROWS)]

    def normed(rs):
        x = x_ref[rs, :]
        ms = jnp.mean(x * x, axis=-1, keepdims=True)
        return (x * lax.rsqrt(ms + NORM_EPS) * nw_ref[...]).astype(_BF16)

    def store_rotated(acc, rs, scale):
        cos, sa, sb = cos_ref[rs, :], sa_ref[rs, :], sb_ref[rs, :]
        for hh in range(HEADS_PER_COL_BLK):
            sl = slice(hh * HEAD_W, (hh + 1) * HEAD_W)
            r = _rotary(acc[:, sl], cos, sa, sb)
            if scale != 1.0:
                r = r * scale
            out_ref[rs, sl] = r.astype(_BF16)

    @pl.when(j == Q_COL)
    def _():
        for rs in row_chunks:
            h = normed(rs)
            h_ref[rs, :] = h
            store_rotated(jnp.dot(h, w_ref[...], preferred_element_type=_F32), rs, Q_SCALE)

    @pl.when(j == K_COL)
    def _():
        for rs in row_chunks:
            store_rotated(jnp.dot(h_ref[rs, :], w_ref[...], preferred_element_type=_F32), rs, 1.0)

    @pl.when(j > K_COL)
    def _():
        out_ref[...] = jnp.dot(h_ref[...], w_ref[...], preferred_element_type=_F32).astype(_BF16)


def _proj(x2, norm_w, w_in_bf, cos_t, sa_t, sb_t, seq):
    m = x2.shape[0]
    seq_blocks = seq // PROJ_TM
    tab_spec = pl.BlockSpec((PROJ_TM, HEAD_W), lambda i, j: (i % seq_blocks, 0))
    return pl.pallas_call(
        _proj_kernel,
        out_shape=jax.ShapeDtypeStruct((m, D_IN), _BF16),
        grid=(m // PROJ_TM, D_IN // PROJ_TN),
        in_specs=[
            pl.BlockSpec((PROJ_TM, D_MODEL), lambda i, j: (i, 0)),
            pl.BlockSpec((1, D_MODEL), lambda i, j: (0, 0)),
            pl.BlockSpec((D_MODEL, PROJ_TN), lambda i, j: (0, j)),
            tab_spec, tab_spec, tab_spec,
        ],
        out_specs=pl.BlockSpec((PROJ_TM, PROJ_TN), lambda i, j: (i, j)),
        scratch_shapes=[pltpu.VMEM((PROJ_TM, D_MODEL), _BF16)],
        compiler_params=pltpu.CompilerParams(
            dimension_semantics=("arbitrary", "arbitrary"),
            vmem_limit_bytes=VMEM_LIMIT_BYTES),
        name="proj",
    )(x2, norm_w, w_in_bf, cos_t, sa_t, sb_t)


def _attn_kernel(q_ref, k_ref, v_ref, z_ref, lam_ref, sw_ref, wo_ref, wp_ref, wout_ref,
                 o_ref, wo_bf_ref, wp_bf_ref, wout_bf_ref,
                 qqt_ref, vt_ref, s_ref, m_ref, l_ref, acc_ref):
    tq, kt = ATTN_TQ, ATTN_KT
    nchunk = 2 * ATTN_QCH
    seq = q_ref.shape[0]
    wo_bf_ref[...] = wo_ref[...].astype(_BF16)
    wp_bf_ref[...] = wp_ref[...].astype(_BF16)
    wout_bf_ref[...] = wout_ref[...].astype(_BF16)

    for j in range(vt_ref.shape[0]):
        vt_ref[j] = v_ref[j * kt:(j + 1) * kt, :].astype(_F32).T.astype(_BF16)

    lam_p = lam_ref[...]
    s1 = jnp.sum(lam_p[0:1] * lam_p[1:2], axis=1, keepdims=True)
    s2 = jnp.sum(lam_p[2:3] * lam_p[3:4], axis=1, keepdims=True)
    lam = jnp.exp(s1) - jnp.exp(s2) + LAMBDA_INIT

    def scores(j, slot, col_ranges):
        k_sub = k_ref[pl.ds(pl.multiple_of(j * kt, kt), kt), :]
        for c0, w in col_ranges:
            s_ref[slot, :, c0:c0 + w] = jnp.dot(k_sub, qqt_ref[:, c0:c0 + w], preferred_element_type=_F32)

    def softmax_pv(j, slot, col_ranges):
        vt_sub = vt_ref[j]
        for c0, w, masked in col_ranges:
            cs = slice(c0, c0 + w)
            s = s_ref[slot, :, cs]
            if masked:
                kv = lax.broadcasted_iota(jnp.int32, (kt, w), 0)
                qq = lax.broadcasted_iota(jnp.int32, (kt, w), 1)
                s = jnp.where(kv <= qq, s, MASK_VALUE)
            m_prev = m_ref[:, cs]
            m_new = jnp.maximum(m_prev, jnp.max(s, axis=0, keepdims=True))
            alpha = jnp.exp2(m_prev - m_new)
            p = jnp.exp2(s - m_new)
            l_ref[:, cs] = alpha * l_ref[:, cs] + jnp.sum(p, axis=0, keepdims=True)
            acc_ref[:, cs] = alpha * acc_ref[:, cs] + jnp.dot(vt_sub, p.astype(_BF16),
                                                              preferred_element_type=_F32)
            m_ref[:, cs] = m_new

    all_cols = [(0, 2 * tq)]
    full = [(0, 2 * tq, False)]

    def query_tile(qi, carry):
        rows = pl.ds(pl.multiple_of(qi * tq, tq), tq)

        qt = q_ref[rows, :].astype(_F32).T
        row = lax.broadcasted_iota(jnp.int32, (HEAD_W, tq), 0)
        qqt_ref[:, 0:tq] = jnp.where(row < DA_HEAD_DIM, qt, 0.0).astype(_BF16)
        qqt_ref[:, tq:2 * tq] = jnp.where(row >= DA_HEAD_DIM, qt, 0.0).astype(_BF16)
        m_ref[...] = jnp.full(m_ref.shape, MASK_VALUE, _F32)
        l_ref[...] = jnp.zeros(l_ref.shape, _F32)
        acc_ref[...] = jnp.zeros(acc_ref.shape, _F32)

        scores(0, 0, all_cols)

        def body(i, c):
            j = 2 * i
            scores(j + 1, 1, all_cols)
            softmax_pv(j, 0, full)
            scores(j + 2, 0, all_cols)
            softmax_pv(j + 1, 1, full)
            return c

        nfull = ATTN_QCH * qi
        lax.fori_loop(0, nfull // 2, body, 0)
        for d in range(ATTN_QCH):
            chunks = [c for c in range(nchunk) if c % ATTN_QCH >= d]
            if d + 1 < ATTN_QCH:
                nxt = [(c * kt, kt) for c in range(nchunk) if c % ATTN_QCH >= d + 1]
                scores(nfull + d + 1, (d + 1) % 2, nxt)
            softmax_pv(nfull + d, d % 2, [(c * kt, kt, c % ATTN_QCH == d) for c in chunks])

        ot = acc_ref[...] / l_ref[...]
        ot = ot[:, 0:tq] - lam * ot[:, tq:2 * tq]
        ot = ot * lax.rsqrt(jnp.mean(ot * ot, axis=0, keepdims=True) + SUBLN_EPS)
        o = ot.T * sw_ref[...] * (1.0 - LAMBDA_INIT)
        z = z_ref[rows, :].astype(_F32)
        o_ref[rows, :] = (o * (z * _sigmoid(z))).astype(_BF16)
        return carry

    lax.fori_loop(0, seq // tq, query_tile, 0)


def _attn(proj, lam_params, subln_w, w_o, w_pw2, w_out, batch, seq):
    m = proj.shape[0]
    assert ATTN_TQ == ATTN_QCH * ATTN_KT and seq % ATTN_TQ == 0 and ATTN_QCH % 2 == 0
    hb = HEADS_PER_COL_BLK
    steps = batch * DA_HEADS

    def head_cols(col):
        return pl.BlockSpec((seq, HEAD_W), lambda b, h: (b, col * hb + h))

    def row_slab(w):
        rows = w.shape[0] // steps
        assert rows * steps == w.shape[0] and rows % (2 * SUBLANES) == 0
        return pl.BlockSpec((rows, w.shape[1]), lambda b, h: (b * DA_HEADS + h, 0))

    weights = (w_o, w_pw2, w_out)
    return pl.pallas_call(
        _attn_kernel,
        out_shape=(jax.ShapeDtypeStruct((m, D_ATTN), _BF16),
                   *[jax.ShapeDtypeStruct(w.shape, _BF16) for w in weights]),
        grid=(batch, DA_HEADS),
        in_specs=[
            head_cols(Q_COL), head_cols(K_COL), head_cols(V_COL), head_cols(ZA_COL),
            pl.BlockSpec((4, DA_HEAD_DIM), lambda b, h: (0, 0)),
            pl.BlockSpec((1, HEAD_W), lambda b, h: (0, 0)),
            *[row_slab(w) for w in weights],
        ],
        out_specs=(pl.BlockSpec((seq, HEAD_W), lambda b, h: (b, h)), *[row_slab(w) for w in weights]),
        scratch_shapes=[
            pltpu.VMEM((HEAD_W, 2 * ATTN_TQ), _BF16),
            pltpu.VMEM((seq // ATTN_KT, HEAD_W, ATTN_KT), _BF16),
            pltpu.VMEM((2, ATTN_KT, 2 * ATTN_TQ), _F32),
            pltpu.VMEM((1, 2 * ATTN_TQ), _F32),
            pltpu.VMEM((1, 2 * ATTN_TQ), _F32),
            pltpu.VMEM((HEAD_W, 2 * ATTN_TQ), _F32),
        ],
        compiler_params=pltpu.CompilerParams(
            dimension_semantics=("arbitrary", "arbitrary"),
            vmem_limit_bytes=VMEM_LIMIT_BYTES),
        name="attn",
    )(proj, proj, proj, proj, lam_params, subln_w, *weights)


def _conv_kernel(a_ref, g_ref, ah_ref, gh_ref, z_ref, bglu_ref, wdw_ref, bdw_ref, lng_ref, lnb_ref,
                 o_ref, stage_ref, shift_ref, y_ref):
    t = pl.program_id(1)
    ba = bglu_ref[:, 0:D_CONV]
    bg = bglu_ref[:, D_CONV:2 * D_CONV]

    def glu(a, g):
        return (a.astype(_F32) + ba) * _sigmoid(g.astype(_F32) + bg)

    halo = glu(ah_ref[...], gh_ref[...])
    halo = jnp.where(t > 0, halo, jnp.zeros_like(halo))
    cur = glu(a_ref[...], g_ref[...])
    nblk = D_CONV // LANES
    for c in range(nblk):
        ls = slice(c * LANES, (c + 1) * LANES)
        stage_ref[c, 0:CONV_HALO, :] = halo[:, ls]
        stage_ref[c, CONV_HALO:CONV_HALO + CONV_T, :] = cur[:, ls]

    first = CONV_HALO - (CONV_WIDTH - 1)
    shift_rows = shift_ref.shape[1]
    row_starts = range(0, CONV_T, CONV_ROWS)

    def channel_block(c, carry):
        for s in range(1, SUBLANES):
            shift_ref[s - 1] = stage_ref[c, s:s + shift_rows, :]
        accs = [jnp.broadcast_to(bdw_ref[c], (CONV_ROWS, LANES)) for _ in row_starts]
        for j in range(CONV_WIDTH):
            q, s = divmod(first + j, SUBLANES)
            w_j = jnp.broadcast_to(wdw_ref[c, j:j + 1, :], (CONV_ROWS, LANES))
            for k, r in enumerate(row_starts):
                lo = r + q * SUBLANES
                if s == 0:
                    win = stage_ref[c, lo:lo + CONV_ROWS, :]
                else:
                    win = shift_ref[s - 1, lo:lo + CONV_ROWS, :]
                accs[k] = accs[k] + win * w_j
        for k, r in enumerate(row_starts):
            y_ref[c, r:r + CONV_ROWS, :] = accs[k]
        return carry

    lax.fori_loop(0, nblk, channel_block, 0)

    y = jnp.concatenate([y_ref[c] for c in range(nblk)], axis=1)
    mu = jnp.mean(y, axis=-1, keepdims=True)
    yc = y - mu
    var = jnp.mean(yc * yc, axis=-1, keepdims=True)
    u = yc * lax.rsqrt(var + LN_EPS) * lng_ref[...] + lnb_ref[...]
    z = z_ref[...].astype(_F32)
    o_ref[...] = ((u * _sigmoid(u)) * (z * _sigmoid(z))).astype(_BF16)


def _conv(proj, b_glu, w_dw, b_dw, ln_g, ln_b, batch, seq):
    m = proj.shape[0]
    nt = seq // CONV_T
    halo_per_tile = CONV_T // CONV_HALO
    nblk = D_CONV // LANES
    w_blk = w_dw.reshape(CONV_WIDTH, nblk, LANES).transpose(1, 0, 2)
    b_blk = b_dw.reshape(nblk, 1, LANES)

    def halo_idx(col):
        return lambda b, t: (jnp.maximum((b * nt + t) * halo_per_tile - 1, 0), col * (COL_BLK // D_CONV))

    def vec(n):
        return pl.BlockSpec((1, n), lambda b, t: (0, 0))

    return pl.pallas_call(
        _conv_kernel,
        out_shape=jax.ShapeDtypeStruct((m, D_CONV), _BF16),
        grid=(batch, nt),
        in_specs=[
            pl.BlockSpec((CONV_T, D_CONV), lambda b, t: (b * nt + t, GLU_A_COL)),
            pl.BlockSpec((CONV_T, D_CONV), lambda b, t: (b * nt + t, GLU_G_COL)),
            pl.BlockSpec((CONV_HALO, D_CONV), halo_idx(GLU_A_COL)),
            pl.BlockSpec((CONV_HALO, D_CONV), halo_idx(GLU_G_COL)),
            pl.BlockSpec((CONV_T, D_CONV), lambda b, t: (b * nt + t, ZC_COL)),
            vec(2 * D_CONV),
            pl.BlockSpec((nblk, CONV_WIDTH, LANES), lambda b, t: (0, 0, 0)),
            pl.BlockSpec((nblk, 1, LANES), lambda b, t: (0, 0, 0)),
            vec(D_CONV), vec(D_CONV),
        ],
        out_specs=pl.BlockSpec((CONV_T, D_CONV), lambda b, t: (b * nt + t, 0)),
        scratch_shapes=[
            pltpu.VMEM((nblk, CONV_HALO + CONV_T, LANES), _F32),
            pltpu.VMEM((SUBLANES - 1, CONV_HALO + CONV_T - SUBLANES, LANES), _F32),
            pltpu.VMEM((nblk, CONV_T, LANES), _F32),
        ],
        compiler_params=pltpu.CompilerParams(
            dimension_semantics=("arbitrary", "arbitrary"),
            vmem_limit_bytes=VMEM_LIMIT_BYTES),
        name="conv",
    )(proj, proj, proj, proj, proj, b_glu, w_blk, b_blk, ln_g, ln_b)


def _merge_kernel(x_ref, o_ref, u_ref, ga0_ref, ga1_ref, gc0_ref, gc1_ref, wo_ref, wp_ref, bp_ref,
                  wout_ref, nw_ref, out_ref):
    ya = jnp.dot(o_ref[...], wo_ref[...], preferred_element_type=_F32)
    yc = jnp.dot(u_ref[...], wp_ref[...], preferred_element_type=_F32) + bp_ref[...]
    ga = jnp.concatenate([ga0_ref[...], ga1_ref[...]], axis=1).astype(_F32)
    gc = jnp.concatenate([gc0_ref[...], gc1_ref[...]], axis=1).astype(_F32)
    mix = _sigmoid(ga) * ya + _sigmoid(gc) * yc
    out = jnp.dot(mix.astype(_BF16), wout_ref[...], preferred_element_type=_F32)
    ms = jnp.mean(out * out, axis=-1, keepdims=True)
    out_ref[...] = x_ref[...] + out * lax.rsqrt(ms + NORM_EPS) * nw_ref[...]


def _merge(x2, o, u, proj, w_o_bf, w_pw2_bf, b_pw2, w_out_bf, norm_w):
    m = x2.shape[0]

    def const(shape):
        return pl.BlockSpec(shape, lambda i: (0, 0), pipeline_mode=pl.Buffered(1))

    def gate(col):
        return pl.BlockSpec((MERGE_TM, COL_BLK), lambda i: (i, col))

    return pl.pallas_call(
        _merge_kernel,
        out_shape=jax.ShapeDtypeStruct((m, D_MODEL), _F32),
        grid=(m // MERGE_TM,),
        in_specs=[
            pl.BlockSpec((MERGE_TM, D_MODEL), lambda i: (i, 0)),
            pl.BlockSpec((MERGE_TM, D_ATTN), lambda i: (i, 0)),
            pl.BlockSpec((MERGE_TM, D_CONV), lambda i: (i, 0)),
            gate(GA_COL), gate(GA_COL + 1), gate(GC_COL), gate(GC_COL + 1),
            const((D_ATTN, D_MODEL)),
            const((D_CONV, D_MODEL)),
            const((1, D_MODEL)),
            const((D_MODEL, D_MODEL)),
            const((1, D_MODEL)),
        ],
        out_specs=pl.BlockSpec((MERGE_TM, D_MODEL), lambda i: (i, 0)),
        compiler_params=pltpu.CompilerParams(
            dimension_semantics=("arbitrary",),
            vmem_limit_bytes=VMEM_LIMIT_BYTES),
        name="merge",
    )(x2, o, u, proj, proj, proj, proj, w_o_bf, w_pw2_bf, b_pw2, w_out_bf, norm_w)


def _rotary_tables(seq):
    half = ROT_DIM // 2
    inv_freq = ROPE_THETA ** (-jnp.arange(0, ROT_DIM, 2, dtype=_F32) / ROT_DIM)
    ang = jnp.arange(seq).astype(_F32)[:, None] * inv_freq[None, :]
    cos, sin = jnp.cos(ang), jnp.sin(ang)
    ones = jnp.ones((seq, DA_HEAD_DIM - ROT_DIM), _F32)
    zeros = jnp.zeros((seq, DA_HEAD_DIM - ROT_DIM), _F32)
    zh = jnp.zeros((seq, half), _F32)
    cos_c = jnp.concatenate([cos, cos, ones], axis=1)
    sa_c = jnp.concatenate([-sin, zh, zeros], axis=1)
    sb_c = jnp.concatenate([zh, sin, zeros], axis=1)
    tile2 = lambda a: jnp.concatenate([a, a], axis=1)
    return tile2(cos_c), tile2(sa_c), tile2(sb_c)


def kernel(x, norm_pre_w, w_in, lambda_q1, lambda_k1, lambda_q2, lambda_k2, subln_w, w_o_attn, b_glu,
           w_dw, b_dw, ln_g, ln_b, w_pw2, b_pw2, w_out, norm_post_w):
    batch, seq, _ = x.shape
    assert norm_pre_w.shape[0] == 1, "single layer"
    assert seq % PROJ_TM == 0 and seq % ATTN_TQ == 0 and seq % CONV_T == 0
    x2 = x.reshape(batch * seq, D_MODEL)
    cos_t, sa_t, sb_t = _rotary_tables(seq)
    lam_params = jnp.concatenate([lambda_q1, lambda_k1, lambda_q2, lambda_k2], axis=0).astype(_F32)

    proj = _proj(x2, norm_pre_w, w_in[0].astype(_BF16), cos_t, sa_t, sb_t, seq)
    o, w_o_bf, w_pw2_bf, w_out_bf = _attn(proj, lam_params, subln_w, w_o_attn[0], w_pw2[0], w_out[0],
                                          batch, seq)
    u = _conv(proj, b_glu, w_dw[0], b_dw, ln_g, ln_b, batch, seq)
    out = _merge(x2, o, u, proj, w_o_bf, w_pw2_bf, b_pw2, w_out_bf, norm_post_w)
    return out.reshape(batch, seq, D_MODEL)
```

```python
import math

import jax
import jax.numpy as jnp
from jax import lax
from jax.experimental import pallas as pl
from jax.experimental.pallas import tpu as pltpu

D_MODEL = 2048
DA_HEADS = 8
DA_HEAD_DIM = 64
HEAD_W = 2 * DA_HEAD_DIM
D_ATTN = DA_HEADS * HEAD_W
ROPE_THETA = 500000.0
ROT_DIM = DA_HEAD_DIM // 4
D_CONV = 1024
CONV_WIDTH = 31
NORM_EPS = 1e-6
SUBLN_EPS = 1e-5
LN_EPS = 1e-5
LAMBDA_INIT = 0.8 - 0.6 * math.exp(-0.3 * 0)
D_IN = D_ATTN * 4 + 2 * D_CONV + D_CONV + 2 * D_MODEL

COL_BLK = 1024
W_CONV_BLKS = (4, 5, 6)
W_REST_BLKS = (0, 1, 2, 3, 7, 8, 9, 10)
GLU_A_COL, GLU_G_COL, ZC_COL = 0, 1, 2
Q_COL, K_COL, V_COL, ZA_COL, GA_COL, GC_COL = 0, 1, 2, 3, 4, 6
HEADS_PER_COL_BLK = COL_BLK // HEAD_W

PROJ_TM = 1024
PROJ_TN = COL_BLK
PROJ_ROWS = 256
Q_SCALE = DA_HEAD_DIM ** -0.5 * math.log2(math.e)
ATTN_KT = 256
ATTN_QCH = 4
ATTN_TQ = ATTN_QCH * ATTN_KT
CONV_T = 256
CONV_HALO = 32
CONV_ROWS = 64
MERGE_TM = 256
LANES = 128
SUBLANES = 8
MASK_VALUE = -1e30

VMEM_LIMIT_BYTES = 56 * 1024 * 1024

_BF16 = jnp.bfloat16
_F32 = jnp.float32


def _sigmoid(x):
    return 0.5 * jnp.tanh(0.5 * x) + 0.5


def _rotary(acc, cos, sa, sb):
    half = ROT_DIM // 2
    up = pltpu.roll(acc, LANES - half, 1)
    down = pltpu.roll(acc, half, 1)
    return acc * cos + up * sa + down * sb


_PROJ_ROW_CHUNKS = [slice(r, r + PROJ_ROWS) for r in range(0, PROJ_TM, PROJ_ROWS)]


def _proj_conv_kernel(x_ref, nw_ref, w_ref, out_ref, h_ref):
    j = pl.program_id(1)

    def normed(rs):
        x = x_ref[rs, :]
        ms = jnp.mean(x * x, axis=-1, keepdims=True)
        return (x * lax.rsqrt(ms + NORM_EPS) * nw_ref[...]).astype(_BF16)

    @pl.when(j == 0)
    def _():
        for rs in _PROJ_ROW_CHUNKS:
            h = normed(rs)
            h_ref[rs, :] = h
            out_ref[rs, :] = jnp.dot(h, w_ref[...], preferred_element_type=_F32).astype(_BF16)

    @pl.when(j > 0)
    def _():
        out_ref[...] = jnp.dot(h_ref[...], w_ref[...], preferred_element_type=_F32).astype(_BF16)


def _proj_rest_kernel(h_ref, w_ref, cos_ref, sa_ref, sb_ref, out_ref):
    j = pl.program_id(1)
    row_chunks = _PROJ_ROW_CHUNKS

    def store_rotated(acc, rs, scale):
        cos, sa, sb = cos_ref[rs, :], sa_ref[rs, :], sb_ref[rs, :]
        for hh in range(HEADS_PER_COL_BLK):
            sl = slice(hh * HEAD_W, (hh + 1) * HEAD_W)
            r = _rotary(acc[:, sl], cos, sa, sb)
            if scale != 1.0:
                r = r * scale
            out_ref[rs, sl] = r.astype(_BF16)

    @pl.when(j == Q_COL)
    def _():
        for rs in row_chunks:
            store_rotated(jnp.dot(h_ref[rs, :], w_ref[...], preferred_element_type=_F32), rs, Q_SCALE)

    @pl.when(j == K_COL)
    def _():
        for rs in row_chunks:
            store_rotated(jnp.dot(h_ref[rs, :], w_ref[...], preferred_element_type=_F32), rs, 1.0)

    @pl.when(j > K_COL)
    def _():
        out_ref[...] = jnp.dot(h_ref[...], w_ref[...], preferred_element_type=_F32).astype(_BF16)


_PROJ_PARAMS = pltpu.CompilerParams(dimension_semantics=("arbitrary", "arbitrary"),
                                    vmem_limit_bytes=VMEM_LIMIT_BYTES)


def _proj_conv(x2, norm_w, w_conv_bf):
    m = x2.shape[0]
    n_col = w_conv_bf.shape[1] // PROJ_TN
    return pl.pallas_call(
        _proj_conv_kernel,
        out_shape=(jax.ShapeDtypeStruct((m, n_col * PROJ_TN), _BF16),
                   jax.ShapeDtypeStruct((m, D_MODEL), _BF16)),
        grid=(m // PROJ_TM, n_col),
        in_specs=[
            pl.BlockSpec((PROJ_TM, D_MODEL), lambda i, j: (i, 0)),
            pl.BlockSpec((1, D_MODEL), lambda i, j: (0, 0)),
            pl.BlockSpec((D_MODEL, PROJ_TN), lambda i, j: (0, j)),
        ],
        out_specs=(pl.BlockSpec((PROJ_TM, PROJ_TN), lambda i, j: (i, j)),
                   pl.BlockSpec((PROJ_TM, D_MODEL), lambda i, j: (i, 0))),
        compiler_params=_PROJ_PARAMS,
        name="proj_conv",
    )(x2, norm_w, w_conv_bf)


def _proj_rest(h, w_rest_bf, cos_t, sa_t, sb_t, seq):
    m = h.shape[0]
    n_col = w_rest_bf.shape[1] // PROJ_TN
    seq_blocks = seq // PROJ_TM
    tab_spec = pl.BlockSpec((PROJ_TM, HEAD_W), lambda i, j: (i % seq_blocks, 0))
    return pl.pallas_call(
        _proj_rest_kernel,
        out_shape=jax.ShapeDtypeStruct((m, n_col * PROJ_TN), _BF16),
        grid=(m // PROJ_TM, n_col),
        in_specs=[
            pl.BlockSpec((PROJ_TM, D_MODEL), lambda i, j: (i, 0)),
            pl.BlockSpec((D_MODEL, PROJ_TN), lambda i, j: (0, j)),
            tab_spec, tab_spec, tab_spec,
        ],
        out_specs=pl.BlockSpec((PROJ_TM, PROJ_TN), lambda i, j: (i, j)),
        compiler_params=_PROJ_PARAMS,
        name="proj_rest",
    )(h, w_rest_bf, cos_t, sa_t, sb_t)


def _attn_kernel(q_ref, k_ref, v_ref, z_ref, lam_ref, sw_ref, o_ref,
                 qqt_ref, vt_ref, s_ref, m_ref, l_ref, acc_ref):
    tq, kt = ATTN_TQ, ATTN_KT
    nchunk = 2 * ATTN_QCH
    seq = q_ref.shape[0]

    for j in range(vt_ref.shape[0]):
        vt_ref[j] = v_ref[j * kt:(j + 1) * kt, :].astype(_F32).T.astype(_BF16)

    lam_p = lam_ref[...]
    s1 = jnp.sum(lam_p[0:1] * lam_p[1:2], axis=1, keepdims=True)
    s2 = jnp.sum(lam_p[2:3] * lam_p[3:4], axis=1, keepdims=True)
    lam = jnp.exp(s1) - jnp.exp(s2) + LAMBDA_INIT

    def scores(j, slot, col_ranges):
        k_sub = k_ref[pl.ds(pl.multiple_of(j * kt, kt), kt), :]
        for c0, w in col_ranges:
            s_ref[slot, :, c0:c0 + w] = jnp.dot(k_sub, qqt_ref[:, c0:c0 + w], preferred_element_type=_F32)

    def softmax_pv(j, slot, col_ranges):
        vt_sub = vt_ref[j]
        for c0, w, masked in col_ranges:
            cs = slice(c0, c0 + w)
            s = s_ref[slot, :, cs]
            if masked:
                kv = lax.broadcasted_iota(jnp.int32, (kt, w), 0)
                qq = lax.broadcasted_iota(jnp.int32, (kt, w), 1)
                s = jnp.where(kv <= qq, s, MASK_VALUE)
            m_prev = m_ref[:, cs]
            m_new = jnp.maximum(m_prev, jnp.max(s, axis=0, keepdims=True))
            alpha = jnp.exp2(m_prev - m_new)
            p = jnp.exp2(s - m_new)
            l_ref[:, cs] = alpha * l_ref[:, cs] + jnp.sum(p, axis=0, keepdims=True)
            acc_ref[:, cs] = alpha * acc_ref[:, cs] + jnp.dot(vt_sub, p.astype(_BF16),
                                                              preferred_element_type=_F32)
            m_ref[:, cs] = m_new

    all_cols = [(0, 2 * tq)]
    full = [(0, 2 * tq, False)]

    def query_tile(qi, carry):
        rows = pl.ds(pl.multiple_of(qi * tq, tq), tq)

        qt = q_ref[rows, :].astype(_F32).T
        row = lax.broadcasted_iota(jnp.int32, (HEAD_W, tq), 0)
        qqt_ref[:, 0:tq] = jnp.where(row < DA_HEAD_DIM, qt, 0.0).astype(_BF16)
        qqt_ref[:, tq:2 * tq] = jnp.where(row >= DA_HEAD_DIM, qt, 0.0).astype(_BF16)
        m_ref[...] = jnp.full(m_ref.shape, MASK_VALUE, _F32)
        l_ref[...] = jnp.zeros(l_ref.shape, _F32)
        acc_ref[...] = jnp.zeros(acc_ref.shape, _F32)

        scores(0, 0, all_cols)

        def body(i, c):
            j = 2 * i
            scores(j + 1, 1, all_cols)
            softmax_pv(j, 0, full)
            scores(j + 2, 0, all_cols)
            softmax_pv(j + 1, 1, full)
            return c

        nfull = ATTN_QCH * qi
        lax.fori_loop(0, nfull // 2, body, 0)
        for d in range(ATTN_QCH):
            chunks = [c for c in range(nchunk) if c % ATTN_QCH >= d]
            if d + 1 < ATTN_QCH:
                nxt = [(c * kt, kt) for c in range(nchunk) if c % ATTN_QCH >= d + 1]
                scores(nfull + d + 1, (d + 1) % 2, nxt)
            softmax_pv(nfull + d, d % 2, [(c * kt, kt, c % ATTN_QCH == d) for c in chunks])

        ot = acc_ref[...] / l_ref[...]
        ot = ot[:, 0:tq] - lam * ot[:, tq:2 * tq]
        ot = ot * lax.rsqrt(jnp.mean(ot * ot, axis=0, keepdims=True) + SUBLN_EPS)
        o = ot.T * sw_ref[...] * (1.0 - LAMBDA_INIT)
        z = z_ref[rows, :].astype(_F32)
        o_ref[rows, :] = (o * (z * _sigmoid(z))).astype(_BF16)
        return carry

    lax.fori_loop(0, seq // tq, query_tile, 0)


def _attn(pr, lam_params, subln_w, batch, seq):
    m = pr.shape[0]
    assert ATTN_TQ == ATTN_QCH * ATTN_KT and seq % ATTN_TQ == 0 and ATTN_QCH % 2 == 0
    hb = HEADS_PER_COL_BLK

    def head_cols(col):
        return pl.BlockSpec((seq, HEAD_W), lambda b, h: (b, col * hb + h))

    return pl.pallas_call(
        _attn_kernel,
        out_shape=jax.ShapeDtypeStruct((m, D_ATTN), _BF16),
        grid=(batch, DA_HEADS),
        in_specs=[
            head_cols(Q_COL), head_cols(K_COL), head_cols(V_COL), head_cols(ZA_COL),
            pl.BlockSpec((4, DA_HEAD_DIM), lambda b, h: (0, 0)),
            pl.BlockSpec((1, HEAD_W), lambda b, h: (0, 0)),
        ],
        out_specs=pl.BlockSpec((seq, HEAD_W), lambda b, h: (b, h)),
        scratch_shapes=[
            pltpu.VMEM((HEAD_W, 2 * ATTN_TQ), _BF16),
            pltpu.VMEM((seq // ATTN_KT, HEAD_W, ATTN_KT), _BF16),
            pltpu.VMEM((2, ATTN_KT, 2 * ATTN_TQ), _F32),
            pltpu.VMEM((1, 2 * ATTN_TQ), _F32),
            pltpu.VMEM((1, 2 * ATTN_TQ), _F32),
            pltpu.VMEM((HEAD_W, 2 * ATTN_TQ), _F32),
        ],
        compiler_params=pltpu.CompilerParams(
            dimension_semantics=("arbitrary", "arbitrary"),
            vmem_limit_bytes=VMEM_LIMIT_BYTES),
        name="attn",
    )(pr, pr, pr, pr, lam_params, subln_w)


def _conv_kernel(a_ref, g_ref, ah_ref, gh_ref, z_ref, bglu_ref, wdw_ref, bdw_ref, lng_ref, lnb_ref, *refs):
    n_rest = len(W_REST_BLKS)
    w_in_refs, (wo_ref, wp_ref, wout_ref) = refs[:n_rest], refs[n_rest:n_rest + 3]
    o_ref, wrest_bf_ref, wo_bf_ref, wp_bf_ref, wout_bf_ref, stage_ref, shift_ref, y_ref = refs[n_rest + 3:]
    for k, w_ref in enumerate(w_in_refs):
        wrest_bf_ref[:, k * COL_BLK:(k + 1) * COL_BLK] = w_ref[...].astype(_BF16)
    wo_bf_ref[...] = wo_ref[...].astype(_BF16)
    wp_bf_ref[...] = wp_ref[...].astype(_BF16)
    wout_bf_ref[...] = wout_ref[...].astype(_BF16)

    t = pl.program_id(1)
    ba = bglu_ref[:, 0:D_CONV]
    bg = bglu_ref[:, D_CONV:2 * D_CONV]

    def glu(a, g):
        return (a.astype(_F32) + ba) * _sigmoid(g.astype(_F32) + bg)

    halo = glu(ah_ref[...], gh_ref[...])
    halo = jnp.where(t > 0, halo, jnp.zeros_like(halo))
    cur = glu(a_ref[...], g_ref[...])
    nblk = D_CONV // LANES
    for c in range(nblk):
        ls = slice(c * LANES, (c + 1) * LANES)
        stage_ref[c, 0:CONV_HALO, :] = halo[:, ls]
        stage_ref[c, CONV_HALO:CONV_HALO + CONV_T, :] = cur[:, ls]

    first = CONV_HALO - (CONV_WIDTH - 1)
    shift_rows = shift_ref.shape[1]
    row_starts = range(0, CONV_T, CONV_ROWS)

    def channel_block(c, carry):
        for s in range(1, SUBLANES):
            shift_ref[s - 1] = stage_ref[c, s:s + shift_rows, :]
        accs = [jnp.broadcast_to(bdw_ref[c], (CONV_ROWS, LANES)) for _ in row_starts]
        for j in range(CONV_WIDTH):
            q, s = divmod(first + j, SUBLANES)
            w_j = jnp.broadcast_to(wdw_ref[c, j:j + 1, :], (CONV_ROWS, LANES))
            for k, r in enumerate(row_starts):
                lo = r + q * SUBLANES
                if s == 0:
                    win = stage_ref[c, lo:lo + CONV_ROWS, :]
                else:
                    win = shift_ref[s - 1, lo:lo + CONV_ROWS, :]
                accs[k] = accs[k] + win * w_j
        for k, r in enumerate(row_starts):
            y_ref[c, r:r + CONV_ROWS, :] = accs[k]
        return carry

    lax.fori_loop(0, nblk, channel_block, 0)

    y = jnp.concatenate([y_ref[c] for c in range(nblk)], axis=1)
    mu = jnp.mean(y, axis=-1, keepdims=True)
    yc = y - mu
    var = jnp.mean(yc * yc, axis=-1, keepdims=True)
    u = yc * lax.rsqrt(var + LN_EPS) * lng_ref[...] + lnb_ref[...]
    z = z_ref[...].astype(_F32)
    o_ref[...] = ((u * _sigmoid(u)) * (z * _sigmoid(z))).astype(_BF16)


def _conv(pc, b_glu, w_dw, b_dw, ln_g, ln_b, w_in, w_o, w_pw2, w_out, batch, seq):
    m = pc.shape[0]
    nt = seq // CONV_T
    steps = batch * nt
    halo_per_tile = CONV_T // CONV_HALO
    nblk = D_CONV // LANES
    w_blk = w_dw.reshape(CONV_WIDTH, nblk, LANES).transpose(1, 0, 2)
    b_blk = b_dw.reshape(nblk, 1, LANES)

    def halo_idx(col):
        return lambda b, t: (jnp.maximum((b * nt + t) * halo_per_tile - 1, 0), col * (COL_BLK // D_CONV))

    def vec(n):
        return pl.BlockSpec((1, n), lambda b, t: (0, 0))

    def row_slab(n_rows, n_cols, col_blk=0):
        rows = n_rows // steps
        assert rows * steps == n_rows and rows % (2 * SUBLANES) == 0
        return pl.BlockSpec((rows, n_cols), lambda b, t: (b * nt + t, col_blk))

    n_rest = len(W_REST_BLKS)
    small = (w_o, w_pw2, w_out)
    return pl.pallas_call(
        _conv_kernel,
        out_shape=(jax.ShapeDtypeStruct((m, D_CONV), _BF16),
                   jax.ShapeDtypeStruct((D_MODEL, n_rest * COL_BLK), _BF16),
                   *[jax.ShapeDtypeStruct(w.shape, _BF16) for w in small]),
        grid=(batch, nt),
        in_specs=[
            pl.BlockSpec((CONV_T, D_CONV), lambda b, t: (b * nt + t, GLU_A_COL)),
            pl.BlockSpec((CONV_T, D_CONV), lambda b, t: (b * nt + t, GLU_G_COL)),
            pl.BlockSpec((CONV_HALO, D_CONV), halo_idx(GLU_A_COL)),
            pl.BlockSpec((CONV_HALO, D_CONV), halo_idx(GLU_G_COL)),
            pl.BlockSpec((CONV_T, D_CONV), lambda b, t: (b * nt + t, ZC_COL)),
            vec(2 * D_CONV),
            pl.BlockSpec((nblk, CONV_WIDTH, LANES), lambda b, t: (0, 0, 0)),
            pl.BlockSpec((nblk, 1, LANES), lambda b, t: (0, 0, 0)),
            vec(D_CONV), vec(D_CONV),
            *[row_slab(D_MODEL, COL_BLK, blk) for blk in W_REST_BLKS],
            *[row_slab(*w.shape) for w in small],
        ],
        out_specs=(pl.BlockSpec((CONV_T, D_CONV), lambda b, t: (b * nt + t, 0)),
                   row_slab(D_MODEL, n_rest * COL_BLK),
                   *[row_slab(*w.shape) for w in small]),
        scratch_shapes=[
            pltpu.VMEM((nblk, CONV_HALO + CONV_T, LANES), _F32),
            pltpu.VMEM((SUBLANES - 1, CONV_HALO + CONV_T - SUBLANES, LANES), _F32),
            pltpu.VMEM((nblk, CONV_T, LANES), _F32),
        ],
        compiler_params=pltpu.CompilerParams(
            dimension_semantics=("arbitrary", "arbitrary"),
            vmem_limit_bytes=VMEM_LIMIT_BYTES),
        name="conv",
    )(pc, pc, pc, pc, pc, b_glu, w_blk, b_blk, ln_g, ln_b, *([w_in] * n_rest), *small)


def _merge_kernel(x_ref, o_ref, u_ref, ga0_ref, ga1_ref, gc0_ref, gc1_ref, wo_ref, wp_ref, bp_ref,
                  wout_ref, nw_ref, out_ref):
    ya = jnp.dot(o_ref[...], wo_ref[...], preferred_element_type=_F32)
    yc = jnp.dot(u_ref[...], wp_ref[...], preferred_element_type=_F32) + bp_ref[...]
    ga = jnp.concatenate([ga0_ref[...], ga1_ref[...]], axis=1).astype(_F32)
    gc = jnp.concatenate([gc0_ref[...], gc1_ref[...]], axis=1).astype(_F32)
    mix = _sigmoid(ga) * ya + _sigmoid(gc) * yc
    out = jnp.dot(mix.astype(_BF16), wout_ref[...], preferred_element_type=_F32)
    ms = jnp.mean(out * out, axis=-1, keepdims=True)
    out_ref[...] = x_ref[...] + out * lax.rsqrt(ms + NORM_EPS) * nw_ref[...]


def _merge(x2, o, u, proj, w_o_bf, w_pw2_bf, b_pw2, w_out_bf, norm_w):
    m = x2.shape[0]

    def const(shape):
        return pl.BlockSpec(shape, lambda i: (0, 0), pipeline_mode=pl.Buffered(1))

    def gate(col):
        return pl.BlockSpec((MERGE_TM, COL_BLK), lambda i: (i, col))

    return pl.pallas_call(
        _merge_kernel,
        out_shape=jax.ShapeDtypeStruct((m, D_MODEL), _F32),
        grid=(m // MERGE_TM,),
        in_specs=[
            pl.BlockSpec((MERGE_TM, D_MODEL), lambda i: (i, 0)),
            pl.BlockSpec((MERGE_TM, D_ATTN), lambda i: (i, 0)),
            pl.BlockSpec((MERGE_TM, D_CONV), lambda i: (i, 0)),
            gate(GA_COL), gate(GA_COL + 1), gate(GC_COL), gate(GC_COL + 1),
            const((D_ATTN, D_MODEL)),
            const((D_CONV, D_MODEL)),
            const((1, D_MODEL)),
            const((D_MODEL, D_MODEL)),
            const((1, D_MODEL)),
        ],
        out_specs=pl.BlockSpec((MERGE_TM, D_MODEL), lambda i: (i, 0)),
        compiler_params=pltpu.CompilerParams(
            dimension_semantics=("arbitrary",),
            vmem_limit_bytes=VMEM_LIMIT_BYTES),
        name="merge",
    )(x2, o, u, proj, proj, proj, proj, w_o_bf, w_pw2_bf, b_pw2, w_out_bf, norm_w)


def _rotary_tables(seq):
    half = ROT_DIM // 2
    inv_freq = ROPE_THETA ** (-jnp.arange(0, ROT_DIM, 2, dtype=_F32) / ROT_DIM)
    ang = jnp.arange(seq).astype(_F32)[:, None] * inv_freq[None, :]
    cos, sin = jnp.cos(ang), jnp.sin(ang)
    ones = jnp.ones((seq, DA_HEAD_DIM - ROT_DIM), _F32)
    zeros = jnp.zeros((seq, DA_HEAD_DIM - ROT_DIM), _F32)
    zh = jnp.zeros((seq, half), _F32)
    cos_c = jnp.concatenate([cos, cos, ones], axis=1)
    sa_c = jnp.concatenate([-sin, zh, zeros], axis=1)
    sb_c = jnp.concatenate([zh, sin, zeros], axis=1)
    tile2 = lambda a: jnp.concatenate([a, a], axis=1)
    return tile2(cos_c), tile2(sa_c), tile2(sb_c)


def kernel(x, norm_pre_w, w_in, lambda_q1, lambda_k1, lambda_q2, lambda_k2, subln_w, w_o_attn, b_glu,
           w_dw, b_dw, ln_g, ln_b, w_pw2, b_pw2, w_out, norm_post_w):
    batch, seq, _ = x.shape
    assert norm_pre_w.shape[0] == 1, "single layer"
    assert seq % PROJ_TM == 0 and seq % ATTN_TQ == 0 and seq % CONV_T == 0
    x2 = x.reshape(batch * seq, D_MODEL)
    cos_t, sa_t, sb_t = _rotary_tables(seq)
    lam_params = jnp.concatenate([lambda_q1, lambda_k1, lambda_q2, lambda_k2], axis=0).astype(_F32)

    conv_cols = slice(W_CONV_BLKS[0] * COL_BLK, (W_CONV_BLKS[-1] + 1) * COL_BLK)
    pc, h = _proj_conv(x2, norm_pre_w, w_in[0][:, conv_cols].astype(_BF16))
    u, w_rest_bf, w_o_bf, w_pw2_bf, w_out_bf = _conv(pc, b_glu, w_dw[0], b_dw, ln_g, ln_b, w_in[0],
                                                     w_o_attn[0], w_pw2[0], w_out[0], batch, seq)
    pr = _proj_rest(h, w_rest_bf, cos_t, sa_t, sb_t, seq)
    o = _attn(pr, lam_params, subln_w, batch, seq)
    out = _merge(x2, o, u, pr, w_o_bf, w_pw2_bf, b_pw2, w_out_bf, norm_post_w)
    return out.reshape(batch, seq, D_MODEL)
```

```python
import math

import jax
import jax.numpy as jnp
import numpy as np
from jax import lax
from jax.experimental import pallas as pl
from jax.experimental.pallas import tpu as pltpu

D_MODEL = 2048
DA_HEADS = 8
DA_HEAD_DIM = 64
HEAD_W = 2 * DA_HEAD_DIM
D_ATTN = DA_HEADS * HEAD_W
ROPE_THETA = 500000.0
ROT_DIM = DA_HEAD_DIM // 4
D_CONV = 1024
CONV_WIDTH = 31
NORM_EPS = 1e-6
SUBLN_EPS = 1e-5
LN_EPS = 1e-5
LAMBDA_INIT = 0.8 - 0.6 * math.exp(-0.3 * 0)
D_IN = D_ATTN * 4 + 2 * D_CONV + D_CONV + 2 * D_MODEL

COL_BLK = 1024
W_CONV_BLKS = (4, 5, 6)
W_REST_BLKS = (0, 1, 2, 3, 7, 8, 9, 10)
GLU_A_COL, GLU_G_COL, ZC_COL = 0, 1, 2
Q_COL, K_COL, V_COL, ZA_COL, GA_COL, GC_COL = 0, 1, 2, 3, 4, 6
HEADS_PER_COL_BLK = COL_BLK // HEAD_W

PROJ_TM = 1024
PROJ_TN = COL_BLK
PROJ_ROWS = 256
Q_SCALE = DA_HEAD_DIM ** -0.5 * math.log2(math.e)
ATTN_KT = 256
ATTN_QCH = 4
ATTN_TQ = ATTN_QCH * ATTN_KT
CONV_T = 256
CONV_HALO = 32
CONV_ROWS = 64
MERGE_TM = 256
LANES = 128
SUBLANES = 8
MASK_VALUE = -1e30

VMEM_LIMIT_BYTES = 56 * 1024 * 1024

_BF16 = jnp.bfloat16
_F32 = jnp.float32


def _sigmoid(x):
    return 0.5 * jnp.tanh(0.5 * x) + 0.5


def _rotary(acc, cos, sa, sb):
    half = ROT_DIM // 2
    up = pltpu.roll(acc, LANES - half, 1)
    down = pltpu.roll(acc, half, 1)
    return acc * cos + up * sa + down * sb


_PROJ_ROW_CHUNKS = [slice(r, r + PROJ_ROWS) for r in range(0, PROJ_TM, PROJ_ROWS)]


def _proj_conv_kernel(x_ref, nw_ref, w_ref, out_ref, h_ref):
    j = pl.program_id(1)

    def normed(rs):
        x = x_ref[rs, :]
        ms = jnp.mean(x * x, axis=-1, keepdims=True)
        return (x * lax.rsqrt(ms + NORM_EPS) * nw_ref[...]).astype(_BF16)

    @pl.when(j == 0)
    def _():
        for rs in _PROJ_ROW_CHUNKS:
            h = normed(rs)
            h_ref[rs, :] = h
            out_ref[rs, :] = jnp.dot(h, w_ref[...], preferred_element_type=_F32).astype(_BF16)

    @pl.when(j > 0)
    def _():
        out_ref[...] = jnp.dot(h_ref[...], w_ref[...], preferred_element_type=_F32).astype(_BF16)


def _proj_rest_kernel(h_ref, w_ref, cos_ref, sa_ref, sb_ref, out_ref):
    j = pl.program_id(1)
    row_chunks = _PROJ_ROW_CHUNKS

    def store_rotated(acc, rs, scale):
        cos, sa, sb = cos_ref[rs, :], sa_ref[rs, :], sb_ref[rs, :]
        for hh in range(HEADS_PER_COL_BLK):
            sl = slice(hh * HEAD_W, (hh + 1) * HEAD_W)
            r = _rotary(acc[:, sl], cos, sa, sb)
            if scale != 1.0:
                r = r * scale
            out_ref[rs, sl] = r.astype(_BF16)

    @pl.when(j == Q_COL)
    def _():
        for rs in row_chunks:
            store_rotated(jnp.dot(h_ref[rs, :], w_ref[...], preferred_element_type=_F32), rs, Q_SCALE)

    @pl.when(j == K_COL)
    def _():
        for rs in row_chunks:
            store_rotated(jnp.dot(h_ref[rs, :], w_ref[...], preferred_element_type=_F32), rs, 1.0)

    @pl.when(j > K_COL)
    def _():
        out_ref[...] = jnp.dot(h_ref[...], w_ref[...], preferred_element_type=_F32).astype(_BF16)


_PROJ_PARAMS = pltpu.CompilerParams(dimension_semantics=("arbitrary", "arbitrary"),
                                    vmem_limit_bytes=VMEM_LIMIT_BYTES)


def _proj_conv(x2, norm_w, w_conv_bf):
    m = x2.shape[0]
    n_col = w_conv_bf.shape[1] // PROJ_TN
    return pl.pallas_call(
        _proj_conv_kernel,
        out_shape=(jax.ShapeDtypeStruct((m, n_col * PROJ_TN), _BF16),
                   jax.ShapeDtypeStruct((m, D_MODEL), _BF16)),
        grid=(m // PROJ_TM, n_col),
        in_specs=[
            pl.BlockSpec((PROJ_TM, D_MODEL), lambda i, j: (i, 0)),
            pl.BlockSpec((1, D_MODEL), lambda i, j: (0, 0)),
            pl.BlockSpec((D_MODEL, PROJ_TN), lambda i, j: (0, j)),
        ],
        out_specs=(pl.BlockSpec((PROJ_TM, PROJ_TN), lambda i, j: (i, j)),
                   pl.BlockSpec((PROJ_TM, D_MODEL), lambda i, j: (i, 0))),
        compiler_params=_PROJ_PARAMS,
        name="proj_conv",
    )(x2, norm_w, w_conv_bf)


def _proj_rest(h, w_rest_bf, cos_t, sa_t, sb_t, seq):
    m = h.shape[0]
    n_col = w_rest_bf.shape[1] // PROJ_TN
    seq_blocks = seq // PROJ_TM
    tab_spec = pl.BlockSpec((PROJ_TM, HEAD_W), lambda i, j: (i % seq_blocks, 0))
    return pl.pallas_call(
        _proj_rest_kernel,
        out_shape=jax.ShapeDtypeStruct((m, n_col * PROJ_TN), _BF16),
        grid=(m // PROJ_TM, n_col),
        in_specs=[
            pl.BlockSpec((PROJ_TM, D_MODEL), lambda i, j: (i, 0)),
            pl.BlockSpec((D_MODEL, PROJ_TN), lambda i, j: (0, j)),
            tab_spec, tab_spec, tab_spec,
        ],
        out_specs=pl.BlockSpec((PROJ_TM, PROJ_TN), lambda i, j: (i, j)),
        compiler_params=_PROJ_PARAMS,
        name="proj_rest",
    )(h, w_rest_bf, cos_t, sa_t, sb_t)


def _attn_kernel(q_ref, k_ref, v_ref, z_ref, lam_ref, sw_ref, o_ref,
                 qqt_ref, vt_ref, s_ref, m_ref, l_ref, acc_ref):
    tq, kt = ATTN_TQ, ATTN_KT
    nchunk = 2 * ATTN_QCH
    seq = q_ref.shape[0]

    for j in range(vt_ref.shape[0]):
        vt_ref[j] = v_ref[j * kt:(j + 1) * kt, :].astype(_F32).T.astype(_BF16)

    lam_p = lam_ref[...]
    s1 = jnp.sum(lam_p[0:1] * lam_p[1:2], axis=1, keepdims=True)
    s2 = jnp.sum(lam_p[2:3] * lam_p[3:4], axis=1, keepdims=True)
    lam = jnp.exp(s1) - jnp.exp(s2) + LAMBDA_INIT

    def scores(j, slot, col_ranges):
        k_sub = k_ref[pl.ds(pl.multiple_of(j * kt, kt), kt), :]
        for c0, w in col_ranges:
            s_ref[slot, :, c0:c0 + w] = jnp.dot(k_sub, qqt_ref[:, c0:c0 + w], preferred_element_type=_F32)

    def softmax_pv(j, slot, col_ranges):
        vt_sub = vt_ref[j]
        for c0, w, masked in col_ranges:
            cs = slice(c0, c0 + w)
            s = s_ref[slot, :, cs]
            if masked:
                kv = lax.broadcasted_iota(jnp.int32, (kt, w), 0)
                qq = lax.broadcasted_iota(jnp.int32, (kt, w), 1)
                s = jnp.where(kv <= qq, s, MASK_VALUE)
            m_prev = m_ref[:, cs]
            m_new = jnp.maximum(m_prev, jnp.max(s, axis=0, keepdims=True))
            alpha = jnp.exp2(m_prev - m_new)
            p = jnp.exp2(s - m_new)
            l_ref[:, cs] = alpha * l_ref[:, cs] + jnp.sum(p, axis=0, keepdims=True)
            acc_ref[:, cs] = alpha * acc_ref[:, cs] + jnp.dot(vt_sub, p.astype(_BF16),
                                                              preferred_element_type=_F32)
            m_ref[:, cs] = m_new

    all_cols = [(0, 2 * tq)]
    full = [(0, 2 * tq, False)]

    def query_tile(qi, carry):
        rows = pl.ds(pl.multiple_of(qi * tq, tq), tq)

        qt = q_ref[rows, :].astype(_F32).T
        row = lax.broadcasted_iota(jnp.int32, (HEAD_W, tq), 0)
        qqt_ref[:, 0:tq] = jnp.where(row < DA_HEAD_DIM, qt, 0.0).astype(_BF16)
        qqt_ref[:, tq:2 * tq] = jnp.where(row >= DA_HEAD_DIM, qt, 0.0).astype(_BF16)
        m_ref[...] = jnp.full(m_ref.shape, MASK_VALUE, _F32)
        l_ref[...] = jnp.zeros(l_ref.shape, _F32)
        acc_ref[...] = jnp.zeros(acc_ref.shape, _F32)

        scores(0, 0, all_cols)

        def body(i, c):
            j = 2 * i
            scores(j + 1, 1, all_cols)
            softmax_pv(j, 0, full)
            scores(j + 2, 0, all_cols)
            softmax_pv(j + 1, 1, full)
            return c

        nfull = ATTN_QCH * qi
        lax.fori_loop(0, nfull // 2, body, 0)
        for d in range(ATTN_QCH):
            chunks = [c for c in range(nchunk) if c % ATTN_QCH >= d]
            if d + 1 < ATTN_QCH:
                nxt = [(c * kt, kt) for c in range(nchunk) if c % ATTN_QCH >= d + 1]
                scores(nfull + d + 1, (d + 1) % 2, nxt)
            softmax_pv(nfull + d, d % 2, [(c * kt, kt, c % ATTN_QCH == d) for c in chunks])

        ot = acc_ref[...] / l_ref[...]
        ot = ot[:, 0:tq] - lam * ot[:, tq:2 * tq]
        ot = ot * lax.rsqrt(jnp.mean(ot * ot, axis=0, keepdims=True) + SUBLN_EPS)
        o = ot.T * sw_ref[...] * (1.0 - LAMBDA_INIT)
        z = z_ref[rows, :].astype(_F32)
        o_ref[rows, :] = (o * (z * _sigmoid(z))).astype(_BF16)
        return carry

    lax.fori_loop(0, seq // tq, query_tile, 0)


def _attn(pr, lam_params, subln_w, batch, seq):
    m = pr.shape[0]
    assert ATTN_TQ == ATTN_QCH * ATTN_KT and seq % ATTN_TQ == 0 and ATTN_QCH % 2 == 0
    hb = HEADS_PER_COL_BLK

    def head_cols(col):
        return pl.BlockSpec((seq, HEAD_W), lambda b, h: (b, col * hb + h))

    return pl.pallas_call(
        _attn_kernel,
        out_shape=jax.ShapeDtypeStruct((m, D_ATTN), _BF16),
        grid=(batch, DA_HEADS),
        in_specs=[
            head_cols(Q_COL), head_cols(K_COL), head_cols(V_COL), head_cols(ZA_COL),
            pl.BlockSpec((4, DA_HEAD_DIM), lambda b, h: (0, 0)),
            pl.BlockSpec((1, HEAD_W), lambda b, h: (0, 0)),
        ],
        out_specs=pl.BlockSpec((seq, HEAD_W), lambda b, h: (b, h)),
        scratch_shapes=[
            pltpu.VMEM((HEAD_W, 2 * ATTN_TQ), _BF16),
            pltpu.VMEM((seq // ATTN_KT, HEAD_W, ATTN_KT), _BF16),
            pltpu.VMEM((2, ATTN_KT, 2 * ATTN_TQ), _F32),
            pltpu.VMEM((1, 2 * ATTN_TQ), _F32),
            pltpu.VMEM((1, 2 * ATTN_TQ), _F32),
            pltpu.VMEM((HEAD_W, 2 * ATTN_TQ), _F32),
        ],
        compiler_params=pltpu.CompilerParams(
            dimension_semantics=("arbitrary", "arbitrary"),
            vmem_limit_bytes=VMEM_LIMIT_BYTES),
        name="attn",
    )(pr, pr, pr, pr, lam_params, subln_w)


def _conv_kernel(a_ref, g_ref, ah_ref, gh_ref, z_ref, bglu_ref, wdw_ref, bdw_ref, lng_ref, lnb_ref, *refs):
    n_rest = len(W_REST_BLKS)
    w_in_refs, (wo_ref, wp_ref, wout_ref) = refs[:n_rest], refs[n_rest:n_rest + 3]
    o_ref, wrest_bf_ref, wo_bf_ref, wp_bf_ref, wout_bf_ref, stage_ref, shift_ref, y_ref = refs[n_rest + 3:]
    for k, w_ref in enumerate(w_in_refs):
        wrest_bf_ref[:, k * COL_BLK:(k + 1) * COL_BLK] = w_ref[...].astype(_BF16)
    wo_bf_ref[...] = wo_ref[...].astype(_BF16)
    wp_bf_ref[...] = wp_ref[...].astype(_BF16)
    wout_bf_ref[...] = wout_ref[...].astype(_BF16)

    t = pl.program_id(1)
    ba = bglu_ref[:, 0:D_CONV]
    bg = bglu_ref[:, D_CONV:2 * D_CONV]

    def glu(a, g):
        return (a.astype(_F32) + ba) * _sigmoid(g.astype(_F32) + bg)

    halo = glu(ah_ref[...], gh_ref[...])
    halo = jnp.where(t > 0, halo, jnp.zeros_like(halo))
    cur = glu(a_ref[...], g_ref[...])
    nblk = D_CONV // LANES
    for c in range(nblk):
        ls = slice(c * LANES, (c + 1) * LANES)
        stage_ref[c, 0:CONV_HALO, :] = halo[:, ls]
        stage_ref[c, CONV_HALO:CONV_HALO + CONV_T, :] = cur[:, ls]

    first = CONV_HALO - (CONV_WIDTH - 1)
    shift_rows = shift_ref.shape[1]
    row_starts = range(0, CONV_T, CONV_ROWS)

    def channel_block(c, carry):
        for s in range(1, SUBLANES):
            shift_ref[s - 1] = stage_ref[c, s:s + shift_rows, :]
        accs = [jnp.broadcast_to(bdw_ref[c], (CONV_ROWS, LANES)) for _ in row_starts]
        for j in range(CONV_WIDTH):
            q, s = divmod(first + j, SUBLANES)
            w_j = jnp.broadcast_to(wdw_ref[c, j:j + 1, :], (CONV_ROWS, LANES))
            for k, r in enumerate(row_starts):
                lo = r + q * SUBLANES
                if s == 0:
                    win = stage_ref[c, lo:lo + CONV_ROWS, :]
                else:
                    win = shift_ref[s - 1, lo:lo + CONV_ROWS, :]
                accs[k] = accs[k] + win * w_j
        for k, r in enumerate(row_starts):
            y_ref[c, r:r + CONV_ROWS, :] = accs[k]
        return carry

    lax.fori_loop(0, nblk, channel_block, 0)

    y = jnp.concatenate([y_ref[c] for c in range(nblk)], axis=1)
    mu = jnp.mean(y, axis=-1, keepdims=True)
    yc = y - mu
    var = jnp.mean(yc * yc, axis=-1, keepdims=True)
    u = yc * lax.rsqrt(var + LN_EPS) * lng_ref[...] + lnb_ref[...]
    z = z_ref[...].astype(_F32)
    o_ref[...] = ((u * _sigmoid(u)) * (z * _sigmoid(z))).astype(_BF16)


def _conv(pc, b_glu, w_dw, b_dw, ln_g, ln_b, w_in, w_o, w_pw2, w_out, batch, seq):
    m = pc.shape[0]
    nt = seq // CONV_T
    steps = batch * nt
    halo_per_tile = CONV_T // CONV_HALO
    nblk = D_CONV // LANES
    w_blk = w_dw.reshape(CONV_WIDTH, nblk, LANES).transpose(1, 0, 2)
    b_blk = b_dw.reshape(nblk, 1, LANES)

    def halo_idx(col):
        return lambda b, t: (jnp.maximum((b * nt + t) * halo_per_tile - 1, 0), col * (COL_BLK // D_CONV))

    def vec(n):
        return pl.BlockSpec((1, n), lambda b, t: (0, 0))

    def row_slab(n_rows, n_cols, col_blk=0):
        rows = n_rows // steps
        assert rows * steps == n_rows and rows % (2 * SUBLANES) == 0
        return pl.BlockSpec((rows, n_cols), lambda b, t: (b * nt + t, col_blk))

    n_rest = len(W_REST_BLKS)
    small = (w_o, w_pw2, w_out)
    return pl.pallas_call(
        _conv_kernel,
        out_shape=(jax.ShapeDtypeStruct((m, D_CONV), _BF16),
                   jax.ShapeDtypeStruct((D_MODEL, n_rest * COL_BLK), _BF16),
                   *[jax.ShapeDtypeStruct(w.shape, _BF16) for w in small]),
        grid=(batch, nt),
        in_specs=[
            pl.BlockSpec((CONV_T, D_CONV), lambda b, t: (b * nt + t, GLU_A_COL)),
            pl.BlockSpec((CONV_T, D_CONV), lambda b, t: (b * nt + t, GLU_G_COL)),
            pl.BlockSpec((CONV_HALO, D_CONV), halo_idx(GLU_A_COL)),
            pl.BlockSpec((CONV_HALO, D_CONV), halo_idx(GLU_G_COL)),
            pl.BlockSpec((CONV_T, D_CONV), lambda b, t: (b * nt + t, ZC_COL)),
            vec(2 * D_CONV),
            pl.BlockSpec((nblk, CONV_WIDTH, LANES), lambda b, t: (0, 0, 0)),
            pl.BlockSpec((nblk, 1, LANES), lambda b, t: (0, 0, 0)),
            vec(D_CONV), vec(D_CONV),
            *[row_slab(D_MODEL, COL_BLK, blk) for blk in W_REST_BLKS],
            *[row_slab(*w.shape) for w in small],
        ],
        out_specs=(pl.BlockSpec((CONV_T, D_CONV), lambda b, t: (b * nt + t, 0)),
                   row_slab(D_MODEL, n_rest * COL_BLK),
                   *[row_slab(*w.shape) for w in small]),
        scratch_shapes=[
            pltpu.VMEM((nblk, CONV_HALO + CONV_T, LANES), _F32),
            pltpu.VMEM((SUBLANES - 1, CONV_HALO + CONV_T - SUBLANES, LANES), _F32),
            pltpu.VMEM((nblk, CONV_T, LANES), _F32),
        ],
        compiler_params=pltpu.CompilerParams(
            dimension_semantics=("arbitrary", "arbitrary"),
            vmem_limit_bytes=VMEM_LIMIT_BYTES),
        name="conv",
    )(pc, pc, pc, pc, pc, b_glu, w_blk, b_blk, ln_g, ln_b, *([w_in] * n_rest), *small)


def _merge_kernel(x_ref, o_ref, u_ref, ga0_ref, ga1_ref, gc0_ref, gc1_ref, wo_ref, wp_ref, bp_ref,
                  wout_ref, nw_ref, out_ref):
    ya = jnp.dot(o_ref[...], wo_ref[...], preferred_element_type=_F32)
    yc = jnp.dot(u_ref[...], wp_ref[...], preferred_element_type=_F32) + bp_ref[...]
    ga = jnp.concatenate([ga0_ref[...], ga1_ref[...]], axis=1).astype(_F32)
    gc = jnp.concatenate([gc0_ref[...], gc1_ref[...]], axis=1).astype(_F32)
    mix = _sigmoid(ga) * ya + _sigmoid(gc) * yc
    out = jnp.dot(mix.astype(_BF16), wout_ref[...], preferred_element_type=_F32)
    ms = jnp.mean(out * out, axis=-1, keepdims=True)
    out_ref[...] = x_ref[...] + out * lax.rsqrt(ms + NORM_EPS) * nw_ref[...]


def _merge(x2, o, u, proj, w_o_bf, w_pw2_bf, b_pw2, w_out_bf, norm_w):
    m = x2.shape[0]

    def const(shape):
        return pl.BlockSpec(shape, lambda i: (0, 0), pipeline_mode=pl.Buffered(1))

    def gate(col):
        return pl.BlockSpec((MERGE_TM, COL_BLK), lambda i: (i, col))

    return pl.pallas_call(
        _merge_kernel,
        out_shape=jax.ShapeDtypeStruct((m, D_MODEL), _F32),
        grid=(m // MERGE_TM,),
        in_specs=[
            pl.BlockSpec((MERGE_TM, D_MODEL), lambda i: (i, 0)),
            pl.BlockSpec((MERGE_TM, D_ATTN), lambda i: (i, 0)),
            pl.BlockSpec((MERGE_TM, D_CONV), lambda i: (i, 0)),
            gate(GA_COL), gate(GA_COL + 1), gate(GC_COL), gate(GC_COL + 1),
            const((D_ATTN, D_MODEL)),
            const((D_CONV, D_MODEL)),
            const((1, D_MODEL)),
            const((D_MODEL, D_MODEL)),
            const((1, D_MODEL)),
        ],
        out_specs=pl.BlockSpec((MERGE_TM, D_MODEL), lambda i: (i, 0)),
        compiler_params=pltpu.CompilerParams(
            dimension_semantics=("arbitrary",),
            vmem_limit_bytes=VMEM_LIMIT_BYTES),
        name="merge",
    )(x2, o, u, proj, proj, proj, proj, w_o_bf, w_pw2_bf, b_pw2, w_out_bf, norm_w)


def _rotary_tables(seq):
    f32 = np.float32
    half = ROT_DIM // 2
    inv_freq = f32(ROPE_THETA) ** (-np.arange(0, ROT_DIM, 2, dtype=f32) / f32(ROT_DIM))
    ang = np.arange(seq, dtype=f32)[:, None] * inv_freq[None, :]
    cos, sin = np.cos(ang).astype(f32), np.sin(ang).astype(f32)
    ones = np.ones((seq, DA_HEAD_DIM - ROT_DIM), f32)
    zeros = np.zeros((seq, DA_HEAD_DIM - ROT_DIM), f32)
    zh = np.zeros((seq, half), f32)
    cos_c = np.concatenate([cos, cos, ones], axis=1)
    sa_c = np.concatenate([-sin, zh, zeros], axis=1)
    sb_c = np.concatenate([zh, sin, zeros], axis=1)
    tile2 = lambda a: np.concatenate([a, a], axis=1)
    return tile2(cos_c), tile2(sa_c), tile2(sb_c)


def kernel(x, norm_pre_w, w_in, lambda_q1, lambda_k1, lambda_q2, lambda_k2, subln_w, w_o_attn, b_glu,
           w_dw, b_dw, ln_g, ln_b, w_pw2, b_pw2, w_out, norm_post_w):
    batch, seq, _ = x.shape
    assert norm_pre_w.shape[0] == 1, "single layer"
    assert seq % PROJ_TM == 0 and seq % ATTN_TQ == 0 and seq % CONV_T == 0
    x2 = x.reshape(batch * seq, D_MODEL)
    cos_t, sa_t, sb_t = _rotary_tables(seq)
    lam_params = jnp.concatenate([lambda_q1, lambda_k1, lambda_q2, lambda_k2], axis=0).astype(_F32)

    conv_cols = slice(W_CONV_BLKS[0] * COL_BLK, (W_CONV_BLKS[-1] + 1) * COL_BLK)
    pc, h = _proj_conv(x2, norm_pre_w, w_in[0][:, conv_cols].astype(_BF16))
    u, w_rest_bf, w_o_bf, w_pw2_bf, w_out_bf = _conv(pc, b_glu, w_dw[0], b_dw, ln_g, ln_b, w_in[0],
                                                     w_o_attn[0], w_pw2[0], w_out[0], batch, seq)
    pr = _proj_rest(h, w_rest_bf, cos_t, sa_t, sb_t, seq)
    o = _attn(pr, lam_params, subln_w, batch, seq)
    out = _merge(x2, o, u, pr, w_o_bf, w_pw2_bf, b_pw2, w_out_bf, norm_post_w)
    return out.reshape(batch, seq, D_MODEL)
```

```python
import math

import jax
import jax.numpy as jnp
import numpy as np
from jax import lax
from jax.experimental import pallas as pl
from jax.experimental.pallas import tpu as pltpu

D_MODEL = 2048
DA_HEADS = 8
DA_HEAD_DIM = 64
HEAD_W = 2 * DA_HEAD_DIM
D_ATTN = DA_HEADS * HEAD_W
ROPE_THETA = 500000.0
ROT_DIM = DA_HEAD_DIM // 4
D_CONV = 1024
CONV_WIDTH = 31
NORM_EPS = 1e-6
SUBLN_EPS = 1e-5
LN_EPS = 1e-5
LAMBDA_INIT = 0.8 - 0.6 * math.exp(-0.3 * 0)
D_IN = D_ATTN * 4 + 2 * D_CONV + D_CONV + 2 * D_MODEL

COL_BLK = 1024
W_CONV_BLKS = (4, 5, 6)
W_REST_BLKS = (0, 1, 2, 3, 7, 8, 9, 10)
GLU_A_COL, GLU_G_COL, ZC_COL = 0, 1, 2
Q_COL, K_COL, V_COL, ZA_COL, GA_COL, GC_COL = 0, 1, 2, 3, 4, 6
HEADS_PER_COL_BLK = COL_BLK // HEAD_W

PROJ_TM = 1024
PROJ_REST_TM = 2048
PROJ_TN = COL_BLK
PROJ_ROWS = 256
Q_SCALE = DA_HEAD_DIM ** -0.5 * math.log2(math.e)
ATTN_KT = 256
ATTN_QCH = 4
ATTN_TQ = ATTN_QCH * ATTN_KT
CONV_T = 256
CONV_HALO = 32
CONV_ROWS = 64
MERGE_TM = 512
MERGE_ROWS = 256
LANES = 128
SUBLANES = 8
MASK_VALUE = -1e30

VMEM_LIMIT_BYTES = 56 * 1024 * 1024

_BF16 = jnp.bfloat16
_F32 = jnp.float32


def _sigmoid(x):
    return 0.5 * jnp.tanh(0.5 * x) + 0.5


def _rotary(acc, cos, sa, sb):
    half = ROT_DIM // 2
    up = pltpu.roll(acc, LANES - half, 1)
    down = pltpu.roll(acc, half, 1)
    return acc * cos + up * sa + down * sb


_PROJ_ROW_CHUNKS = [slice(r, r + PROJ_ROWS) for r in range(0, PROJ_TM, PROJ_ROWS)]


def _proj_conv_kernel(x_ref, nw_ref, w_ref, out_ref, h_ref):
    j = pl.program_id(1)

    def normed(rs):
        x = x_ref[rs, :]
        ms = jnp.mean(x * x, axis=-1, keepdims=True)
        return (x * lax.rsqrt(ms + NORM_EPS) * nw_ref[...]).astype(_BF16)

    @pl.when(j == 0)
    def _():
        for rs in _PROJ_ROW_CHUNKS:
            h = normed(rs)
            h_ref[rs, :] = h
            out_ref[rs, :] = jnp.dot(h, w_ref[...], preferred_element_type=_F32).astype(_BF16)

    @pl.when(j > 0)
    def _():
        out_ref[...] = jnp.dot(h_ref[...], w_ref[...], preferred_element_type=_F32).astype(_BF16)


def _proj_rest_kernel(h_ref, w_ref, cos_ref, sa_ref, sb_ref, out_ref):
    j = pl.program_id(1)
    row_chunks = [slice(r, r + PROJ_ROWS) for r in range(0, h_ref.shape[0], PROJ_ROWS)]

    def store_rotated(acc, rs, scale):
        cos, sa, sb = cos_ref[rs, :], sa_ref[rs, :], sb_ref[rs, :]
        for hh in range(HEADS_PER_COL_BLK):
            sl = slice(hh * HEAD_W, (hh + 1) * HEAD_W)
            r = _rotary(acc[:, sl], cos, sa, sb)
            if scale != 1.0:
                r = r * scale
            out_ref[rs, sl] = r.astype(_BF16)

    @pl.when(j == Q_COL)
    def _():
        for rs in row_chunks:
            store_rotated(jnp.dot(h_ref[rs, :], w_ref[...], preferred_element_type=_F32), rs, Q_SCALE)

    @pl.when(j == K_COL)
    def _():
        for rs in row_chunks:
            store_rotated(jnp.dot(h_ref[rs, :], w_ref[...], preferred_element_type=_F32), rs, 1.0)

    @pl.when(j > K_COL)
    def _():
        out_ref[...] = jnp.dot(h_ref[...], w_ref[...], preferred_element_type=_F32).astype(_BF16)


_PROJ_PARAMS = pltpu.CompilerParams(dimension_semantics=("arbitrary", "arbitrary"),
                                    vmem_limit_bytes=VMEM_LIMIT_BYTES)


def _proj_conv(x2, norm_w, w_conv_bf):
    m = x2.shape[0]
    n_col = w_conv_bf.shape[1] // PROJ_TN
    return pl.pallas_call(
        _proj_conv_kernel,
        out_shape=(jax.ShapeDtypeStruct((m, n_col * PROJ_TN), _BF16),
                   jax.ShapeDtypeStruct((m, D_MODEL), _BF16)),
        grid=(m // PROJ_TM, n_col),
        in_specs=[
            pl.BlockSpec((PROJ_TM, D_MODEL), lambda i, j: (i, 0)),
            pl.BlockSpec((1, D_MODEL), lambda i, j: (0, 0)),
            pl.BlockSpec((D_MODEL, PROJ_TN), lambda i, j: (0, j)),
        ],
        out_specs=(pl.BlockSpec((PROJ_TM, PROJ_TN), lambda i, j: (i, j)),
                   pl.BlockSpec((PROJ_TM, D_MODEL), lambda i, j: (i, 0))),
        compiler_params=_PROJ_PARAMS,
        name="proj_conv",
    )(x2, norm_w, w_conv_bf)


def _proj_rest(h, w_rest_bf, cos_t, sa_t, sb_t, seq):
    m = h.shape[0]
    n_col = w_rest_bf.shape[1] // PROJ_TN
    tm = PROJ_REST_TM
    assert seq % tm == 0
    seq_blocks = seq // tm
    tab_spec = pl.BlockSpec((tm, HEAD_W), lambda i, j: (i % seq_blocks, 0))
    return pl.pallas_call(
        _proj_rest_kernel,
        out_shape=jax.ShapeDtypeStruct((m, n_col * PROJ_TN), _BF16),
        grid=(m // tm, n_col),
        in_specs=[
            pl.BlockSpec((tm, D_MODEL), lambda i, j: (i, 0)),
            pl.BlockSpec((D_MODEL, PROJ_TN), lambda i, j: (0, j)),
            tab_spec, tab_spec, tab_spec,
        ],
        out_specs=pl.BlockSpec((tm, PROJ_TN), lambda i, j: (i, j)),
        compiler_params=_PROJ_PARAMS,
        name="proj_rest",
    )(h, w_rest_bf, cos_t, sa_t, sb_t)


def _attn_kernel(q_ref, k_ref, v_ref, z_ref, lam_ref, sw_ref, o_ref,
                 qqt_ref, vt_ref, s_ref, m_ref, l_ref, acc_ref):
    tq, kt = ATTN_TQ, ATTN_KT
    nchunk = 2 * ATTN_QCH
    seq = q_ref.shape[0]

    for j in range(vt_ref.shape[0]):
        vt_ref[j] = v_ref[j * kt:(j + 1) * kt, :].astype(_F32).T.astype(_BF16)

    lam_p = lam_ref[...]
    s1 = jnp.sum(lam_p[0:1] * lam_p[1:2], axis=1, keepdims=True)
    s2 = jnp.sum(lam_p[2:3] * lam_p[3:4], axis=1, keepdims=True)
    lam = jnp.exp(s1) - jnp.exp(s2) + LAMBDA_INIT

    def scores(j, slot, col_ranges):
        k_sub = k_ref[pl.ds(pl.multiple_of(j * kt, kt), kt), :]
        for c0, w in col_ranges:
            s_ref[slot, :, c0:c0 + w] = jnp.dot(k_sub, qqt_ref[:, c0:c0 + w], preferred_element_type=_F32)

    def softmax_pv(j, slot, col_ranges):
        vt_sub = vt_ref[j]
        for c0, w, masked in col_ranges:
            cs = slice(c0, c0 + w)
            s = s_ref[slot, :, cs]
            if masked:
                kv = lax.broadcasted_iota(jnp.int32, (kt, w), 0)
                qq = lax.broadcasted_iota(jnp.int32, (kt, w), 1)
                s = jnp.where(kv <= qq, s, MASK_VALUE)
            m_prev = m_ref[:, cs]
            m_new = jnp.maximum(m_prev, jnp.max(s, axis=0, keepdims=True))
            alpha = jnp.exp2(m_prev - m_new)
            p = jnp.exp2(s - m_new)
            l_ref[:, cs] = alpha * l_ref[:, cs] + jnp.sum(p, axis=0, keepdims=True)
            acc_ref[:, cs] = alpha * acc_ref[:, cs] + jnp.dot(vt_sub, p.astype(_BF16),
                                                              preferred_element_type=_F32)
            m_ref[:, cs] = m_new

    all_cols = [(0, 2 * tq)]
    full = [(0, 2 * tq, False)]

    def query_tile(qi, carry):
        rows = pl.ds(pl.multiple_of(qi * tq, tq), tq)

        qt = q_ref[rows, :].astype(_F32).T
        row = lax.broadcasted_iota(jnp.int32, (HEAD_W, tq), 0)
        qqt_ref[:, 0:tq] = jnp.where(row < DA_HEAD_DIM, qt, 0.0).astype(_BF16)
        qqt_ref[:, tq:2 * tq] = jnp.where(row >= DA_HEAD_DIM, qt, 0.0).astype(_BF16)
        m_ref[...] = jnp.full(m_ref.shape, MASK_VALUE, _F32)
        l_ref[...] = jnp.zeros(l_ref.shape, _F32)
        acc_ref[...] = jnp.zeros(acc_ref.shape, _F32)

        scores(0, 0, all_cols)

        def body(i, c):
            j = 2 * i
            scores(j + 1, 1, all_cols)
            softmax_pv(j, 0, full)
            scores(j + 2, 0, all_cols)
            softmax_pv(j + 1, 1, full)
            return c

        nfull = ATTN_QCH * qi
        lax.fori_loop(0, nfull // 2, body, 0)
        for d in range(ATTN_QCH):
            chunks = [c for c in range(nchunk) if c % ATTN_QCH >= d]
            if d + 1 < ATTN_QCH:
                nxt = [(c * kt, kt) for c in range(nchunk) if c % ATTN_QCH >= d + 1]
                scores(nfull + d + 1, (d + 1) % 2, nxt)
            softmax_pv(nfull + d, d % 2, [(c * kt, kt, c % ATTN_QCH == d) for c in chunks])

        ot = acc_ref[...] / l_ref[...]
        ot = ot[:, 0:tq] - lam * ot[:, tq:2 * tq]
        ot = ot * lax.rsqrt(jnp.mean(ot * ot, axis=0, keepdims=True) + SUBLN_EPS)
        o = ot.T * sw_ref[...] * (1.0 - LAMBDA_INIT)
        z = z_ref[rows, :].astype(_F32)
        o_ref[rows, :] = (o * (z * _sigmoid(z))).astype(_BF16)
        return carry

    lax.fori_loop(0, seq // tq, query_tile, 0)


def _attn(pr, lam_params, subln_w, batch, seq):
    m = pr.shape[0]
    assert ATTN_TQ == ATTN_QCH * ATTN_KT and seq % ATTN_TQ == 0 and ATTN_QCH % 2 == 0
    hb = HEADS_PER_COL_BLK

    def head_cols(col):
        return pl.BlockSpec((seq, HEAD_W), lambda b, h: (b, col * hb + h))

    return pl.pallas_call(
        _attn_kernel,
        out_shape=jax.ShapeDtypeStruct((m, D_ATTN), _BF16),
        grid=(batch, DA_HEADS),
        in_specs=[
            head_cols(Q_COL), head_cols(K_COL), head_cols(V_COL), head_cols(ZA_COL),
            pl.BlockSpec((4, DA_HEAD_DIM), lambda b, h: (0, 0)),
            pl.BlockSpec((1, HEAD_W), lambda b, h: (0, 0)),
        ],
        out_specs=pl.BlockSpec((seq, HEAD_W), lambda b, h: (b, h)),
        scratch_shapes=[
            pltpu.VMEM((HEAD_W, 2 * ATTN_TQ), _BF16),
            pltpu.VMEM((seq // ATTN_KT, HEAD_W, ATTN_KT), _BF16),
            pltpu.VMEM((2, ATTN_KT, 2 * ATTN_TQ), _F32),
            pltpu.VMEM((1, 2 * ATTN_TQ), _F32),
            pltpu.VMEM((1, 2 * ATTN_TQ), _F32),
            pltpu.VMEM((HEAD_W, 2 * ATTN_TQ), _F32),
        ],
        compiler_params=pltpu.CompilerParams(
            dimension_semantics=("arbitrary", "arbitrary"),
            vmem_limit_bytes=VMEM_LIMIT_BYTES),
        name="attn",
    )(pr, pr, pr, pr, lam_params, subln_w)


def _conv_kernel(a_ref, g_ref, ah_ref, gh_ref, z_ref, bglu_ref, wdw_ref, bdw_ref, lng_ref, lnb_ref, *refs):
    n_rest = len(W_REST_BLKS)
    w_in_refs, (wo_ref, wp_ref, wout_ref) = refs[:n_rest], refs[n_rest:n_rest + 3]
    o_ref, wrest_bf_ref, wo_bf_ref, wp_bf_ref, wout_bf_ref, stage_ref, shift_ref, y_ref = refs[n_rest + 3:]
    for k, w_ref in enumerate(w_in_refs):
        wrest_bf_ref[:, k * COL_BLK:(k + 1) * COL_BLK] = w_ref[...].astype(_BF16)
    wo_bf_ref[...] = wo_ref[...].astype(_BF16)
    wp_bf_ref[...] = wp_ref[...].astype(_BF16)
    wout_bf_ref[...] = wout_ref[...].astype(_BF16)

    t = pl.program_id(1)
    ba = bglu_ref[:, 0:D_CONV]
    bg = bglu_ref[:, D_CONV:2 * D_CONV]

    def glu(a, g):
        return (a.astype(_F32) + ba) * _sigmoid(g.astype(_F32) + bg)

    halo = glu(ah_ref[...], gh_ref[...])
    halo = jnp.where(t > 0, halo, jnp.zeros_like(halo))
    cur = glu(a_ref[...], g_ref[...])
    nblk = D_CONV // LANES
    for c in range(nblk):
        ls = slice(c * LANES, (c + 1) * LANES)
        stage_ref[c, 0:CONV_HALO, :] = halo[:, ls]
        stage_ref[c, CONV_HALO:CONV_HALO + CONV_T, :] = cur[:, ls]

    first = CONV_HALO - (CONV_WIDTH - 1)
    shift_rows = shift_ref.shape[1]
    row_starts = range(0, CONV_T, CONV_ROWS)

    def channel_block(c, carry):
        for s in range(1, SUBLANES):
            shift_ref[s - 1] = stage_ref[c, s:s + shift_rows, :]
        accs = [jnp.broadcast_to(bdw_ref[c], (CONV_ROWS, LANES)) for _ in row_starts]
        for j in range(CONV_WIDTH):
            q, s = divmod(first + j, SUBLANES)
            w_j = jnp.broadcast_to(wdw_ref[c, j:j + 1, :], (CONV_ROWS, LANES))
            for k, r in enumerate(row_starts):
                lo = r + q * SUBLANES
                if s == 0:
                    win = stage_ref[c, lo:lo + CONV_ROWS, :]
                else:
                    win = shift_ref[s - 1, lo:lo + CONV_ROWS, :]
                accs[k] = accs[k] + win * w_j
        for k, r in enumerate(row_starts):
            y_ref[c, r:r + CONV_ROWS, :] = accs[k]
        return carry

    lax.fori_loop(0, nblk, channel_block, 0)

    y = jnp.concatenate([y_ref[c] for c in range(nblk)], axis=1)
    mu = jnp.mean(y, axis=-1, keepdims=True)
    yc = y - mu
    var = jnp.mean(yc * yc, axis=-1, keepdims=True)
    u = yc * lax.rsqrt(var + LN_EPS) * lng_ref[...] + lnb_ref[...]
    z = z_ref[...].astype(_F32)
    o_ref[...] = ((u * _sigmoid(u)) * (z * _sigmoid(z))).astype(_BF16)


def _conv(pc, b_glu, w_dw, b_dw, ln_g, ln_b, w_in, w_o, w_pw2, w_out, batch, seq):
    m = pc.shape[0]
    nt = seq // CONV_T
    steps = batch * nt
    halo_per_tile = CONV_T // CONV_HALO
    nblk = D_CONV // LANES
    w_blk = w_dw.reshape(CONV_WIDTH, nblk, LANES).transpose(1, 0, 2)
    b_blk = b_dw.reshape(nblk, 1, LANES)

    def halo_idx(col):
        return lambda b, t: (jnp.maximum((b * nt + t) * halo_per_tile - 1, 0), col * (COL_BLK // D_CONV))

    def vec(n):
        return pl.BlockSpec((1, n), lambda b, t: (0, 0))

    def row_slab(n_rows, n_cols, col_blk=0):
        rows = n_rows // steps
        assert rows * steps == n_rows and rows % (2 * SUBLANES) == 0
        return pl.BlockSpec((rows, n_cols), lambda b, t: (b * nt + t, col_blk))

    n_rest = len(W_REST_BLKS)
    small = (w_o, w_pw2, w_out)
    return pl.pallas_call(
        _conv_kernel,
        out_shape=(jax.ShapeDtypeStruct((m, D_CONV), _BF16),
                   jax.ShapeDtypeStruct((D_MODEL, n_rest * COL_BLK), _BF16),
                   *[jax.ShapeDtypeStruct(w.shape, _BF16) for w in small]),
        grid=(batch, nt),
        in_specs=[
            pl.BlockSpec((CONV_T, D_CONV), lambda b, t: (b * nt + t, GLU_A_COL)),
            pl.BlockSpec((CONV_T, D_CONV), lambda b, t: (b * nt + t, GLU_G_COL)),
            pl.BlockSpec((CONV_HALO, D_CONV), halo_idx(GLU_A_COL)),
            pl.BlockSpec((CONV_HALO, D_CONV), halo_idx(GLU_G_COL)),
            pl.BlockSpec((CONV_T, D_CONV), lambda b, t: (b * nt + t, ZC_COL)),
            vec(2 * D_CONV),
            pl.BlockSpec((nblk, CONV_WIDTH, LANES), lambda b, t: (0, 0, 0)),
            pl.BlockSpec((nblk, 1, LANES), lambda b, t: (0, 0, 0)),
            vec(D_CONV), vec(D_CONV),
            *[row_slab(D_MODEL, COL_BLK, blk) for blk in W_REST_BLKS],
            *[row_slab(*w.shape) for w in small],
        ],
        out_specs=(pl.BlockSpec((CONV_T, D_CONV), lambda b, t: (b * nt + t, 0)),
                   row_slab(D_MODEL, n_rest * COL_BLK),
                   *[row_slab(*w.shape) for w in small]),
        scratch_shapes=[
            pltpu.VMEM((nblk, CONV_HALO + CONV_T, LANES), _F32),
            pltpu.VMEM((SUBLANES - 1, CONV_HALO + CONV_T - SUBLANES, LANES), _F32),
            pltpu.VMEM((nblk, CONV_T, LANES), _F32),
        ],
        compiler_params=pltpu.CompilerParams(
            dimension_semantics=("arbitrary", "arbitrary"),
            vmem_limit_bytes=VMEM_LIMIT_BYTES),
        name="conv",
    )(pc, pc, pc, pc, pc, b_glu, w_blk, b_blk, ln_g, ln_b, *([w_in] * n_rest), *small)


def _merge_kernel(x_ref, o_ref, u_ref, ga0_ref, ga1_ref, gc0_ref, gc1_ref, wo_ref, wp_ref, bp_ref,
                  wout_ref, nw_ref, out_ref):
    for r in range(0, MERGE_TM, MERGE_ROWS):
        rs = slice(r, r + MERGE_ROWS)
        ya = jnp.dot(o_ref[rs, :], wo_ref[...], preferred_element_type=_F32)
        yc = jnp.dot(u_ref[rs, :], wp_ref[...], preferred_element_type=_F32) + bp_ref[...]
        ga = jnp.concatenate([ga0_ref[rs, :], ga1_ref[rs, :]], axis=1).astype(_F32)
        gc = jnp.concatenate([gc0_ref[rs, :], gc1_ref[rs, :]], axis=1).astype(_F32)
        mix = _sigmoid(ga) * ya + _sigmoid(gc) * yc
        out = jnp.dot(mix.astype(_BF16), wout_ref[...], preferred_element_type=_F32)
        ms = jnp.mean(out * out, axis=-1, keepdims=True)
        out_ref[rs, :] = x_ref[rs, :] + out * lax.rsqrt(ms + NORM_EPS) * nw_ref[...]


def _merge(x2, o, u, proj, w_o_bf, w_pw2_bf, b_pw2, w_out_bf, norm_w):
    m = x2.shape[0]

    def const(shape):
        return pl.BlockSpec(shape, lambda i: (0, 0), pipeline_mode=pl.Buffered(1))

    def gate(col):
        return pl.BlockSpec((MERGE_TM, COL_BLK), lambda i: (i, col))

    return pl.pallas_call(
        _merge_kernel,
        out_shape=jax.ShapeDtypeStruct((m, D_MODEL), _F32),
        grid=(m // MERGE_TM,),
        in_specs=[
            pl.BlockSpec((MERGE_TM, D_MODEL), lambda i: (i, 0)),
            pl.BlockSpec((MERGE_TM, D_ATTN), lambda i: (i, 0)),
            pl.BlockSpec((MERGE_TM, D_CONV), lambda i: (i, 0)),
            gate(GA_COL), gate(GA_COL + 1), gate(GC_COL), gate(GC_COL + 1),
            const((D_ATTN, D_MODEL)),
            const((D_CONV, D_MODEL)),
            const((1, D_MODEL)),
            const((D_MODEL, D_MODEL)),
            const((1, D_MODEL)),
        ],
        out_specs=pl.BlockSpec((MERGE_TM, D_MODEL), lambda i: (i, 0)),
        compiler_params=pltpu.CompilerParams(
            dimension_semantics=("arbitrary",),
            vmem_limit_bytes=VMEM_LIMIT_BYTES),
        name="merge",
    )(x2, o, u, proj, proj, proj, proj, w_o_bf, w_pw2_bf, b_pw2, w_out_bf, norm_w)


def _rotary_tables(seq):
    f32 = np.float32
    half = ROT_DIM // 2
    inv_freq = f32(ROPE_THETA) ** (-np.arange(0, ROT_DIM, 2, dtype=f32) / f32(ROT_DIM))
    ang = np.arange(seq, dtype=f32)[:, None] * inv_freq[None, :]
    cos, sin = np.cos(ang).astype(f32), np.sin(ang).astype(f32)
    ones = np.ones((seq, DA_HEAD_DIM - ROT_DIM), f32)
    zeros = np.zeros((seq, DA_HEAD_DIM - ROT_DIM), f32)
    zh = np.zeros((seq, half), f32)
    cos_c = np.concatenate([cos, cos, ones], axis=1)
    sa_c = np.concatenate([-sin, zh, zeros], axis=1)
    sb_c = np.concatenate([zh, sin, zeros], axis=1)
    tile2 = lambda a: np.concatenate([a, a], axis=1)
    return tile2(cos_c), tile2(sa_c), tile2(sb_c)


def kernel(x, norm_pre_w, w_in, lambda_q1, lambda_k1, lambda_q2, lambda_k2, subln_w, w_o_attn, b_glu,
           w_dw, b_dw, ln_g, ln_b, w_pw2, b_pw2, w_out, norm_post_w):
    batch, seq, _ = x.shape
    assert norm_pre_w.shape[0] == 1, "single layer"
    assert seq % PROJ_TM == 0 and seq % ATTN_TQ == 0 and seq % CONV_T == 0
    x2 = x.reshape(batch * seq, D_MODEL)
    cos_t, sa_t, sb_t = _rotary_tables(seq)
    lam_params = jnp.concatenate([lambda_q1, lambda_k1, lambda_q2, lambda_k2], axis=0).astype(_F32)

    conv_cols = slice(W_CONV_BLKS[0] * COL_BLK, (W_CONV_BLKS[-1] + 1) * COL_BLK)
    pc, h = _proj_conv(x2, norm_pre_w, w_in[0][:, conv_cols].astype(_BF16))
    u, w_rest_bf, w_o_bf, w_pw2_bf, w_out_bf = _conv(pc, b_glu, w_dw[0], b_dw, ln_g, ln_b, w_in[0],
                                                     w_o_attn[0], w_pw2[0], w_out[0], batch, seq)
    pr = _proj_rest(h, w_rest_bf, cos_t, sa_t, sb_t, seq)
    o = _attn(pr, lam_params, subln_w, batch, seq)
    out = _merge(x2, o, u, pr, w_o_bf, w_pw2_bf, b_pw2, w_out_bf, norm_post_w)
    return out.reshape(batch, seq, D_MODEL)
```

```python
import math

import jax
import jax.numpy as jnp
import numpy as np
from jax import lax
from jax.experimental import pallas as pl
from jax.experimental.pallas import tpu as pltpu

D_MODEL = 2048
DA_HEADS = 8
DA_HEAD_DIM = 64
HEAD_W = 2 * DA_HEAD_DIM
D_ATTN = DA_HEADS * HEAD_W
ROPE_THETA = 500000.0
ROT_DIM = DA_HEAD_DIM // 4
D_CONV = 1024
CONV_WIDTH = 31
NORM_EPS = 1e-6
SUBLN_EPS = 1e-5
LN_EPS = 1e-5
LAMBDA_INIT = 0.8 - 0.6 * math.exp(-0.3 * 0)
D_IN = D_ATTN * 4 + 2 * D_CONV + D_CONV + 2 * D_MODEL

COL_BLK = 1024
W_CONV_BLKS = (4, 5, 6)
W_REST_BLKS = (0, 1, 2, 3, 7, 8, 9, 10)
GLU_A_COL, GLU_G_COL, ZC_COL = 0, 1, 2
Q_COL, K_COL, V_COL, ZA_COL, GA_COL, GC_COL = 0, 1, 2, 3, 4, 6
HEADS_PER_COL_BLK = COL_BLK // HEAD_W

PROJ_TM = 1024
PROJ_REST_TM = 1024
PROJ_TN = COL_BLK
PROJ_ROWS = 256
Q_SCALE = DA_HEAD_DIM ** -0.5 * math.log2(math.e)
ATTN_KT = 256
ATTN_QCH = 4
ATTN_TQ = ATTN_QCH * ATTN_KT
ATTN_DIAG_W = 256
ATTN_PASS_W = 256
CONV_T = 256
CONV_HALO = 32
CONV_ROWS = 64
MERGE_TM = 256
MERGE_ROWS = 256
LANES = 128
SUBLANES = 8
MASK_VALUE = -1e30

VMEM_LIMIT_BYTES = 56 * 1024 * 1024

_BF16 = jnp.bfloat16
_F32 = jnp.float32


def _sigmoid(x):
    return 0.5 * jnp.tanh(0.5 * x) + 0.5


def _rotary(acc, cos, sa, sb):
    half = ROT_DIM // 2
    up = pltpu.roll(acc, LANES - half, 1)
    down = pltpu.roll(acc, half, 1)
    return acc * cos + up * sa + down * sb


_PROJ_ROW_CHUNKS = [slice(r, r + PROJ_ROWS) for r in range(0, PROJ_TM, PROJ_ROWS)]


def _proj_conv_kernel(x_ref, nw_ref, w_ref, out_ref, h_ref):
    j = pl.program_id(1)

    def normed(rs):
        x = x_ref[rs, :]
        ms = jnp.mean(x * x, axis=-1, keepdims=True)
        return (x * lax.rsqrt(ms + NORM_EPS) * nw_ref[...]).astype(_BF16)

    @pl.when(j == 0)
    def _():
        for rs in _PROJ_ROW_CHUNKS:
            h = normed(rs)
            h_ref[rs, :] = h
            out_ref[rs, :] = jnp.dot(h, w_ref[...], preferred_element_type=_F32).astype(_BF16)

    @pl.when(j > 0)
    def _():
        out_ref[...] = jnp.dot(h_ref[...], w_ref[...], preferred_element_type=_F32).astype(_BF16)


def _proj_rest_kernel(h_ref, w_ref, cos_ref, sa_ref, sb_ref, out_ref):
    j = pl.program_id(1)
    row_chunks = [slice(r, r + PROJ_ROWS) for r in range(0, h_ref.shape[0], PROJ_ROWS)]

    def store_rotated(acc, rs, scale):
        cos, sa, sb = cos_ref[rs, :], sa_ref[rs, :], sb_ref[rs, :]
        for hh in range(HEADS_PER_COL_BLK):
            sl = slice(hh * HEAD_W, (hh + 1) * HEAD_W)
            r = _rotary(acc[:, sl], cos, sa, sb)
            if scale != 1.0:
                r = r * scale
            out_ref[rs, sl] = r.astype(_BF16)

    @pl.when(j == Q_COL)
    def _():
        for rs in row_chunks:
            store_rotated(jnp.dot(h_ref[rs, :], w_ref[...], preferred_element_type=_F32), rs, Q_SCALE)

    @pl.when(j == K_COL)
    def _():
        for rs in row_chunks:
            store_rotated(jnp.dot(h_ref[rs, :], w_ref[...], preferred_element_type=_F32), rs, 1.0)

    @pl.when(j > K_COL)
    def _():
        out_ref[...] = jnp.dot(h_ref[...], w_ref[...], preferred_element_type=_F32).astype(_BF16)


_PROJ_PARAMS = pltpu.CompilerParams(dimension_semantics=("arbitrary", "arbitrary"),
                                    vmem_limit_bytes=VMEM_LIMIT_BYTES)


def _proj_conv(x2, norm_w, w_conv_bf):
    m = x2.shape[0]
    n_col = w_conv_bf.shape[1] // PROJ_TN
    return pl.pallas_call(
        _proj_conv_kernel,
        out_shape=(jax.ShapeDtypeStruct((m, n_col * PROJ_TN), _BF16),
                   jax.ShapeDtypeStruct((m, D_MODEL), _BF16)),
        grid=(m // PROJ_TM, n_col),
        in_specs=[
            pl.BlockSpec((PROJ_TM, D_MODEL), lambda i, j: (i, 0)),
            pl.BlockSpec((1, D_MODEL), lambda i, j: (0, 0)),
            pl.BlockSpec((D_MODEL, PROJ_TN), lambda i, j: (0, j)),
        ],
        out_specs=(pl.BlockSpec((PROJ_TM, PROJ_TN), lambda i, j: (i, j)),
                   pl.BlockSpec((PROJ_TM, D_MODEL), lambda i, j: (i, 0))),
        compiler_params=_PROJ_PARAMS,
        name="proj_conv",
    )(x2, norm_w, w_conv_bf)


def _proj_rest(h, w_rest_bf, cos_t, sa_t, sb_t, seq):
    m = h.shape[0]
    n_col = w_rest_bf.shape[1] // PROJ_TN
    tm = PROJ_REST_TM
    assert seq % tm == 0
    seq_blocks = seq // tm
    tab_spec = pl.BlockSpec((tm, HEAD_W), lambda i, j: (i % seq_blocks, 0))
    return pl.pallas_call(
        _proj_rest_kernel,
        out_shape=jax.ShapeDtypeStruct((m, n_col * PROJ_TN), _BF16),
        grid=(m // tm, n_col),
        in_specs=[
            pl.BlockSpec((tm, D_MODEL), lambda i, j: (i, 0)),
            pl.BlockSpec((D_MODEL, PROJ_TN), lambda i, j: (0, j)),
            tab_spec, tab_spec, tab_spec,
        ],
        out_specs=pl.BlockSpec((tm, PROJ_TN), lambda i, j: (i, j)),
        compiler_params=_PROJ_PARAMS,
        name="proj_rest",
    )(h, w_rest_bf, cos_t, sa_t, sb_t)


def _attn_kernel(q_ref, k_ref, v_ref, z_ref, lam_ref, sw_ref, o_ref,
                 qqt_ref, vt_ref, s_ref, m_ref, l_ref, acc_ref):
    tq, kt = ATTN_TQ, ATTN_KT
    nchunk = 2 * ATTN_QCH
    seq = q_ref.shape[0]

    for j in range(vt_ref.shape[0]):
        vt_ref[j] = v_ref[j * kt:(j + 1) * kt, :].astype(_F32).T.astype(_BF16)

    lam_p = lam_ref[...]
    s1 = jnp.sum(lam_p[0:1] * lam_p[1:2], axis=1, keepdims=True)
    s2 = jnp.sum(lam_p[2:3] * lam_p[3:4], axis=1, keepdims=True)
    lam = jnp.exp(s1) - jnp.exp(s2) + LAMBDA_INIT

    def scores(j, slot, col_ranges):
        k_sub = k_ref[pl.ds(pl.multiple_of(j * kt, kt), kt), :]
        for c0, w in col_ranges:
            s_ref[slot, :, c0:c0 + w] = jnp.dot(k_sub, qqt_ref[:, c0:c0 + w], preferred_element_type=_F32)

    def softmax_pv(j, slot, col_ranges):
        for c0, w, diag in col_ranges:
            cs = slice(c0, c0 + w)
            rows = kt if diag is None else diag + w
            s = s_ref[slot, 0:rows, cs]
            if diag is not None:
                kv = lax.broadcasted_iota(jnp.int32, (rows, w), 0)
                qq = lax.broadcasted_iota(jnp.int32, (rows, w), 1) + diag
                s = jnp.where(kv <= qq, s, MASK_VALUE)
            m_prev = m_ref[:, cs]
            m_new = jnp.maximum(m_prev, jnp.max(s, axis=0, keepdims=True))
            alpha = jnp.exp2(m_prev - m_new)
            p = jnp.exp2(s - m_new)
            l_ref[:, cs] = alpha * l_ref[:, cs] + jnp.sum(p, axis=0, keepdims=True)
            acc_ref[:, cs] = alpha * acc_ref[:, cs] + jnp.dot(vt_ref[j, :, 0:rows], p.astype(_BF16),
                                                              preferred_element_type=_F32)
            m_ref[:, cs] = m_new

    all_cols = [(0, 2 * tq)]
    full = [(c0, ATTN_PASS_W, None) for c0 in range(0, 2 * tq, ATTN_PASS_W)]

    def diag_ranges(d):
        out = []
        for c in range(nchunk):
            if c % ATTN_QCH > d:
                out.append((c * kt, kt, None))
            elif c % ATTN_QCH == d:
                out.extend((c * kt + off, ATTN_DIAG_W, off) for off in range(0, kt, ATTN_DIAG_W))
        return out

    def query_tile(qi, carry):
        rows = pl.ds(pl.multiple_of(qi * tq, tq), tq)

        qt = q_ref[rows, :].astype(_F32).T
        row = lax.broadcasted_iota(jnp.int32, (HEAD_W, tq), 0)
        qqt_ref[:, 0:tq] = jnp.where(row < DA_HEAD_DIM, qt, 0.0).astype(_BF16)
        qqt_ref[:, tq:2 * tq] = jnp.where(row >= DA_HEAD_DIM, qt, 0.0).astype(_BF16)
        m_ref[...] = jnp.full(m_ref.shape, MASK_VALUE, _F32)
        l_ref[...] = jnp.zeros(l_ref.shape, _F32)
        acc_ref[...] = jnp.zeros(acc_ref.shape, _F32)

        scores(0, 0, all_cols)

        def body(i, c):
            j = 2 * i
            scores(j + 1, 1, all_cols)
            softmax_pv(j, 0, full)
            scores(j + 2, 0, all_cols)
            softmax_pv(j + 1, 1, full)
            return c

        nfull = ATTN_QCH * qi
        lax.fori_loop(0, nfull // 2, body, 0)
        for d in range(ATTN_QCH):
            chunks = [c for c in range(nchunk) if c % ATTN_QCH >= d]
            if d + 1 < ATTN_QCH:
                nxt = [(c * kt, kt) for c in range(nchunk) if c % ATTN_QCH >= d + 1]
                scores(nfull + d + 1, (d + 1) % 2, nxt)
            softmax_pv(nfull + d, d % 2, diag_ranges(d))

        ot = acc_ref[...] / l_ref[...]
        ot = ot[:, 0:tq] - lam * ot[:, tq:2 * tq]
        ot = ot * lax.rsqrt(jnp.mean(ot * ot, axis=0, keepdims=True) + SUBLN_EPS)
        o = ot.T * sw_ref[...] * (1.0 - LAMBDA_INIT)
        z = z_ref[rows, :].astype(_F32)
        o_ref[rows, :] = (o * (z * _sigmoid(z))).astype(_BF16)
        return carry

    lax.fori_loop(0, seq // tq, query_tile, 0)


def _attn(pr, lam_params, subln_w, batch, seq):
    m = pr.shape[0]
    assert ATTN_TQ == ATTN_QCH * ATTN_KT and seq % ATTN_TQ == 0 and ATTN_QCH % 2 == 0
    hb = HEADS_PER_COL_BLK

    def head_cols(col):
        return pl.BlockSpec((seq, HEAD_W), lambda b, h: (b, col * hb + h))

    return pl.pallas_call(
        _attn_kernel,
        out_shape=jax.ShapeDtypeStruct((m, D_ATTN), _BF16),
        grid=(batch, DA_HEADS),
        in_specs=[
            head_cols(Q_COL), head_cols(K_COL), head_cols(V_COL), head_cols(ZA_COL),
            pl.BlockSpec((4, DA_HEAD_DIM), lambda b, h: (0, 0)),
            pl.BlockSpec((1, HEAD_W), lambda b, h: (0, 0)),
        ],
        out_specs=pl.BlockSpec((seq, HEAD_W), lambda b, h: (b, h)),
        scratch_shapes=[
            pltpu.VMEM((HEAD_W, 2 * ATTN_TQ), _BF16),
            pltpu.VMEM((seq // ATTN_KT, HEAD_W, ATTN_KT), _BF16),
            pltpu.VMEM((2, ATTN_KT, 2 * ATTN_TQ), _F32),
            pltpu.VMEM((1, 2 * ATTN_TQ), _F32),
            pltpu.VMEM((1, 2 * ATTN_TQ), _F32),
            pltpu.VMEM((HEAD_W, 2 * ATTN_TQ), _F32),
        ],
        compiler_params=pltpu.CompilerParams(
            dimension_semantics=("arbitrary", "arbitrary"),
            vmem_limit_bytes=VMEM_LIMIT_BYTES),
        name="attn",
    )(pr, pr, pr, pr, lam_params, subln_w)


def _conv_kernel(a_ref, g_ref, ah_ref, gh_ref, z_ref, bglu_ref, wdw_ref, bdw_ref, lng_ref, lnb_ref, *refs):
    n_rest = len(W_REST_BLKS)
    w_in_refs, (wo_ref, wp_ref, wout_ref) = refs[:n_rest], refs[n_rest:n_rest + 3]
    o_ref, wrest_bf_ref, wo_bf_ref, wp_bf_ref, wout_bf_ref, stage_ref, shift_ref, y_ref = refs[n_rest + 3:]
    for k, w_ref in enumerate(w_in_refs):
        wrest_bf_ref[:, k * COL_BLK:(k + 1) * COL_BLK] = w_ref[...].astype(_BF16)
    wo_bf_ref[...] = wo_ref[...].astype(_BF16)
    wp_bf_ref[...] = wp_ref[...].astype(_BF16)
    wout_bf_ref[...] = wout_ref[...].astype(_BF16)

    t = pl.program_id(1)
    ba = bglu_ref[:, 0:D_CONV]
    bg = bglu_ref[:, D_CONV:2 * D_CONV]

    def glu(a, g):
        return (a.astype(_F32) + ba) * _sigmoid(g.astype(_F32) + bg)

    halo = glu(ah_ref[...], gh_ref[...])
    halo = jnp.where(t > 0, halo, jnp.zeros_like(halo))
    cur = glu(a_ref[...], g_ref[...])
    nblk = D_CONV // LANES
    for c in range(nblk):
        ls = slice(c * LANES, (c + 1) * LANES)
        stage_ref[c, 0:CONV_HALO, :] = halo[:, ls]
        stage_ref[c, CONV_HALO:CONV_HALO + CONV_T, :] = cur[:, ls]

    first = CONV_HALO - (CONV_WIDTH - 1)
    shift_rows = shift_ref.shape[1]
    row_starts = range(0, CONV_T, CONV_ROWS)

    def channel_block(c, carry):
        for s in range(1, SUBLANES):
            shift_ref[s - 1] = stage_ref[c, s:s + shift_rows, :]
        accs = [jnp.broadcast_to(bdw_ref[c], (CONV_ROWS, LANES)) for _ in row_starts]
        for j in range(CONV_WIDTH):
            q, s = divmod(first + j, SUBLANES)
            w_j = jnp.broadcast_to(wdw_ref[c, j:j + 1, :], (CONV_ROWS, LANES))
            for k, r in enumerate(row_starts):
                lo = r + q * SUBLANES
                if s == 0:
                    win = stage_ref[c, lo:lo + CONV_ROWS, :]
                else:
                    win = shift_ref[s - 1, lo:lo + CONV_ROWS, :]
                accs[k] = accs[k] + win * w_j
        for k, r in enumerate(row_starts):
            y_ref[c, r:r + CONV_ROWS, :] = accs[k]
        return carry

    lax.fori_loop(0, nblk, channel_block, 0)

    y = jnp.concatenate([y_ref[c] for c in range(nblk)], axis=1)
    mu = jnp.mean(y, axis=-1, keepdims=True)
    yc = y - mu
    var = jnp.mean(yc * yc, axis=-1, keepdims=True)
    u = yc * lax.rsqrt(var + LN_EPS) * lng_ref[...] + lnb_ref[...]
    z = z_ref[...].astype(_F32)
    o_ref[...] = ((u * _sigmoid(u)) * (z * _sigmoid(z))).astype(_BF16)


def _conv(pc, b_glu, w_dw, b_dw, ln_g, ln_b, w_in, w_o, w_pw2, w_out, batch, seq):
    m = pc.shape[0]
    nt = seq // CONV_T
    steps = batch * nt
    halo_per_tile = CONV_T // CONV_HALO
    nblk = D_CONV // LANES
    w_blk = w_dw.reshape(CONV_WIDTH, nblk, LANES).transpose(1, 0, 2)
    b_blk = b_dw.reshape(nblk, 1, LANES)

    def halo_idx(col):
        return lambda b, t: (jnp.maximum((b * nt + t) * halo_per_tile - 1, 0), col * (COL_BLK // D_CONV))

    def vec(n):
        return pl.BlockSpec((1, n), lambda b, t: (0, 0))

    def row_slab(n_rows, n_cols, col_blk=0):
        rows = n_rows // steps
        assert rows * steps == n_rows and rows % (2 * SUBLANES) == 0
        return pl.BlockSpec((rows, n_cols), lambda b, t: (b * nt + t, col_blk))

    n_rest = len(W_REST_BLKS)
    small = (w_o, w_pw2, w_out)
    return pl.pallas_call(
        _conv_kernel,
        out_shape=(jax.ShapeDtypeStruct((m, D_CONV), _BF16),
                   jax.ShapeDtypeStruct((D_MODEL, n_rest * COL_BLK), _BF16),
                   *[jax.ShapeDtypeStruct(w.shape, _BF16) for w in small]),
        grid=(batch, nt),
        in_specs=[
            pl.BlockSpec((CONV_T, D_CONV), lambda b, t: (b * nt + t, GLU_A_COL)),
            pl.BlockSpec((CONV_T, D_CONV), lambda b, t: (b * nt + t, GLU_G_COL)),
            pl.BlockSpec((CONV_HALO, D_CONV), halo_idx(GLU_A_COL)),
            pl.BlockSpec((CONV_HALO, D_CONV), halo_idx(GLU_G_COL)),
            pl.BlockSpec((CONV_T, D_CONV), lambda b, t: (b * nt + t, ZC_COL)),
            vec(2 * D_CONV),
            pl.BlockSpec((nblk, CONV_WIDTH, LANES), lambda b, t: (0, 0, 0)),
            pl.BlockSpec((nblk, 1, LANES), lambda b, t: (0, 0, 0)),
            vec(D_CONV), vec(D_CONV),
            *[row_slab(D_MODEL, COL_BLK, blk) for blk in W_REST_BLKS],
            *[row_slab(*w.shape) for w in small],
        ],
        out_specs=(pl.BlockSpec((CONV_T, D_CONV), lambda b, t: (b * nt + t, 0)),
                   row_slab(D_MODEL, n_rest * COL_BLK),
                   *[row_slab(*w.shape) for w in small]),
        scratch_shapes=[
            pltpu.VMEM((nblk, CONV_HALO + CONV_T, LANES), _F32),
            pltpu.VMEM((SUBLANES - 1, CONV_HALO + CONV_T - SUBLANES, LANES), _F32),
            pltpu.VMEM((nblk, CONV_T, LANES), _F32),
        ],
        compiler_params=pltpu.CompilerParams(
            dimension_semantics=("arbitrary", "arbitrary"),
            vmem_limit_bytes=VMEM_LIMIT_BYTES),
        name="conv",
    )(pc, pc, pc, pc, pc, b_glu, w_blk, b_blk, ln_g, ln_b, *([w_in] * n_rest), *small)


def _merge_kernel(x_ref, o_ref, u_ref, ga0_ref, ga1_ref, gc0_ref, gc1_ref, wo_ref, wp_ref, bp_ref,
                  wout_ref, nw_ref, out_ref):
    for r in range(0, MERGE_TM, MERGE_ROWS):
        rs = slice(r, r + MERGE_ROWS)
        ya = jnp.dot(o_ref[rs, :], wo_ref[...], preferred_element_type=_F32)
        yc = jnp.dot(u_ref[rs, :], wp_ref[...], preferred_element_type=_F32) + bp_ref[...]
        ga = jnp.concatenate([ga0_ref[rs, :], ga1_ref[rs, :]], axis=1).astype(_F32)
        gc = jnp.concatenate([gc0_ref[rs, :], gc1_ref[rs, :]], axis=1).astype(_F32)
        mix = _sigmoid(ga) * ya + _sigmoid(gc) * yc
        out = jnp.dot(mix.astype(_BF16), wout_ref[...], preferred_element_type=_F32)
        ms = jnp.mean(out * out, axis=-1, keepdims=True)
        out_ref[rs, :] = x_ref[rs, :] + out * lax.rsqrt(ms + NORM_EPS) * nw_ref[...]


def _merge(x2, o, u, proj, w_o_bf, w_pw2_bf, b_pw2, w_out_bf, norm_w):
    m = x2.shape[0]

    def const(shape):
        return pl.BlockSpec(shape, lambda i: (0, 0), pipeline_mode=pl.Buffered(1))

    def gate(col):
        return pl.BlockSpec((MERGE_TM, COL_BLK), lambda i: (i, col))

    return pl.pallas_call(
        _merge_kernel,
        out_shape=jax.ShapeDtypeStruct((m, D_MODEL), _F32),
        grid=(m // MERGE_TM,),
        in_specs=[
            pl.BlockSpec((MERGE_TM, D_MODEL), lambda i: (i, 0)),
            pl.BlockSpec((MERGE_TM, D_ATTN), lambda i: (i, 0)),
            pl.BlockSpec((MERGE_TM, D_CONV), lambda i: (i, 0)),
            gate(GA_COL), gate(GA_COL + 1), gate(GC_COL), gate(GC_COL + 1),
            const((D_ATTN, D_MODEL)),
            const((D_CONV, D_MODEL)),
            const((1, D_MODEL)),
            const((D_MODEL, D_MODEL)),
            const((1, D_MODEL)),
        ],
        out_specs=pl.BlockSpec((MERGE_TM, D_MODEL), lambda i: (i, 0)),
        compiler_params=pltpu.CompilerParams(
            dimension_semantics=("arbitrary",),
            vmem_limit_bytes=VMEM_LIMIT_BYTES),
        name="merge",
    )(x2, o, u, proj, proj, proj, proj, w_o_bf, w_pw2_bf, b_pw2, w_out_bf, norm_w)


def _rotary_tables(seq):
    f32 = np.float32
    half = ROT_DIM // 2
    inv_freq = f32(ROPE_THETA) ** (-np.arange(0, ROT_DIM, 2, dtype=f32) / f32(ROT_DIM))
    ang = np.arange(seq, dtype=f32)[:, None] * inv_freq[None, :]
    cos, sin = np.cos(ang).astype(f32), np.sin(ang).astype(f32)
    ones = np.ones((seq, DA_HEAD_DIM - ROT_DIM), f32)
    zeros = np.zeros((seq, DA_HEAD_DIM - ROT_DIM), f32)
    zh = np.zeros((seq, half), f32)
    cos_c = np.concatenate([cos, cos, ones], axis=1)
    sa_c = np.concatenate([-sin, zh, zeros], axis=1)
    sb_c = np.concatenate([zh, sin, zeros], axis=1)
    tile2 = lambda a: np.concatenate([a, a], axis=1)
    return tile2(cos_c), tile2(sa_c), tile2(sb_c)


def kernel(x, norm_pre_w, w_in, lambda_q1, lambda_k1, lambda_q2, lambda_k2, subln_w, w_o_attn, b_glu,
           w_dw, b_dw, ln_g, ln_b, w_pw2, b_pw2, w_out, norm_post_w):
    batch, seq, _ = x.shape
    assert norm_pre_w.shape[0] == 1, "single layer"
    assert seq % PROJ_TM == 0 and seq % ATTN_TQ == 0 and seq % CONV_T == 0
    x2 = x.reshape(batch * seq, D_MODEL)
    cos_t, sa_t, sb_t = _rotary_tables(seq)
    lam_params = jnp.concatenate([lambda_q1, lambda_k1, lambda_q2, lambda_k2], axis=0).astype(_F32)

    conv_cols = slice(W_CONV_BLKS[0] * COL_BLK, (W_CONV_BLKS[-1] + 1) * COL_BLK)
    pc, h = _proj_conv(x2, norm_pre_w, w_in[0][:, conv_cols].astype(_BF16))
    u, w_rest_bf, w_o_bf, w_pw2_bf, w_out_bf = _conv(pc, b_glu, w_dw[0], b_dw, ln_g, ln_b, w_in[0],
                                                     w_o_attn[0], w_pw2[0], w_out[0], batch, seq)
    pr = _proj_rest(h, w_rest_bf, cos_t, sa_t, sb_t, seq)
    o = _attn(pr, lam_params, subln_w, batch, seq)
    out = _merge(x2, o, u, pr, w_o_bf, w_pw2_bf, b_pw2, w_out_bf, norm_post_w)
    return out.reshape(batch, seq, D_MODEL)
```

```python
import math

import jax
import jax.numpy as jnp
import numpy as np
from jax import lax
from jax.experimental import pallas as pl
from jax.experimental.pallas import tpu as pltpu

D_MODEL = 2048
DA_HEADS = 8
DA_HEAD_DIM = 64
HEAD_W = 2 * DA_HEAD_DIM
D_ATTN = DA_HEADS * HEAD_W
ROPE_THETA = 500000.0
ROT_DIM = DA_HEAD_DIM // 4
D_CONV = 1024
CONV_WIDTH = 31
NORM_EPS = 1e-6
SUBLN_EPS = 1e-5
LN_EPS = 1e-5
LAMBDA_INIT = 0.8 - 0.6 * math.exp(-0.3 * 0)
D_IN = D_ATTN * 4 + 2 * D_CONV + D_CONV + 2 * D_MODEL

COL_BLK = 1024
W_CONV_BLKS = (4, 5, 6)
W_REST_BLKS = (0, 1, 2, 3, 7, 8, 9, 10)
GLU_A_COL, GLU_G_COL, ZC_COL = 0, 1, 2
Q_COL, K_COL, V_COL, ZA_COL, GA_COL, GC_COL = 0, 1, 2, 3, 4, 6
HEADS_PER_COL_BLK = COL_BLK // HEAD_W

PROJ_TM = 1024
PROJ_REST_TM = 1024
PROJ_TN = COL_BLK
PROJ_ROWS = 256
Q_SCALE = DA_HEAD_DIM ** -0.5 * math.log2(math.e)
ATTN_KT = 256
ATTN_QCH = 4
ATTN_TQ = ATTN_QCH * ATTN_KT
ATTN_DIAG_W = 256
ATTN_PASS_W = 256
CONV_T = 256
CONV_HALO = 32
CONV_ROWS = 64
MERGE_TM = 256
MERGE_ROWS = 256
LANES = 128
SUBLANES = 8
MASK_VALUE = -1e30

VMEM_LIMIT_BYTES = 56 * 1024 * 1024

_BF16 = jnp.bfloat16
_F32 = jnp.float32


def _sigmoid(x):
    return 0.5 * jnp.tanh(0.5 * x) + 0.5


def _rotary(acc, cos, sa, sb):
    half = ROT_DIM // 2
    up = pltpu.roll(acc, LANES - half, 1)
    down = pltpu.roll(acc, half, 1)
    return acc * cos + up * sa + down * sb


_PROJ_ROW_CHUNKS = [slice(r, r + PROJ_ROWS) for r in range(0, PROJ_TM, PROJ_ROWS)]


def _proj_conv_kernel(x_ref, nw_ref, w_ref, out_ref, h_ref):
    j = pl.program_id(1)

    def normed(rs):
        x = x_ref[rs, :]
        ms = jnp.mean(x * x, axis=-1, keepdims=True)
        return (x * lax.rsqrt(ms + NORM_EPS) * nw_ref[...]).astype(_BF16)

    @pl.when(j == 0)
    def _():
        for rs in _PROJ_ROW_CHUNKS:
            h = normed(rs)
            h_ref[rs, :] = h
            out_ref[rs, :] = jnp.dot(h, w_ref[...], preferred_element_type=_F32).astype(_BF16)

    @pl.when(j > 0)
    def _():
        out_ref[...] = jnp.dot(h_ref[...], w_ref[...], preferred_element_type=_F32).astype(_BF16)


def _proj_rest_kernel(h_ref, w_ref, cos_ref, sa_ref, sb_ref, out_ref):
    j = pl.program_id(1)
    row_chunks = [slice(r, r + PROJ_ROWS) for r in range(0, h_ref.shape[0], PROJ_ROWS)]

    def store_rotated(acc, rs, scale):
        cos, sa, sb = cos_ref[rs, :], sa_ref[rs, :], sb_ref[rs, :]
        for hh in range(HEADS_PER_COL_BLK):
            sl = slice(hh * HEAD_W, (hh + 1) * HEAD_W)
            r = _rotary(acc[:, sl], cos, sa, sb)
            if scale != 1.0:
                r = r * scale
            out_ref[rs, sl] = r.astype(_BF16)

    @pl.when(j == Q_COL)
    def _():
        for rs in row_chunks:
            store_rotated(jnp.dot(h_ref[rs, :], w_ref[...], preferred_element_type=_F32), rs, Q_SCALE)

    @pl.when(j == K_COL)
    def _():
        for rs in row_chunks:
            store_rotated(jnp.dot(h_ref[rs, :], w_ref[...], preferred_element_type=_F32), rs, 1.0)

    @pl.when(j > K_COL)
    def _():
        out_ref[...] = jnp.dot(h_ref[...], w_ref[...], preferred_element_type=_F32).astype(_BF16)


_PROJ_PARAMS = pltpu.CompilerParams(dimension_semantics=("arbitrary", "arbitrary"),
                                    vmem_limit_bytes=VMEM_LIMIT_BYTES)


def _proj_conv(x2, norm_w, w_conv_bf):
    m = x2.shape[0]
    n_col = w_conv_bf.shape[1] // PROJ_TN
    return pl.pallas_call(
        _proj_conv_kernel,
        out_shape=(jax.ShapeDtypeStruct((m, n_col * PROJ_TN), _BF16),
                   jax.ShapeDtypeStruct((m, D_MODEL), _BF16)),
        grid=(m // PROJ_TM, n_col),
        in_specs=[
            pl.BlockSpec((PROJ_TM, D_MODEL), lambda i, j: (i, 0)),
            pl.BlockSpec((1, D_MODEL), lambda i, j: (0, 0)),
            pl.BlockSpec((D_MODEL, PROJ_TN), lambda i, j: (0, j)),
        ],
        out_specs=(pl.BlockSpec((PROJ_TM, PROJ_TN), lambda i, j: (i, j)),
                   pl.BlockSpec((PROJ_TM, D_MODEL), lambda i, j: (i, 0))),
        compiler_params=_PROJ_PARAMS,
        name="proj_conv",
    )(x2, norm_w, w_conv_bf)


def _proj_rest(h, w_rest_bf, cos_t, sa_t, sb_t, seq):
    m = h.shape[0]
    n_col = w_rest_bf.shape[1] // PROJ_TN
    tm = PROJ_REST_TM
    assert seq % tm == 0
    seq_blocks = seq // tm
    tab_spec = pl.BlockSpec((tm, HEAD_W), lambda i, j: (i % seq_blocks, 0))
    return pl.pallas_call(
        _proj_rest_kernel,
        out_shape=jax.ShapeDtypeStruct((m, n_col * PROJ_TN), _BF16),
        grid=(m // tm, n_col),
        in_specs=[
            pl.BlockSpec((tm, D_MODEL), lambda i, j: (i, 0)),
            pl.BlockSpec((D_MODEL, PROJ_TN), lambda i, j: (0, j)),
            tab_spec, tab_spec, tab_spec,
        ],
        out_specs=pl.BlockSpec((tm, PROJ_TN), lambda i, j: (i, j)),
        compiler_params=_PROJ_PARAMS,
        name="proj_rest",
    )(h, w_rest_bf, cos_t, sa_t, sb_t)


def _attn_kernel(q_ref, k_ref, v_ref, z_ref, lam_ref, sw_ref, o_ref,
                 qqt_ref, vt_ref, s_ref, m_ref, l_ref, acc_ref):
    tq, kt = ATTN_TQ, ATTN_KT
    nchunk = 2 * ATTN_QCH
    seq = q_ref.shape[0]

    for j in range(vt_ref.shape[0]):
        vt_ref[j] = v_ref[j * kt:(j + 1) * kt, :].astype(_F32).T.astype(_BF16)

    lam_p = lam_ref[...]
    s1 = jnp.sum(lam_p[0:1] * lam_p[1:2], axis=1, keepdims=True)
    s2 = jnp.sum(lam_p[2:3] * lam_p[3:4], axis=1, keepdims=True)
    lam = jnp.exp(s1) - jnp.exp(s2) + LAMBDA_INIT

    def scores(j, slot, col_ranges):
        k_sub = k_ref[j * kt:(j + 1) * kt, :]
        for c0, w in col_ranges:
            s_ref[slot, :, c0:c0 + w] = jnp.dot(k_sub, qqt_ref[:, c0:c0 + w], preferred_element_type=_F32)

    def softmax_pv(j, slot, col_ranges):
        for c0, w, diag in col_ranges:
            cs = slice(c0, c0 + w)
            rows = kt if diag is None else diag + w
            s = s_ref[slot, 0:rows, cs]
            if diag is not None:
                kv = lax.broadcasted_iota(jnp.int32, (rows, w), 0)
                qq = lax.broadcasted_iota(jnp.int32, (rows, w), 1) + diag
                s = jnp.where(kv <= qq, s, MASK_VALUE)
            m_prev = m_ref[:, cs]
            m_new = jnp.maximum(m_prev, jnp.max(s, axis=0, keepdims=True))
            alpha = jnp.exp2(m_prev - m_new)
            p = jnp.exp2(s - m_new)
            l_ref[:, cs] = alpha * l_ref[:, cs] + jnp.sum(p, axis=0, keepdims=True)
            acc_ref[:, cs] = alpha * acc_ref[:, cs] + jnp.dot(vt_ref[j, :, 0:rows], p.astype(_BF16),
                                                              preferred_element_type=_F32)
            m_ref[:, cs] = m_new

    all_cols = [(0, 2 * tq)]
    full = [(c0, ATTN_PASS_W, None) for c0 in range(0, 2 * tq, ATTN_PASS_W)]

    def diag_ranges(d):
        out = []
        for c in range(nchunk):
            if c % ATTN_QCH > d:
                out.append((c * kt, kt, None))
            elif c % ATTN_QCH == d:
                out.extend((c * kt + off, ATTN_DIAG_W, off) for off in range(0, kt, ATTN_DIAG_W))
        return out

    def query_tile(qi, carry):
        rows = slice(qi * tq, (qi + 1) * tq)

        qt = q_ref[rows, :].astype(_F32).T
        row = lax.broadcasted_iota(jnp.int32, (HEAD_W, tq), 0)
        qqt_ref[:, 0:tq] = jnp.where(row < DA_HEAD_DIM, qt, 0.0).astype(_BF16)
        qqt_ref[:, tq:2 * tq] = jnp.where(row >= DA_HEAD_DIM, qt, 0.0).astype(_BF16)
        m_ref[...] = jnp.full(m_ref.shape, MASK_VALUE, _F32)
        l_ref[...] = jnp.zeros(l_ref.shape, _F32)
        acc_ref[...] = jnp.zeros(acc_ref.shape, _F32)

        scores(0, 0, all_cols)

        def body(i, c):
            j = 2 * i
            for rng in full:
                scores(j + 1, 1, [rng[:2]])
                softmax_pv(j, 0, [rng])
            for rng in full:
                scores(j + 2, 0, [rng[:2]])
                softmax_pv(j + 1, 1, [rng])
            return c

        nfull = ATTN_QCH * qi
        for i in range(nfull // 2):
            body(i, 0)
        for d in range(ATTN_QCH):
            nxt = {c * kt for c in range(nchunk) if c % ATTN_QCH >= d + 1} if d + 1 < ATTN_QCH else set()
            for rng in diag_ranges(d):
                if rng[0] in nxt:
                    scores(nfull + d + 1, (d + 1) % 2, [(rng[0], kt)])
                softmax_pv(nfull + d, d % 2, [rng])

        ot = acc_ref[...] / l_ref[...]
        ot = ot[:, 0:tq] - lam * ot[:, tq:2 * tq]
        ot = ot * lax.rsqrt(jnp.mean(ot * ot, axis=0, keepdims=True) + SUBLN_EPS)
        o = ot.T * sw_ref[...] * (1.0 - LAMBDA_INIT)
        z = z_ref[rows, :].astype(_F32)
        o_ref[rows, :] = (o * (z * _sigmoid(z))).astype(_BF16)
        return carry

    for qi in range(seq // tq):
        query_tile(qi, 0)


def _attn(pr, lam_params, subln_w, batch, seq):
    m = pr.shape[0]
    assert ATTN_TQ == ATTN_QCH * ATTN_KT and seq % ATTN_TQ == 0 and ATTN_QCH % 2 == 0
    hb = HEADS_PER_COL_BLK

    def head_cols(col):
        return pl.BlockSpec((seq, HEAD_W), lambda b, h: (b, col * hb + h))

    return pl.pallas_call(
        _attn_kernel,
        out_shape=jax.ShapeDtypeStruct((m, D_ATTN), _BF16),
        grid=(batch, DA_HEADS),
        in_specs=[
            head_cols(Q_COL), head_cols(K_COL), head_cols(V_COL), head_cols(ZA_COL),
            pl.BlockSpec((4, DA_HEAD_DIM), lambda b, h: (0, 0)),
            pl.BlockSpec((1, HEAD_W), lambda b, h: (0, 0)),
        ],
        out_specs=pl.BlockSpec((seq, HEAD_W), lambda b, h: (b, h)),
        scratch_shapes=[
            pltpu.VMEM((HEAD_W, 2 * ATTN_TQ), _BF16),
            pltpu.VMEM((seq // ATTN_KT, HEAD_W, ATTN_KT), _BF16),
            pltpu.VMEM((2, ATTN_KT, 2 * ATTN_TQ), _F32),
            pltpu.VMEM((1, 2 * ATTN_TQ), _F32),
            pltpu.VMEM((1, 2 * ATTN_TQ), _F32),
            pltpu.VMEM((HEAD_W, 2 * ATTN_TQ), _F32),
        ],
        compiler_params=pltpu.CompilerParams(
            dimension_semantics=("arbitrary", "arbitrary"),
            vmem_limit_bytes=VMEM_LIMIT_BYTES),
        name="attn",
    )(pr, pr, pr, pr, lam_params, subln_w)


def _conv_kernel(a_ref, g_ref, ah_ref, gh_ref, z_ref, bglu_ref, wdw_ref, bdw_ref, lng_ref, lnb_ref, *refs):
    n_rest = len(W_REST_BLKS)
    w_in_refs, (wo_ref, wp_ref, wout_ref) = refs[:n_rest], refs[n_rest:n_rest + 3]
    o_ref, wrest_bf_ref, wo_bf_ref, wp_bf_ref, wout_bf_ref, stage_ref, shift_ref, y_ref = refs[n_rest + 3:]
    for k, w_ref in enumerate(w_in_refs):
        wrest_bf_ref[:, k * COL_BLK:(k + 1) * COL_BLK] = w_ref[...].astype(_BF16)
    wo_bf_ref[...] = wo_ref[...].astype(_BF16)
    wp_bf_ref[...] = wp_ref[...].astype(_BF16)
    wout_bf_ref[...] = wout_ref[...].astype(_BF16)

    t = pl.program_id(1)
    ba = bglu_ref[:, 0:D_CONV]
    bg = bglu_ref[:, D_CONV:2 * D_CONV]

    def glu(a, g):
        return (a.astype(_F32) + ba) * _sigmoid(g.astype(_F32) + bg)

    halo = glu(ah_ref[...], gh_ref[...])
    halo = jnp.where(t > 0, halo, jnp.zeros_like(halo))
    cur = glu(a_ref[...], g_ref[...])
    nblk = D_CONV // LANES
    for c in range(nblk):
        ls = slice(c * LANES, (c + 1) * LANES)
        stage_ref[c, 0:CONV_HALO, :] = halo[:, ls]
        stage_ref[c, CONV_HALO:CONV_HALO + CONV_T, :] = cur[:, ls]

    first = CONV_HALO - (CONV_WIDTH - 1)
    shift_rows = shift_ref.shape[1]
    row_starts = range(0, CONV_T, CONV_ROWS)

    def channel_block(c, carry):
        for s in range(1, SUBLANES):
            shift_ref[s - 1] = stage_ref[c, s:s + shift_rows, :]
        accs = [jnp.broadcast_to(bdw_ref[c], (CONV_ROWS, LANES)) for _ in row_starts]
        for j in range(CONV_WIDTH):
            q, s = divmod(first + j, SUBLANES)
            w_j = jnp.broadcast_to(wdw_ref[c, j:j + 1, :], (CONV_ROWS, LANES))
            for k, r in enumerate(row_starts):
                lo = r + q * SUBLANES
                if s == 0:
                    win = stage_ref[c, lo:lo + CONV_ROWS, :]
                else:
                    win = shift_ref[s - 1, lo:lo + CONV_ROWS, :]
                accs[k] = accs[k] + win * w_j
        for k, r in enumerate(row_starts):
            y_ref[c, r:r + CONV_ROWS, :] = accs[k]
        return carry

    lax.fori_loop(0, nblk, channel_block, 0)

    y = jnp.concatenate([y_ref[c] for c in range(nblk)], axis=1)
    mu = jnp.mean(y, axis=-1, keepdims=True)
    yc = y - mu
    var = jnp.mean(yc * yc, axis=-1, keepdims=True)
    u = yc * lax.rsqrt(var + LN_EPS) * lng_ref[...] + lnb_ref[...]
    z = z_ref[...].astype(_F32)
    o_ref[...] = ((u * _sigmoid(u)) * (z * _sigmoid(z))).astype(_BF16)


def _conv(pc, b_glu, w_dw, b_dw, ln_g, ln_b, w_in, w_o, w_pw2, w_out, batch, seq):
    m = pc.shape[0]
    nt = seq // CONV_T
    steps = batch * nt
    halo_per_tile = CONV_T // CONV_HALO
    nblk = D_CONV // LANES
    w_blk = w_dw.reshape(CONV_WIDTH, nblk, LANES).transpose(1, 0, 2)
    b_blk = b_dw.reshape(nblk, 1, LANES)

    def halo_idx(col):
        return lambda b, t: (jnp.maximum((b * nt + t) * halo_per_tile - 1, 0), col * (COL_BLK // D_CONV))

    def vec(n):
        return pl.BlockSpec((1, n), lambda b, t: (0, 0))

    def row_slab(n_rows, n_cols, col_blk=0):
        rows = n_rows // steps
        assert rows * steps == n_rows and rows % (2 * SUBLANES) == 0
        return pl.BlockSpec((rows, n_cols), lambda b, t: (b * nt + t, col_blk))

    n_rest = len(W_REST_BLKS)
    small = (w_o, w_pw2, w_out)
    return pl.pallas_call(
        _conv_kernel,
        out_shape=(jax.ShapeDtypeStruct((m, D_CONV), _BF16),
                   jax.ShapeDtypeStruct((D_MODEL, n_rest * COL_BLK), _BF16),
                   *[jax.ShapeDtypeStruct(w.shape, _BF16) for w in small]),
        grid=(batch, nt),
        in_specs=[
            pl.BlockSpec((CONV_T, D_CONV), lambda b, t: (b * nt + t, GLU_A_COL)),
            pl.BlockSpec((CONV_T, D_CONV), lambda b, t: (b * nt + t, GLU_G_COL)),
            pl.BlockSpec((CONV_HALO, D_CONV), halo_idx(GLU_A_COL)),
            pl.BlockSpec((CONV_HALO, D_CONV), halo_idx(GLU_G_COL)),
            pl.BlockSpec((CONV_T, D_CONV), lambda b, t: (b * nt + t, ZC_COL)),
            vec(2 * D_CONV),
            pl.BlockSpec((nblk, CONV_WIDTH, LANES), lambda b, t: (0, 0, 0)),
            pl.BlockSpec((nblk, 1, LANES), lambda b, t: (0, 0, 0)),
            vec(D_CONV), vec(D_CONV),
            *[row_slab(D_MODEL, COL_BLK, blk) for blk in W_REST_BLKS],
            *[row_slab(*w.shape) for w in small],
        ],
        out_specs=(pl.BlockSpec((CONV_T, D_CONV), lambda b, t: (b * nt + t, 0)),
                   row_slab(D_MODEL, n_rest * COL_BLK),
                   *[row_slab(*w.shape) for w in small]),
        scratch_shapes=[
            pltpu.VMEM((nblk, CONV_HALO + CONV_T, LANES), _F32),
            pltpu.VMEM((SUBLANES - 1, CONV_HALO + CONV_T - SUBLANES, LANES), _F32),
            pltpu.VMEM((nblk, CONV_T, LANES), _F32),
        ],
        compiler_params=pltpu.CompilerParams(
            dimension_semantics=("arbitrary", "arbitrary"),
            vmem_limit_bytes=VMEM_LIMIT_BYTES),
        name="conv",
    )(pc, pc, pc, pc, pc, b_glu, w_blk, b_blk, ln_g, ln_b, *([w_in] * n_rest), *small)


def _merge_kernel(x_ref, o_ref, u_ref, ga0_ref, ga1_ref, gc0_ref, gc1_ref, wo_ref, wp_ref, bp_ref,
                  wout_ref, nw_ref, out_ref):
    for r in range(0, MERGE_TM, MERGE_ROWS):
        rs = slice(r, r + MERGE_ROWS)
        ya = jnp.dot(o_ref[rs, :], wo_ref[...], preferred_element_type=_F32)
        yc = jnp.dot(u_ref[rs, :], wp_ref[...], preferred_element_type=_F32) + bp_ref[...]
        ga = jnp.concatenate([ga0_ref[rs, :], ga1_ref[rs, :]], axis=1).astype(_F32)
        gc = jnp.concatenate([gc0_ref[rs, :], gc1_ref[rs, :]], axis=1).astype(_F32)
        mix = _sigmoid(ga) * ya + _sigmoid(gc) * yc
        out = jnp.dot(mix.astype(_BF16), wout_ref[...], preferred_element_type=_F32)
        ms = jnp.mean(out * out, axis=-1, keepdims=True)
        out_ref[rs, :] = x_ref[rs, :] + out * lax.rsqrt(ms + NORM_EPS) * nw_ref[...]


def _merge(x2, o, u, proj, w_o_bf, w_pw2_bf, b_pw2, w_out_bf, norm_w):
    m = x2.shape[0]

    def const(shape):
        return pl.BlockSpec(shape, lambda i: (0, 0), pipeline_mode=pl.Buffered(1))

    def gate(col):
        return pl.BlockSpec((MERGE_TM, COL_BLK), lambda i: (i, col))

    return pl.pallas_call(
        _merge_kernel,
        out_shape=jax.ShapeDtypeStruct((m, D_MODEL), _F32),
        grid=(m // MERGE_TM,),
        in_specs=[
            pl.BlockSpec((MERGE_TM, D_MODEL), lambda i: (i, 0)),
            pl.BlockSpec((MERGE_TM, D_ATTN), lambda i: (i, 0)),
            pl.BlockSpec((MERGE_TM, D_CONV), lambda i: (i, 0)),
            gate(GA_COL), gate(GA_COL + 1), gate(GC_COL), gate(GC_COL + 1),
            const((D_ATTN, D_MODEL)),
            const((D_CONV, D_MODEL)),
            const((1, D_MODEL)),
            const((D_MODEL, D_MODEL)),
            const((1, D_MODEL)),
        ],
        out_specs=pl.BlockSpec((MERGE_TM, D_MODEL), lambda i: (i, 0)),
        compiler_params=pltpu.CompilerParams(
            dimension_semantics=("arbitrary",),
            vmem_limit_bytes=VMEM_LIMIT_BYTES),
        name="merge",
    )(x2, o, u, proj, proj, proj, proj, w_o_bf, w_pw2_bf, b_pw2, w_out_bf, norm_w)


def _rotary_tables(seq):
    f32 = np.float32
    half = ROT_DIM // 2
    inv_freq = f32(ROPE_THETA) ** (-np.arange(0, ROT_DIM, 2, dtype=f32) / f32(ROT_DIM))
    ang = np.arange(seq, dtype=f32)[:, None] * inv_freq[None, :]
    cos, sin = np.cos(ang).astype(f32), np.sin(ang).astype(f32)
    ones = np.ones((seq, DA_HEAD_DIM - ROT_DIM), f32)
    zeros = np.zeros((seq, DA_HEAD_DIM - ROT_DIM), f32)
    zh = np.zeros((seq, half), f32)
    cos_c = np.concatenate([cos, cos, ones], axis=1)
    sa_c = np.concatenate([-sin, zh, zeros], axis=1)
    sb_c = np.concatenate([zh, sin, zeros], axis=1)
    tile2 = lambda a: np.concatenate([a, a], axis=1)
    return tile2(cos_c), tile2(sa_c), tile2(sb_c)


def kernel(x, norm_pre_w, w_in, lambda_q1, lambda_k1, lambda_q2, lambda_k2, subln_w, w_o_attn, b_glu,
           w_dw, b_dw, ln_g, ln_b, w_pw2, b_pw2, w_out, norm_post_w):
    batch, seq, _ = x.shape
    assert norm_pre_w.shape[0] == 1, "single layer"
    assert seq % PROJ_TM == 0 and seq % ATTN_TQ == 0 and seq % CONV_T == 0
    x2 = x.reshape(batch * seq, D_MODEL)
    cos_t, sa_t, sb_t = _rotary_tables(seq)
    lam_params = jnp.concatenate([lambda_q1, lambda_k1, lambda_q2, lambda_k2], axis=0).astype(_F32)

    conv_cols = slice(W_CONV_BLKS[0] * COL_BLK, (W_CONV_BLKS[-1] + 1) * COL_BLK)
    pc, h = _proj_conv(x2, norm_pre_w, w_in[0][:, conv_cols].astype(_BF16))
    u, w_rest_bf, w_o_bf, w_pw2_bf, w_out_bf = _conv(pc, b_glu, w_dw[0], b_dw, ln_g, ln_b, w_in[0],
                                                     w_o_attn[0], w_pw2[0], w_out[0], batch, seq)
    pr = _proj_rest(h, w_rest_bf, cos_t, sa_t, sb_t, seq)
    o = _attn(pr, lam_params, subln_w, batch, seq)
    out = _merge(x2, o, u, pr, w_o_bf, w_pw2_bf, b_pw2, w_out_bf, norm_post_w)
    return out.reshape(batch, seq, D_MODEL)
```

```python
import functools
import math

import jax
import jax.numpy as jnp
import numpy as np
from jax import lax
from jax.experimental import pallas as pl
from jax.experimental.pallas import tpu as pltpu

D_MODEL = 2048
DA_HEADS = 8
DA_HEAD_DIM = 64
HEAD_W = 2 * DA_HEAD_DIM
D_ATTN = DA_HEADS * HEAD_W
ROPE_THETA = 500000.0
ROT_DIM = DA_HEAD_DIM // 4
D_CONV = 1024
CONV_WIDTH = 31
NORM_EPS = 1e-6
SUBLN_EPS = 1e-5
LN_EPS = 1e-5
LAMBDA_INIT = 0.8 - 0.6 * math.exp(-0.3 * 0)
D_IN = D_ATTN * 4 + 2 * D_CONV + D_CONV + 2 * D_MODEL

COL_BLK = 1024
W_CONV_BLKS = (4, 5, 6)
W_REST_BLKS = (0, 1, 2, 3, 7, 8, 9, 10)
GLU_A_COL, GLU_G_COL, ZC_COL = 0, 1, 2
Q_COL, K_COL, V_COL, ZA_COL, GA_COL, GC_COL = 0, 1, 2, 3, 4, 6
HEADS_PER_COL_BLK = COL_BLK // HEAD_W

PROJ_TM = 1024
PROJ_REST_TM = 1024
PROJ_TN = COL_BLK
PROJ_ROWS = 256
W_SLAB_STEPS = 16
Q_SCALE = DA_HEAD_DIM ** -0.5 * math.log2(math.e)
ATTN_KT = 256
ATTN_QCH = 8
ATTN_TQ = ATTN_QCH * ATTN_KT
ATTN_DIAG_W = 256
ATTN_PASS_W = 256
CONV_T = 256
CONV_HALO = 32
CONV_ROWS = 64
MERGE_TM = 256
TAIL_CHUNKS = 4
MERGE_ROWS = 256
LANES = 128
SUBLANES = 8
MASK_VALUE = -1e30

VMEM_LIMIT_BYTES = 56 * 1024 * 1024

_BF16 = jnp.bfloat16
_F32 = jnp.float32


def _sigmoid(x):
    return 0.5 * jnp.tanh(0.5 * x) + 0.5


def _rotary(acc, cos, sa, sb):
    half = ROT_DIM // 2
    up = pltpu.roll(acc, LANES - half, 1)
    down = pltpu.roll(acc, half, 1)
    return acc * cos + up * sa + down * sb


_PROJ_ROW_CHUNKS = [slice(r, r + PROJ_ROWS) for r in range(0, PROJ_TM, PROJ_ROWS)]


def _proj_conv_kernel(x_ref, nw_ref, w_ref, *refs):
    n_rest = len(W_REST_BLKS)
    w_in_refs, (out_ref, h_ref, wrest_bf_ref) = refs[:n_rest], refs[n_rest:]
    j = pl.program_id(1)
    step = pl.program_id(0) * pl.num_programs(1) + j

    @pl.when(step < W_SLAB_STEPS)
    def _():
        for k, wk_ref in enumerate(w_in_refs):
            wrest_bf_ref[:, k * COL_BLK:(k + 1) * COL_BLK] = wk_ref[...].astype(_BF16)

    def normed(rs):
        x = x_ref[rs, :]
        ms = jnp.mean(x * x, axis=-1, keepdims=True)
        return (x * lax.rsqrt(ms + NORM_EPS) * nw_ref[...]).astype(_BF16)

    @pl.when(j == 0)
    def _():
        for rs in _PROJ_ROW_CHUNKS:
            h = normed(rs)
            h_ref[rs, :] = h
            out_ref[rs, :] = jnp.dot(h, w_ref[...], preferred_element_type=_F32).astype(_BF16)

    @pl.when(j > 0)
    def _():
        out_ref[...] = jnp.dot(h_ref[...], w_ref[...], preferred_element_type=_F32).astype(_BF16)


def _proj_rest_kernel(h_ref, w_ref, cos_ref, sa_ref, sb_ref, out_ref):
    j = pl.program_id(1)
    row_chunks = [slice(r, r + PROJ_ROWS) for r in range(0, h_ref.shape[0], PROJ_ROWS)]

    def store_rotated(acc, rs, scale):
        cos, sa, sb = cos_ref[rs, :], sa_ref[rs, :], sb_ref[rs, :]
        for hh in range(HEADS_PER_COL_BLK):
            sl = slice(hh * HEAD_W, (hh + 1) * HEAD_W)
            r = _rotary(acc[:, sl], cos, sa, sb)
            if scale != 1.0:
                r = r * scale
            out_ref[rs, sl] = r.astype(_BF16)

    @pl.when(j == Q_COL)
    def _():
        for rs in row_chunks:
            store_rotated(jnp.dot(h_ref[rs, :], w_ref[...], preferred_element_type=_F32), rs, Q_SCALE)

    @pl.when(j == K_COL)
    def _():
        for rs in row_chunks:
            store_rotated(jnp.dot(h_ref[rs, :], w_ref[...], preferred_element_type=_F32), rs, 1.0)

    @pl.when(j > K_COL)
    def _():
        out_ref[...] = jnp.dot(h_ref[...], w_ref[...], preferred_element_type=_F32).astype(_BF16)


_PROJ_PARAMS = pltpu.CompilerParams(dimension_semantics=("arbitrary", "arbitrary"),
                                    vmem_limit_bytes=VMEM_LIMIT_BYTES)


def _proj_conv(x2, norm_w, w_conv_bf, w_in):
    m = x2.shape[0]
    n_col = w_conv_bf.shape[1] // PROJ_TN
    n_rest = len(W_REST_BLKS)
    slab_rows = D_MODEL // W_SLAB_STEPS
    assert (m // PROJ_TM) * n_col >= W_SLAB_STEPS and slab_rows % (2 * SUBLANES) == 0

    def slab(n_cols, col_blk=0):
        return pl.BlockSpec((slab_rows, n_cols),
                            lambda i, j: (jnp.minimum(i * n_col + j, W_SLAB_STEPS - 1), col_blk))

    return pl.pallas_call(
        _proj_conv_kernel,
        out_shape=(jax.ShapeDtypeStruct((m, n_col * PROJ_TN), _BF16),
                   jax.ShapeDtypeStruct((m, D_MODEL), _BF16),
                   jax.ShapeDtypeStruct((D_MODEL, n_rest * COL_BLK), _BF16)),
        grid=(m // PROJ_TM, n_col),
        in_specs=[
            pl.BlockSpec((PROJ_TM, D_MODEL), lambda i, j: (i, 0)),
            pl.BlockSpec((1, D_MODEL), lambda i, j: (0, 0)),
            pl.BlockSpec((D_MODEL, PROJ_TN), lambda i, j: (0, j)),
            *[slab(COL_BLK, blk) for blk in W_REST_BLKS],
        ],
        out_specs=(pl.BlockSpec((PROJ_TM, PROJ_TN), lambda i, j: (i, j)),
                   pl.BlockSpec((PROJ_TM, D_MODEL), lambda i, j: (i, 0)),
                   slab(n_rest * COL_BLK)),
        compiler_params=_PROJ_PARAMS,
        name="proj_conv",
    )(x2, norm_w, w_conv_bf, *([w_in] * n_rest))


def _proj_rest(h, w_rest_bf, cos_t, sa_t, sb_t, seq):
    m = h.shape[0]
    n_col = w_rest_bf.shape[1] // PROJ_TN
    tm = PROJ_REST_TM
    assert seq % tm == 0
    seq_blocks = seq // tm
    tab_spec = pl.BlockSpec((tm, HEAD_W), lambda i, j: (i % seq_blocks, 0))
    return pl.pallas_call(
        _proj_rest_kernel,
        out_shape=jax.ShapeDtypeStruct((m, n_col * PROJ_TN), _BF16),
        grid=(m // tm, n_col),
        in_specs=[
            pl.BlockSpec((tm, D_MODEL), lambda i, j: (i, 0)),
            pl.BlockSpec((D_MODEL, PROJ_TN), lambda i, j: (0, j)),
            tab_spec, tab_spec, tab_spec,
        ],
        out_specs=pl.BlockSpec((tm, PROJ_TN), lambda i, j: (i, j)),
        compiler_params=_PROJ_PARAMS,
        name="proj_rest",
    )(h, w_rest_bf, cos_t, sa_t, sb_t)


def _attn_kernel(q_ref, k_ref, v_ref, z_ref, lam_ref, sw_ref, wo_ref, wp_ref, wout_ref,
                 o_ref, wo_bf_ref, wp_bf_ref, wout_bf_ref,
                 qqt_ref, vt_ref, s_ref, m_ref, l_ref, acc_ref):
    wo_bf_ref[...] = wo_ref[...].astype(_BF16)
    wp_bf_ref[...] = wp_ref[...].astype(_BF16)
    wout_bf_ref[...] = wout_ref[...].astype(_BF16)
    tq, kt = ATTN_TQ, ATTN_KT
    nchunk = 2 * ATTN_QCH
    seq = q_ref.shape[0]

    for j in range(vt_ref.shape[0]):
        vt_ref[j] = v_ref[j * kt:(j + 1) * kt, :].astype(_F32).T.astype(_BF16)

    lam_p = lam_ref[...]
    s1 = jnp.sum(lam_p[0:1] * lam_p[1:2], axis=1, keepdims=True)
    s2 = jnp.sum(lam_p[2:3] * lam_p[3:4], axis=1, keepdims=True)
    lam = jnp.exp(s1) - jnp.exp(s2) + LAMBDA_INIT

    def scores(j, slot, col_ranges):
        k_sub = k_ref[j * kt:(j + 1) * kt, :]
        for c0, w in col_ranges:
            s_ref[slot, :, c0:c0 + w] = jnp.dot(k_sub, qqt_ref[:, c0:c0 + w], preferred_element_type=_F32)

    def softmax_pv(j, slot, col_ranges):
        for c0, w, diag in col_ranges:
            cs = slice(c0, c0 + w)
            rows = kt if diag is None else diag + w
            s = s_ref[slot, 0:rows, cs]
            if diag is not None:
                kv = lax.broadcasted_iota(jnp.int32, (rows, w), 0)
                qq = lax.broadcasted_iota(jnp.int32, (rows, w), 1) + diag
                s = jnp.where(kv <= qq, s, MASK_VALUE)
            m_prev = m_ref[:, cs]
            m_new = jnp.maximum(m_prev, jnp.max(s, axis=0, keepdims=True))
            alpha = jnp.exp2(m_prev - m_new)
            p = jnp.exp2(s - m_new)
            l_ref[:, cs] = alpha * l_ref[:, cs] + jnp.sum(p, axis=0, keepdims=True)
            acc_ref[:, cs] = alpha * acc_ref[:, cs] + jnp.dot(vt_ref[j, :, 0:rows], p.astype(_BF16),
                                                              preferred_element_type=_F32)
            m_ref[:, cs] = m_new

    all_cols = [(0, 2 * tq)]
    full = [(c0, ATTN_PASS_W, None) for c0 in range(0, 2 * tq, ATTN_PASS_W)]

    def diag_ranges(d):
        out = []
        for c in range(nchunk):
            if c % ATTN_QCH > d:
                out.append((c * kt, kt, None))
            elif c % ATTN_QCH == d:
                out.extend((c * kt + off, ATTN_DIAG_W, off) for off in range(0, kt, ATTN_DIAG_W))
        return out

    def query_tile(qi, carry):
        rows = slice(qi * tq, (qi + 1) * tq)

        qt = q_ref[rows, :].astype(_F32).T
        row = lax.broadcasted_iota(jnp.int32, (HEAD_W, tq), 0)
        qqt_ref[:, 0:tq] = jnp.where(row < DA_HEAD_DIM, qt, 0.0).astype(_BF16)
        qqt_ref[:, tq:2 * tq] = jnp.where(row >= DA_HEAD_DIM, qt, 0.0).astype(_BF16)
        m_ref[...] = jnp.full(m_ref.shape, MASK_VALUE, _F32)
        l_ref[...] = jnp.zeros(l_ref.shape, _F32)
        acc_ref[...] = jnp.zeros(acc_ref.shape, _F32)

        scores(0, 0, all_cols)

        def body(i, c):
            j = 2 * i
            for rng in full:
                scores(j + 1, 1, [rng[:2]])
                softmax_pv(j, 0, [rng])
            for rng in full:
                scores(j + 2, 0, [rng[:2]])
                softmax_pv(j + 1, 1, [rng])
            return c

        nfull = ATTN_QCH * qi
        for i in range(nfull // 2):
            body(i, 0)
        for d in range(ATTN_QCH):
            nxt = {c * kt for c in range(nchunk) if c % ATTN_QCH >= d + 1} if d + 1 < ATTN_QCH else set()
            for rng in diag_ranges(d):
                if rng[0] in nxt:
                    scores(nfull + d + 1, (d + 1) % 2, [(rng[0], kt)])
                softmax_pv(nfull + d, d % 2, [rng])

        ot = acc_ref[...] / l_ref[...]
        ot = ot[:, 0:tq] - lam * ot[:, tq:2 * tq]
        ot = ot * lax.rsqrt(jnp.mean(ot * ot, axis=0, keepdims=True) + SUBLN_EPS)
        o = ot.T * sw_ref[...] * (1.0 - LAMBDA_INIT)
        z = z_ref[rows, :].astype(_F32)
        o_ref[rows, :] = (o * (z * _sigmoid(z))).astype(_BF16)
        return carry

    for qi in range(seq // tq):
        query_tile(qi, 0)


def _attn(pr, lam_params, subln_w, w_o, w_pw2, w_out, batch, seq):
    m = pr.shape[0]
    assert ATTN_TQ == ATTN_QCH * ATTN_KT and seq % ATTN_TQ == 0 and ATTN_QCH % 2 == 0
    hb = HEADS_PER_COL_BLK
    steps = batch * DA_HEADS

    def head_cols(col):
        return pl.BlockSpec((seq, HEAD_W), lambda b, h: (b, col * hb + h))

    def row_slab(w):
        rows = w.shape[0] // steps
        assert rows * steps == w.shape[0] and rows % (2 * SUBLANES) == 0
        return pl.BlockSpec((rows, w.shape[1]), lambda b, h: (b * DA_HEADS + h, 0))

    weights = (w_o, w_pw2, w_out)
    return pl.pallas_call(
        _attn_kernel,
        out_shape=(jax.ShapeDtypeStruct((m, D_ATTN), _BF16),
                   *[jax.ShapeDtypeStruct(w.shape, _BF16) for w in weights]),
        grid=(batch, DA_HEADS),
        in_specs=[
            head_cols(Q_COL), head_cols(K_COL), head_cols(V_COL), head_cols(ZA_COL),
            pl.BlockSpec((4, DA_HEAD_DIM), lambda b, h: (0, 0)),
            pl.BlockSpec((1, HEAD_W), lambda b, h: (0, 0)),
            *[row_slab(w) for w in weights],
        ],
        out_specs=(pl.BlockSpec((seq, HEAD_W), lambda b, h: (b, h)), *[row_slab(w) for w in weights]),
        scratch_shapes=[
            pltpu.VMEM((HEAD_W, 2 * ATTN_TQ), _BF16),
            pltpu.VMEM((seq // ATTN_KT, HEAD_W, ATTN_KT), _BF16),
            pltpu.VMEM((2, ATTN_KT, 2 * ATTN_TQ), _F32),
            pltpu.VMEM((1, 2 * ATTN_TQ), _F32),
            pltpu.VMEM((1, 2 * ATTN_TQ), _F32),
            pltpu.VMEM((HEAD_W, 2 * ATTN_TQ), _F32),
        ],
        compiler_params=pltpu.CompilerParams(
            dimension_semantics=("arbitrary", "arbitrary"),
            vmem_limit_bytes=VMEM_LIMIT_BYTES),
        name="attn",
    )(pr, pr, pr, pr, lam_params, subln_w, *weights)


def _conv_kernel(a_ref, g_ref, ah_ref, gh_ref, z_ref, bglu_ref, wdw_ref, bdw_ref, lng_ref, lnb_ref, *refs):
    n_rest = len(W_REST_BLKS)
    w_in_refs, (wo_ref, wp_ref, wout_ref) = refs[:n_rest], refs[n_rest:n_rest + 3]
    o_ref, wrest_bf_ref, wo_bf_ref, wp_bf_ref, wout_bf_ref, stage_ref, shift_ref, y_ref = refs[n_rest + 3:]
    for k, w_ref in enumerate(w_in_refs):
        wrest_bf_ref[:, k * COL_BLK:(k + 1) * COL_BLK] = w_ref[...].astype(_BF16)
    wo_bf_ref[...] = wo_ref[...].astype(_BF16)
    wp_bf_ref[...] = wp_ref[...].astype(_BF16)
    wout_bf_ref[...] = wout_ref[...].astype(_BF16)

    t = pl.program_id(1)
    ba = bglu_ref[:, 0:D_CONV]
    bg = bglu_ref[:, D_CONV:2 * D_CONV]

    def glu(a, g):
        return (a.astype(_F32) + ba) * _sigmoid(g.astype(_F32) + bg)

    halo = glu(ah_ref[...], gh_ref[...])
    halo = jnp.where(t > 0, halo, jnp.zeros_like(halo))
    cur = glu(a_ref[...], g_ref[...])
    nblk = D_CONV // LANES
    for c in range(nblk):
        ls = slice(c * LANES, (c + 1) * LANES)
        stage_ref[c, 0:CONV_HALO, :] = halo[:, ls]
        stage_ref[c, CONV_HALO:CONV_HALO + CONV_T, :] = cur[:, ls]

    first = CONV_HALO - (CONV_WIDTH - 1)
    shift_rows = shift_ref.shape[1]
    row_starts = range(0, CONV_T, CONV_ROWS)

    def channel_block(c, carry):
        for s in range(1, SUBLANES):
            shift_ref[s - 1] = stage_ref[c, s:s + shift_rows, :]
        accs = [jnp.broadcast_to(bdw_ref[c], (CONV_ROWS, LANES)) for _ in row_starts]
        for j in range(CONV_WIDTH):
            q, s = divmod(first + j, SUBLANES)
            w_j = jnp.broadcast_to(wdw_ref[c, j:j + 1, :], (CONV_ROWS, LANES))
            for k, r in enumerate(row_starts):
                lo = r + q * SUBLANES
                if s == 0:
                    win = stage_ref[c, lo:lo + CONV_ROWS, :]
                else:
                    win = shift_ref[s - 1, lo:lo + CONV_ROWS, :]
                accs[k] = accs[k] + win * w_j
        for k, r in enumerate(row_starts):
            y_ref[c, r:r + CONV_ROWS, :] = accs[k]
        return carry

    lax.fori_loop(0, nblk, channel_block, 0)

    y = jnp.concatenate([y_ref[c] for c in range(nblk)], axis=1)
    mu = jnp.mean(y, axis=-1, keepdims=True)
    yc = y - mu
    var = jnp.mean(yc * yc, axis=-1, keepdims=True)
    u = yc * lax.rsqrt(var + LN_EPS) * lng_ref[...] + lnb_ref[...]
    z = z_ref[...].astype(_F32)
    o_ref[...] = ((u * _sigmoid(u)) * (z * _sigmoid(z))).astype(_BF16)


def _conv(pc, b_glu, w_dw, b_dw, ln_g, ln_b, w_in, w_o, w_pw2, w_out, batch, seq):
    m = pc.shape[0]
    nt = seq // CONV_T
    steps = batch * nt
    halo_per_tile = CONV_T // CONV_HALO
    nblk = D_CONV // LANES
    w_blk = w_dw.reshape(CONV_WIDTH, nblk, LANES).transpose(1, 0, 2)
    b_blk = b_dw.reshape(nblk, 1, LANES)

    def halo_idx(col):
        return lambda b, t: (jnp.maximum((b * nt + t) * halo_per_tile - 1, 0), col * (COL_BLK // D_CONV))

    def vec(n):
        return pl.BlockSpec((1, n), lambda b, t: (0, 0))

    def row_slab(n_rows, n_cols, col_blk=0):
        rows = n_rows // steps
        assert rows * steps == n_rows and rows % (2 * SUBLANES) == 0
        return pl.BlockSpec((rows, n_cols), lambda b, t: (b * nt + t, col_blk))

    n_rest = len(W_REST_BLKS)
    small = (w_o, w_pw2, w_out)
    return pl.pallas_call(
        _conv_kernel,
        out_shape=(jax.ShapeDtypeStruct((m, D_CONV), _BF16),
                   jax.ShapeDtypeStruct((D_MODEL, n_rest * COL_BLK), _BF16),
                   *[jax.ShapeDtypeStruct(w.shape, _BF16) for w in small]),
        grid=(batch, nt),
        in_specs=[
            pl.BlockSpec((CONV_T, D_CONV), lambda b, t: (b * nt + t, GLU_A_COL)),
            pl.BlockSpec((CONV_T, D_CONV), lambda b, t: (b * nt + t, GLU_G_COL)),
            pl.BlockSpec((CONV_HALO, D_CONV), halo_idx(GLU_A_COL)),
            pl.BlockSpec((CONV_HALO, D_CONV), halo_idx(GLU_G_COL)),
            pl.BlockSpec((CONV_T, D_CONV), lambda b, t: (b * nt + t, ZC_COL)),
            vec(2 * D_CONV),
            pl.BlockSpec((nblk, CONV_WIDTH, LANES), lambda b, t: (0, 0, 0)),
            pl.BlockSpec((nblk, 1, LANES), lambda b, t: (0, 0, 0)),
            vec(D_CONV), vec(D_CONV),
            *[row_slab(D_MODEL, COL_BLK, blk) for blk in W_REST_BLKS],
            *[row_slab(*w.shape) for w in small],
        ],
        out_specs=(pl.BlockSpec((CONV_T, D_CONV), lambda b, t: (b * nt + t, 0)),
                   row_slab(D_MODEL, n_rest * COL_BLK),
                   *[row_slab(*w.shape) for w in small]),
        scratch_shapes=[
            pltpu.VMEM((nblk, CONV_HALO + CONV_T, LANES), _F32),
            pltpu.VMEM((SUBLANES - 1, CONV_HALO + CONV_T - SUBLANES, LANES), _F32),
            pltpu.VMEM((nblk, CONV_T, LANES), _F32),
        ],
        compiler_params=pltpu.CompilerParams(
            dimension_semantics=("arbitrary", "arbitrary"),
            vmem_limit_bytes=VMEM_LIMIT_BYTES),
        name="conv",
    )(pc, pc, pc, pc, pc, b_glu, w_blk, b_blk, ln_g, ln_b, *([w_in] * n_rest), *small)


def _merge_kernel(x_ref, o_ref, u_ref, ga0_ref, ga1_ref, gc0_ref, gc1_ref, wo_ref, wp_ref, bp_ref,
                  wout_ref, nw_ref, out_ref):
    for r in range(0, MERGE_TM, MERGE_ROWS):
        rs = slice(r, r + MERGE_ROWS)
        ya = jnp.dot(o_ref[rs, :], wo_ref[...], preferred_element_type=_F32)
        yc = jnp.dot(u_ref[rs, :], wp_ref[...], preferred_element_type=_F32) + bp_ref[...]
        ga = jnp.concatenate([ga0_ref[rs, :], ga1_ref[rs, :]], axis=1).astype(_F32)
        gc = jnp.concatenate([gc0_ref[rs, :], gc1_ref[rs, :]], axis=1).astype(_F32)
        mix = _sigmoid(ga) * ya + _sigmoid(gc) * yc
        out = jnp.dot(mix.astype(_BF16), wout_ref[...], preferred_element_type=_F32)
        ms = jnp.mean(out * out, axis=-1, keepdims=True)
        out_ref[rs, :] = x_ref[rs, :] + out * lax.rsqrt(ms + NORM_EPS) * nw_ref[...]


def _merge(x2, o, u, proj, w_o_bf, w_pw2_bf, b_pw2, w_out_bf, norm_w):
    m = x2.shape[0]

    def const(shape):
        return pl.BlockSpec(shape, lambda i: (0, 0), pipeline_mode=pl.Buffered(1))

    def gate(col):
        return pl.BlockSpec((MERGE_TM, COL_BLK), lambda i: (i, col))

    return pl.pallas_call(
        _merge_kernel,
        out_shape=jax.ShapeDtypeStruct((m, D_MODEL), _F32),
        grid=(m // MERGE_TM,),
        in_specs=[
            pl.BlockSpec((MERGE_TM, D_MODEL), lambda i: (i, 0)),
            pl.BlockSpec((MERGE_TM, D_ATTN), lambda i: (i, 0)),
            pl.BlockSpec((MERGE_TM, D_CONV), lambda i: (i, 0)),
            gate(GA_COL), gate(GA_COL + 1), gate(GC_COL), gate(GC_COL + 1),
            const((D_ATTN, D_MODEL)),
            const((D_CONV, D_MODEL)),
            const((1, D_MODEL)),
            const((D_MODEL, D_MODEL)),
            const((1, D_MODEL)),
        ],
        out_specs=pl.BlockSpec((MERGE_TM, D_MODEL), lambda i: (i, 0)),
        compiler_params=pltpu.CompilerParams(
            dimension_semantics=("arbitrary",),
            vmem_limit_bytes=VMEM_LIMIT_BYTES),
        name="merge",
    )(x2, o, u, proj, proj, proj, proj, w_o_bf, w_pw2_bf, b_pw2, w_out_bf, norm_w)


def _tail_kernel(a_ref, g_ref, ah_ref, gh_ref, z_ref, bglu_ref, wdw_ref, bdw_ref, lng_ref, lnb_ref,
                 o_ref, ga0_ref, ga1_ref, gc0_ref, gc1_ref, wo_ref, wp_ref, bp_ref, wout_ref, x_ref, nw_ref,
                 out_ref, stage_ref, shift_ref, y_ref, u_ref, pre_ref, mix_ref, raw_ref, *, tiles_per_seq):
    i = pl.program_id(0)
    n_tiles = pl.num_programs(0) - 1
    t = lax.rem(jnp.minimum(i, n_tiles - 1), tiles_per_seq)
    nblk = D_CONV // LANES
    cw = D_MODEL // TAIL_CHUNKS

    @pl.when(i == 0)
    def _():
        u_ref[...] = jnp.zeros(u_ref.shape, u_ref.dtype)

    def proj_chunk(p):
        which, q = divmod(p, TAIL_CHUNKS)
        cs = slice(q * cw, (q + 1) * cw)
        if which == 0:
            pre_ref[0, :, cs] = jnp.dot(o_ref[...], wo_ref[:, cs], preferred_element_type=_F32)
        else:
            pre_ref[1, :, cs] = jnp.dot(u_ref[...], wp_ref[:, cs], preferred_element_type=_F32) + bp_ref[:, cs]

    def gate():
        ga = _sigmoid(jnp.concatenate([ga0_ref[...], ga1_ref[...]], axis=1).astype(_F32))
        gc = _sigmoid(jnp.concatenate([gc0_ref[...], gc1_ref[...]], axis=1).astype(_F32))
        mix_ref[...] = (ga * pre_ref[0] + gc * pre_ref[1]).astype(_BF16)

    def out_chunk(q):
        cs = slice(q * cw, (q + 1) * cw)
        raw_ref[:, cs] = jnp.dot(mix_ref[...], wout_ref[:, cs], preferred_element_type=_F32)

    def finish():
        out = raw_ref[...]
        ms = jnp.mean(out * out, axis=-1, keepdims=True)
        out_ref[...] = x_ref[...] + out * lax.rsqrt(ms + NORM_EPS) * nw_ref[...]

    def glu_stage():
        ba = bglu_ref[:, 0:D_CONV]
        bg = bglu_ref[:, D_CONV:2 * D_CONV]

        def glu(a, g):
            return (a.astype(_F32) + ba) * _sigmoid(g.astype(_F32) + bg)

        halo = glu(ah_ref[...], gh_ref[...])
        halo = jnp.where(t > 0, halo, jnp.zeros_like(halo))
        cur = glu(a_ref[...], g_ref[...])
        for c in range(nblk):
            ls = slice(c * LANES, (c + 1) * LANES)
            stage_ref[c, 0:CONV_HALO, :] = halo[:, ls]
            stage_ref[c, CONV_HALO:CONV_HALO + CONV_T, :] = cur[:, ls]

    first = CONV_HALO - (CONV_WIDTH - 1)
    shift_rows = shift_ref.shape[1]
    row_starts = range(0, CONV_T, CONV_ROWS)

    def conv_block(c):
        for s in range(1, SUBLANES):
            shift_ref[s - 1] = stage_ref[c, s:s + shift_rows, :]
        for r in row_starts:
            acc = jnp.broadcast_to(bdw_ref[c], (CONV_ROWS, LANES))
            for j in range(CONV_WIDTH):
                q, s = divmod(first + j, SUBLANES)
                lo = r + q * SUBLANES
                if s == 0:
                    win = stage_ref[c, lo:lo + CONV_ROWS, :]
                else:
                    win = shift_ref[s - 1, lo:lo + CONV_ROWS, :]
                acc = acc + win * wdw_ref[c, j:j + 1, :]
            y_ref[c, r:r + CONV_ROWS, :] = acc

    def layer_norm_gate():
        y = jnp.concatenate([y_ref[c] for c in range(nblk)], axis=1)
        mu = jnp.mean(y, axis=-1, keepdims=True)
        yd = y - mu
        var = jnp.mean(yd * yd, axis=-1, keepdims=True)
        u = yd * lax.rsqrt(var + LN_EPS) * lng_ref[...] + lnb_ref[...]
        z = z_ref[...].astype(_F32)
        u_ref[...] = ((u * _sigmoid(u)) * (z * _sigmoid(z))).astype(_BF16)

    n_proj = 2 * TAIL_CHUNKS
    proj_chunk(0)
    glu_stage()
    for c in range(nblk // 2):
        proj_chunk(2 * c + 1)
        conv_block(c)
        if 2 * c + 2 < n_proj:
            proj_chunk(2 * c + 2)
    gate()
    for c in range(nblk // 2, nblk):
        out_chunk(c - nblk // 2)
        conv_block(c)
    layer_norm_gate()
    finish()


def _tail(x2, o, pc, pr, b_glu, w_dw, b_dw, ln_g, ln_b, w_o_bf, w_pw2_bf, b_pw2, w_out_bf, norm_w, seq):
    m = x2.shape[0]
    n_tiles = m // CONV_T
    halo_per_tile = CONV_T // CONV_HALO
    nblk = D_CONV // LANES
    assert nblk == 2 * TAIL_CHUNKS
    w_blk = w_dw.reshape(CONV_WIDTH, nblk, LANES).transpose(1, 0, 2)
    b_blk = b_dw.reshape(nblk, 1, LANES)

    conv_tile = lambda i: jnp.minimum(i, n_tiles - 1)
    merge_tile = lambda i: jnp.maximum(i - 1, 0)

    def cols(tile, col):
        return pl.BlockSpec((CONV_T, COL_BLK), lambda i: (tile(i), col))

    def halo(col):
        return pl.BlockSpec((CONV_HALO, COL_BLK),
                            lambda i: (jnp.maximum(conv_tile(i) * halo_per_tile - 1, 0), col))

    def whole(shape, buffered_once=False):
        zeros = (0,) * len(shape)
        if buffered_once:
            return pl.BlockSpec(shape, lambda i: zeros, pipeline_mode=pl.Buffered(1))
        return pl.BlockSpec(shape, lambda i: zeros)

    return pl.pallas_call(
        functools.partial(_tail_kernel, tiles_per_seq=seq // CONV_T),
        out_shape=jax.ShapeDtypeStruct((m, D_MODEL), _F32),
        grid=(n_tiles + 1,),
        in_specs=[
            cols(conv_tile, GLU_A_COL), cols(conv_tile, GLU_G_COL),
            halo(GLU_A_COL), halo(GLU_G_COL),
            cols(conv_tile, ZC_COL),
            whole((1, 2 * D_CONV)),
            whole((nblk, CONV_WIDTH, LANES)), whole((nblk, 1, LANES)),
            whole((1, D_CONV)), whole((1, D_CONV)),
            pl.BlockSpec((CONV_T, D_ATTN), lambda i: (merge_tile(i), 0)),
            cols(merge_tile, GA_COL), cols(merge_tile, GA_COL + 1),
            cols(merge_tile, GC_COL), cols(merge_tile, GC_COL + 1),
            whole((D_ATTN, D_MODEL), True), whole((D_CONV, D_MODEL), True), whole((1, D_MODEL)),
            whole((D_MODEL, D_MODEL), True),
            pl.BlockSpec((CONV_T, D_MODEL), lambda i: (merge_tile(i), 0)),
            whole((1, D_MODEL)),
        ],
        out_specs=pl.BlockSpec((CONV_T, D_MODEL), lambda i: (merge_tile(i), 0)),
        scratch_shapes=[
            pltpu.VMEM((nblk, CONV_HALO + CONV_T, LANES), _F32),
            pltpu.VMEM((SUBLANES - 1, CONV_HALO + CONV_T - SUBLANES, LANES), _F32),
            pltpu.VMEM((nblk, CONV_T, LANES), _F32),
            pltpu.VMEM((CONV_T, D_CONV), _BF16),
            pltpu.VMEM((2, CONV_T, D_MODEL), _F32),
            pltpu.VMEM((CONV_T, D_MODEL), _BF16),
            pltpu.VMEM((CONV_T, D_MODEL), _F32),
        ],
        compiler_params=pltpu.CompilerParams(
            dimension_semantics=("arbitrary",),
            vmem_limit_bytes=VMEM_LIMIT_BYTES),
        name="tail",
    )(pc, pc, pc, pc, pc, b_glu, w_blk, b_blk, ln_g, ln_b,
      o, pr, pr, pr, pr, w_o_bf, w_pw2_bf, b_pw2, w_out_bf, x2, norm_w)


def _rotary_tables(seq):
    f32 = np.float32
    half = ROT_DIM // 2
    inv_freq = f32(ROPE_THETA) ** (-np.arange(0, ROT_DIM, 2, dtype=f32) / f32(ROT_DIM))
    ang = np.arange(seq, dtype=f32)[:, None] * inv_freq[None, :]
    cos, sin = np.cos(ang).astype(f32), np.sin(ang).astype(f32)
    ones = np.ones((seq, DA_HEAD_DIM - ROT_DIM), f32)
    zeros = np.zeros((seq, DA_HEAD_DIM - ROT_DIM), f32)
    zh = np.zeros((seq, half), f32)
    cos_c = np.concatenate([cos, cos, ones], axis=1)
    sa_c = np.concatenate([-sin, zh, zeros], axis=1)
    sb_c = np.concatenate([zh, sin, zeros], axis=1)
    tile2 = lambda a: np.concatenate([a, a], axis=1)
    return tile2(cos_c), tile2(sa_c), tile2(sb_c)


def kernel(x, norm_pre_w, w_in, lambda_q1, lambda_k1, lambda_q2, lambda_k2, subln_w, w_o_attn, b_glu,
           w_dw, b_dw, ln_g, ln_b, w_pw2, b_pw2, w_out, norm_post_w):
    batch, seq, _ = x.shape
    assert norm_pre_w.shape[0] == 1, "single layer"
    assert seq % PROJ_TM == 0 and seq % ATTN_TQ == 0 and seq % CONV_T == 0
    x2 = x.reshape(batch * seq, D_MODEL)
    cos_t, sa_t, sb_t = _rotary_tables(seq)
    lam_params = jnp.concatenate([lambda_q1, lambda_k1, lambda_q2, lambda_k2], axis=0).astype(_F32)

    conv_cols = slice(W_CONV_BLKS[0] * COL_BLK, (W_CONV_BLKS[-1] + 1) * COL_BLK)
    pc, h, w_rest_bf = _proj_conv(x2, norm_pre_w, w_in[0][:, conv_cols].astype(_BF16), w_in[0])
    pr = _proj_rest(h, w_rest_bf, cos_t, sa_t, sb_t, seq)
    o, w_o_bf, w_pw2_bf, w_out_bf = _attn(pr, lam_params, subln_w, w_o_attn[0], w_pw2[0], w_out[0],
                                          batch, seq)
    out = _tail(x2, o, pc, pr, b_glu, w_dw[0], b_dw, ln_g, ln_b, w_o_bf, w_pw2_bf, b_pw2, w_out_bf,
                norm_post_w, seq)
    return out.reshape(batch, seq, D_MODEL)
```

```python
import math

import jax
import jax.numpy as jnp
import numpy as np
from jax import lax
from jax.experimental import pallas as pl
from jax.experimental.pallas import tpu as pltpu

D_MODEL = 2048
DA_HEADS = 8
DA_HEAD_DIM = 64
HEAD_W = 2 * DA_HEAD_DIM
D_ATTN = DA_HEADS * HEAD_W
ROPE_THETA = 500000.0
ROT_DIM = DA_HEAD_DIM // 4
D_CONV = 1024
CONV_WIDTH = 31
NORM_EPS = 1e-6
SUBLN_EPS = 1e-5
LN_EPS = 1e-5
LAMBDA_INIT = 0.8 - 0.6 * math.exp(-0.3 * 0)
D_IN = D_ATTN * 4 + 2 * D_CONV + D_CONV + 2 * D_MODEL

COL_BLK = 1024
W_CONV_BLKS = (4, 5, 6)
W_REST_BLKS = (0, 1, 2, 3, 7, 8, 9, 10)
GLU_A_COL, GLU_G_COL, ZC_COL = 0, 1, 2
Q_COL, K_COL, V_COL, ZA_COL, GA_COL, GC_COL = 0, 1, 2, 3, 4, 6
HEADS_PER_COL_BLK = COL_BLK // HEAD_W

PROJ_TM = 1024
PROJ_REST_TM = 1024
PROJ_TN = COL_BLK
PROJ_ROWS = 256
Q_SCALE = DA_HEAD_DIM ** -0.5 * math.log2(math.e)
ATTN_KT = 256
ATTN_QCH = 8
ATTN_TQ = ATTN_QCH * ATTN_KT
ATTN_DIAG_W = 256
ATTN_PASS_W = 256
CONV_T = 256
CONV_HALO = 32
CONV_ROWS = 64
MERGE_TM = 256
MERGE_ROWS = 256
LANES = 128
SUBLANES = 8
MASK_VALUE = -1e30

VMEM_LIMIT_BYTES = 56 * 1024 * 1024

_BF16 = jnp.bfloat16
_F32 = jnp.float32


def _sigmoid(x):
    return 0.5 * jnp.tanh(0.5 * x) + 0.5


def _rotary(acc, cos, sa, sb):
    half = ROT_DIM // 2
    up = pltpu.roll(acc, LANES - half, 1)
    down = pltpu.roll(acc, half, 1)
    return acc * cos + up * sa + down * sb


_PROJ_ROW_CHUNKS = [slice(r, r + PROJ_ROWS) for r in range(0, PROJ_TM, PROJ_ROWS)]


def _proj_conv_kernel(x_ref, nw_ref, w_ref, out_ref, h_ref):
    j = pl.program_id(1)

    def normed(rs):
        x = x_ref[rs, :]
        ms = jnp.mean(x * x, axis=-1, keepdims=True)
        return (x * lax.rsqrt(ms + NORM_EPS) * nw_ref[...]).astype(_BF16)

    @pl.when(j == 0)
    def _():
        for rs in _PROJ_ROW_CHUNKS:
            h = normed(rs)
            h_ref[rs, :] = h
            out_ref[rs, :] = jnp.dot(h, w_ref[...], preferred_element_type=_F32).astype(_BF16)

    @pl.when(j > 0)
    def _():
        out_ref[...] = jnp.dot(h_ref[...], w_ref[...], preferred_element_type=_F32).astype(_BF16)


def _proj_rest_kernel(h_ref, w_ref, cos_ref, sa_ref, sb_ref, out_ref):
    j = pl.program_id(1)
    row_chunks = [slice(r, r + PROJ_ROWS) for r in range(0, h_ref.shape[0], PROJ_ROWS)]

    def store_rotated(acc, rs, scale):
        cos, sa, sb = cos_ref[rs, :], sa_ref[rs, :], sb_ref[rs, :]
        for hh in range(HEADS_PER_COL_BLK):
            sl = slice(hh * HEAD_W, (hh + 1) * HEAD_W)
            r = _rotary(acc[:, sl], cos, sa, sb)
            if scale != 1.0:
                r = r * scale
            out_ref[rs, sl] = r.astype(_BF16)

    @pl.when(j == Q_COL)
    def _():
        for rs in row_chunks:
            store_rotated(jnp.dot(h_ref[rs, :], w_ref[...], preferred_element_type=_F32), rs, Q_SCALE)

    @pl.when(j == K_COL)
    def _():
        for rs in row_chunks:
            store_rotated(jnp.dot(h_ref[rs, :], w_ref[...], preferred_element_type=_F32), rs, 1.0)

    @pl.when(j > K_COL)
    def _():
        out_ref[...] = jnp.dot(h_ref[...], w_ref[...], preferred_element_type=_F32).astype(_BF16)


_PROJ_PARAMS = pltpu.CompilerParams(dimension_semantics=("arbitrary", "arbitrary"),
                                    vmem_limit_bytes=VMEM_LIMIT_BYTES)


def _proj_conv(x2, norm_w, w_conv_bf):
    m = x2.shape[0]
    n_col = w_conv_bf.shape[1] // PROJ_TN
    return pl.pallas_call(
        _proj_conv_kernel,
        out_shape=(jax.ShapeDtypeStruct((m, n_col * PROJ_TN), _BF16),
                   jax.ShapeDtypeStruct((m, D_MODEL), _BF16)),
        grid=(m // PROJ_TM, n_col),
        in_specs=[
            pl.BlockSpec((PROJ_TM, D_MODEL), lambda i, j: (i, 0)),
            pl.BlockSpec((1, D_MODEL), lambda i, j: (0, 0)),
            pl.BlockSpec((D_MODEL, PROJ_TN), lambda i, j: (0, j)),
        ],
        out_specs=(pl.BlockSpec((PROJ_TM, PROJ_TN), lambda i, j: (i, j)),
                   pl.BlockSpec((PROJ_TM, D_MODEL), lambda i, j: (i, 0))),
        compiler_params=_PROJ_PARAMS,
        name="proj_conv",
    )(x2, norm_w, w_conv_bf)


def _proj_rest(h, w_rest_bf, cos_t, sa_t, sb_t, seq):
    m = h.shape[0]
    n_col = w_rest_bf.shape[1] // PROJ_TN
    tm = PROJ_REST_TM
    assert seq % tm == 0
    seq_blocks = seq // tm
    tab_spec = pl.BlockSpec((tm, HEAD_W), lambda i, j: (i % seq_blocks, 0))
    return pl.pallas_call(
        _proj_rest_kernel,
        out_shape=jax.ShapeDtypeStruct((m, n_col * PROJ_TN), _BF16),
        grid=(m // tm, n_col),
        in_specs=[
            pl.BlockSpec((tm, D_MODEL), lambda i, j: (i, 0)),
            pl.BlockSpec((D_MODEL, PROJ_TN), lambda i, j: (0, j)),
            tab_spec, tab_spec, tab_spec,
        ],
        out_specs=pl.BlockSpec((tm, PROJ_TN), lambda i, j: (i, j)),
        compiler_params=_PROJ_PARAMS,
        name="proj_rest",
    )(h, w_rest_bf, cos_t, sa_t, sb_t)


def _attn_kernel(q_ref, k_ref, v_ref, z_ref, lam_ref, sw_ref, o_ref,
                 qqt_ref, vt_ref, s_ref, m_ref, l_ref, acc_ref):
    tq, kt = ATTN_TQ, ATTN_KT
    nchunk = 2 * ATTN_QCH
    seq = q_ref.shape[0]

    for j in range(vt_ref.shape[0]):
        vt_ref[j] = v_ref[j * kt:(j + 1) * kt, :].astype(_F32).T.astype(_BF16)

    lam_p = lam_ref[...]
    s1 = jnp.sum(lam_p[0:1] * lam_p[1:2], axis=1, keepdims=True)
    s2 = jnp.sum(lam_p[2:3] * lam_p[3:4], axis=1, keepdims=True)
    lam = jnp.exp(s1) - jnp.exp(s2) + LAMBDA_INIT

    def scores(j, slot, col_ranges):
        k_sub = k_ref[j * kt:(j + 1) * kt, :]
        for c0, w in col_ranges:
            s_ref[slot, :, c0:c0 + w] = jnp.dot(k_sub, qqt_ref[:, c0:c0 + w], preferred_element_type=_F32)

    def softmax_pv(j, slot, col_ranges):
        for c0, w, diag in col_ranges:
            cs = slice(c0, c0 + w)
            rows = kt if diag is None else diag + w
            s = s_ref[slot, 0:rows, cs]
            if diag is not None:
                kv = lax.broadcasted_iota(jnp.int32, (rows, w), 0)
                qq = lax.broadcasted_iota(jnp.int32, (rows, w), 1) + diag
                s = jnp.where(kv <= qq, s, MASK_VALUE)
            m_prev = m_ref[:, cs]
            m_new = jnp.maximum(m_prev, jnp.max(s, axis=0, keepdims=True))
            alpha = jnp.exp2(m_prev - m_new)
            p = jnp.exp2(s - m_new)
            l_ref[:, cs] = alpha * l_ref[:, cs] + jnp.sum(p, axis=0, keepdims=True)
            acc_ref[:, cs] = alpha * acc_ref[:, cs] + jnp.dot(vt_ref[j, :, 0:rows], p.astype(_BF16),
                                                              preferred_element_type=_F32)
            m_ref[:, cs] = m_new

    all_cols = [(0, 2 * tq)]
    full = [(c0, ATTN_PASS_W, None) for c0 in range(0, 2 * tq, ATTN_PASS_W)]

    def diag_ranges(d):
        out = []
        for c in range(nchunk):
            if c % ATTN_QCH > d:
                out.append((c * kt, kt, None))
            elif c % ATTN_QCH == d:
                out.extend((c * kt + off, ATTN_DIAG_W, off) for off in range(0, kt, ATTN_DIAG_W))
        return out

    def query_tile(qi):
        rows = slice(qi * tq, (qi + 1) * tq)

        qt = q_ref[rows, :].astype(_F32).T
        row = lax.broadcasted_iota(jnp.int32, (HEAD_W, tq), 0)
        qqt_ref[:, 0:tq] = jnp.where(row < DA_HEAD_DIM, qt, 0.0).astype(_BF16)
        qqt_ref[:, tq:2 * tq] = jnp.where(row >= DA_HEAD_DIM, qt, 0.0).astype(_BF16)
        m_ref[...] = jnp.full(m_ref.shape, MASK_VALUE, _F32)
        l_ref[...] = jnp.zeros(l_ref.shape, _F32)
        acc_ref[...] = jnp.zeros(acc_ref.shape, _F32)

        scores(0, 0, all_cols)

        def block_pair(i):
            j = 2 * i
            for rng in full:
                scores(j + 1, 1, [rng[:2]])
                softmax_pv(j, 0, [rng])
            for rng in full:
                scores(j + 2, 0, [rng[:2]])
                softmax_pv(j + 1, 1, [rng])

        nfull = ATTN_QCH * qi
        for i in range(nfull // 2):
            block_pair(i)
        for d in range(ATTN_QCH):
            nxt = {c * kt for c in range(nchunk) if c % ATTN_QCH >= d + 1} if d + 1 < ATTN_QCH else set()
            for rng in diag_ranges(d):
                if rng[0] in nxt:
                    scores(nfull + d + 1, (d + 1) % 2, [(rng[0], kt)])
                softmax_pv(nfull + d, d % 2, [rng])

        ot = acc_ref[...] / l_ref[...]
        ot = ot[:, 0:tq] - lam * ot[:, tq:2 * tq]
        ot = ot * lax.rsqrt(jnp.mean(ot * ot, axis=0, keepdims=True) + SUBLN_EPS)
        o = ot.T * sw_ref[...] * (1.0 - LAMBDA_INIT)
        z = z_ref[rows, :].astype(_F32)
        o_ref[rows, :] = (o * (z * _sigmoid(z))).astype(_BF16)

    for qi in range(seq // tq):
        query_tile(qi)


def _attn(pr, lam_params, subln_w, batch, seq):
    m = pr.shape[0]
    assert ATTN_TQ == ATTN_QCH * ATTN_KT and seq % ATTN_TQ == 0 and ATTN_QCH % 2 == 0
    hb = HEADS_PER_COL_BLK

    def head_cols(col):
        return pl.BlockSpec((seq, HEAD_W), lambda b, h: (b, col * hb + h))

    return pl.pallas_call(
        _attn_kernel,
        out_shape=jax.ShapeDtypeStruct((m, D_ATTN), _BF16),
        grid=(batch, DA_HEADS),
        in_specs=[
            head_cols(Q_COL), head_cols(K_COL), head_cols(V_COL), head_cols(ZA_COL),
            pl.BlockSpec((4, DA_HEAD_DIM), lambda b, h: (0, 0)),
            pl.BlockSpec((1, HEAD_W), lambda b, h: (0, 0)),
        ],
        out_specs=pl.BlockSpec((seq, HEAD_W), lambda b, h: (b, h)),
        scratch_shapes=[
            pltpu.VMEM((HEAD_W, 2 * ATTN_TQ), _BF16),
            pltpu.VMEM((seq // ATTN_KT, HEAD_W, ATTN_KT), _BF16),
            pltpu.VMEM((2, ATTN_KT, 2 * ATTN_TQ), _F32),
            pltpu.VMEM((1, 2 * ATTN_TQ), _F32),
            pltpu.VMEM((1, 2 * ATTN_TQ), _F32),
            pltpu.VMEM((HEAD_W, 2 * ATTN_TQ), _F32),
        ],
        compiler_params=pltpu.CompilerParams(
            dimension_semantics=("arbitrary", "arbitrary"),
            vmem_limit_bytes=VMEM_LIMIT_BYTES),
        name="attn",
    )(pr, pr, pr, pr, lam_params, subln_w)


def _conv_kernel(a_ref, g_ref, ah_ref, gh_ref, z_ref, bglu_ref, wdw_ref, bdw_ref, lng_ref, lnb_ref, *refs):
    n_rest = len(W_REST_BLKS)
    w_in_refs, (wo_ref, wp_ref, wout_ref) = refs[:n_rest], refs[n_rest:n_rest + 3]
    o_ref, wrest_bf_ref, wo_bf_ref, wp_bf_ref, wout_bf_ref, stage_ref, shift_ref, y_ref = refs[n_rest + 3:]
    for k, w_ref in enumerate(w_in_refs):
        wrest_bf_ref[:, k * COL_BLK:(k + 1) * COL_BLK] = w_ref[...].astype(_BF16)
    wo_bf_ref[...] = wo_ref[...].astype(_BF16)
    wp_bf_ref[...] = wp_ref[...].astype(_BF16)
    wout_bf_ref[...] = wout_ref[...].astype(_BF16)

    t = pl.program_id(1)
    ba = bglu_ref[:, 0:D_CONV]
    bg = bglu_ref[:, D_CONV:2 * D_CONV]

    def glu(a, g):
        return (a.astype(_F32) + ba) * _sigmoid(g.astype(_F32) + bg)

    halo = glu(ah_ref[...], gh_ref[...])
    halo = jnp.where(t > 0, halo, jnp.zeros_like(halo))
    cur = glu(a_ref[...], g_ref[...])
    nblk = D_CONV // LANES
    for c in range(nblk):
        ls = slice(c * LANES, (c + 1) * LANES)
        stage_ref[c, 0:CONV_HALO, :] = halo[:, ls]
        stage_ref[c, CONV_HALO:CONV_HALO + CONV_T, :] = cur[:, ls]

    first = CONV_HALO - (CONV_WIDTH - 1)
    shift_rows = shift_ref.shape[1]
    row_starts = range(0, CONV_T, CONV_ROWS)

    def channel_block(c, carry):
        for s in range(1, SUBLANES):
            shift_ref[s - 1] = stage_ref[c, s:s + shift_rows, :]
        accs = [jnp.broadcast_to(bdw_ref[c], (CONV_ROWS, LANES)) for _ in row_starts]
        for j in range(CONV_WIDTH):
            q, s = divmod(first + j, SUBLANES)
            w_j = jnp.broadcast_to(wdw_ref[c, j:j + 1, :], (CONV_ROWS, LANES))
            for k, r in enumerate(row_starts):
                lo = r + q * SUBLANES
                if s == 0:
                    win = stage_ref[c, lo:lo + CONV_ROWS, :]
                else:
                    win = shift_ref[s - 1, lo:lo + CONV_ROWS, :]
                accs[k] = accs[k] + win * w_j
        for k, r in enumerate(row_starts):
            y_ref[c, r:r + CONV_ROWS, :] = accs[k]
        return carry

    lax.fori_loop(0, nblk, channel_block, 0)

    y = jnp.concatenate([y_ref[c] for c in range(nblk)], axis=1)
    mu = jnp.mean(y, axis=-1, keepdims=True)
    yc = y - mu
    var = jnp.mean(yc * yc, axis=-1, keepdims=True)
    u = yc * lax.rsqrt(var + LN_EPS) * lng_ref[...] + lnb_ref[...]
    z = z_ref[...].astype(_F32)
    o_ref[...] = ((u * _sigmoid(u)) * (z * _sigmoid(z))).astype(_BF16)


def _conv(pc, b_glu, w_dw, b_dw, ln_g, ln_b, w_in, w_o, w_pw2, w_out, batch, seq):
    m = pc.shape[0]
    nt = seq // CONV_T
    steps = batch * nt
    halo_per_tile = CONV_T // CONV_HALO
    nblk = D_CONV // LANES
    w_blk = w_dw.reshape(CONV_WIDTH, nblk, LANES).transpose(1, 0, 2)
    b_blk = b_dw.reshape(nblk, 1, LANES)

    def halo_idx(col):
        return lambda b, t: (jnp.maximum((b * nt + t) * halo_per_tile - 1, 0), col * (COL_BLK // D_CONV))

    def vec(n):
        return pl.BlockSpec((1, n), lambda b, t: (0, 0))

    def row_slab(n_rows, n_cols, col_blk=0):
        rows = n_rows // steps
        assert rows * steps == n_rows and rows % (2 * SUBLANES) == 0
        return pl.BlockSpec((rows, n_cols), lambda b, t: (b * nt + t, col_blk))

    n_rest = len(W_REST_BLKS)
    small = (w_o, w_pw2, w_out)
    return pl.pallas_call(
        _conv_kernel,
        out_shape=(jax.ShapeDtypeStruct((m, D_CONV), _BF16),
                   jax.ShapeDtypeStruct((D_MODEL, n_rest * COL_BLK), _BF16),
                   *[jax.ShapeDtypeStruct(w.shape, _BF16) for w in small]),
        grid=(batch, nt),
        in_specs=[
            pl.BlockSpec((CONV_T, D_CONV), lambda b, t: (b * nt + t, GLU_A_COL)),
            pl.BlockSpec((CONV_T, D_CONV), lambda b, t: (b * nt + t, GLU_G_COL)),
            pl.BlockSpec((CONV_HALO, D_CONV), halo_idx(GLU_A_COL)),
            pl.BlockSpec((CONV_HALO, D_CONV), halo_idx(GLU_G_COL)),
            pl.BlockSpec((CONV_T, D_CONV), lambda b, t: (b * nt + t, ZC_COL)),
            vec(2 * D_CONV),
            pl.BlockSpec((nblk, CONV_WIDTH, LANES), lambda b, t: (0, 0, 0)),
            pl.BlockSpec((nblk, 1, LANES), lambda b, t: (0, 0, 0)),
            vec(D_CONV), vec(D_CONV),
            *[row_slab(D_MODEL, COL_BLK, blk) for blk in W_REST_BLKS],
            *[row_slab(*w.shape) for w in small],
        ],
        out_specs=(pl.BlockSpec((CONV_T, D_CONV), lambda b, t: (b * nt + t, 0)),
                   row_slab(D_MODEL, n_rest * COL_BLK),
                   *[row_slab(*w.shape) for w in small]),
        scratch_shapes=[
            pltpu.VMEM((nblk, CONV_HALO + CONV_T, LANES), _F32),
            pltpu.VMEM((SUBLANES - 1, CONV_HALO + CONV_T - SUBLANES, LANES), _F32),
            pltpu.VMEM((nblk, CONV_T, LANES), _F32),
        ],
        compiler_params=pltpu.CompilerParams(
            dimension_semantics=("arbitrary", "arbitrary"),
            vmem_limit_bytes=VMEM_LIMIT_BYTES),
        name="conv",
    )(pc, pc, pc, pc, pc, b_glu, w_blk, b_blk, ln_g, ln_b, *([w_in] * n_rest), *small)


def _merge_kernel(x_ref, o_ref, u_ref, ga0_ref, ga1_ref, gc0_ref, gc1_ref, wo_ref, wp_ref, bp_ref,
                  wout_ref, nw_ref, out_ref):
    for r in range(0, MERGE_TM, MERGE_ROWS):
        rs = slice(r, r + MERGE_ROWS)
        ya = jnp.dot(o_ref[rs, :], wo_ref[...], preferred_element_type=_F32)
        yc = jnp.dot(u_ref[rs, :], wp_ref[...], preferred_element_type=_F32) + bp_ref[...]
        ga = jnp.concatenate([ga0_ref[rs, :], ga1_ref[rs, :]], axis=1).astype(_F32)
        gc = jnp.concatenate([gc0_ref[rs, :], gc1_ref[rs, :]], axis=1).astype(_F32)
        mix = _sigmoid(ga) * ya + _sigmoid(gc) * yc
        out = jnp.dot(mix.astype(_BF16), wout_ref[...], preferred_element_type=_F32)
        ms = jnp.mean(out * out, axis=-1, keepdims=True)
        out_ref[rs, :] = x_ref[rs, :] + out * lax.rsqrt(ms + NORM_EPS) * nw_ref[...]


def _merge(x2, o, u, proj, w_o_bf, w_pw2_bf, b_pw2, w_out_bf, norm_w):
    m = x2.shape[0]

    def const(shape):
        return pl.BlockSpec(shape, lambda i: (0, 0), pipeline_mode=pl.Buffered(1))

    def gate(col):
        return pl.BlockSpec((MERGE_TM, COL_BLK), lambda i: (i, col))

    return pl.pallas_call(
        _merge_kernel,
        out_shape=jax.ShapeDtypeStruct((m, D_MODEL), _F32),
        grid=(m // MERGE_TM,),
        in_specs=[
            pl.BlockSpec((MERGE_TM, D_MODEL), lambda i: (i, 0)),
            pl.BlockSpec((MERGE_TM, D_ATTN), lambda i: (i, 0)),
            pl.BlockSpec((MERGE_TM, D_CONV), lambda i: (i, 0)),
            gate(GA_COL), gate(GA_COL + 1), gate(GC_COL), gate(GC_COL + 1),
            const((D_ATTN, D_MODEL)),
            const((D_CONV, D_MODEL)),
            const((1, D_MODEL)),
            const((D_MODEL, D_MODEL)),
            const((1, D_MODEL)),
        ],
        out_specs=pl.BlockSpec((MERGE_TM, D_MODEL), lambda i: (i, 0)),
        compiler_params=pltpu.CompilerParams(
            dimension_semantics=("arbitrary",),
            vmem_limit_bytes=VMEM_LIMIT_BYTES),
        name="merge",
    )(x2, o, u, proj, proj, proj, proj, w_o_bf, w_pw2_bf, b_pw2, w_out_bf, norm_w)


def _rotary_tables(seq):
    f32 = np.float32
    half = ROT_DIM // 2
    inv_freq = f32(ROPE_THETA) ** (-np.arange(0, ROT_DIM, 2, dtype=f32) / f32(ROT_DIM))
    ang = np.arange(seq, dtype=f32)[:, None] * inv_freq[None, :]
    cos, sin = np.cos(ang).astype(f32), np.sin(ang).astype(f32)
    ones = np.ones((seq, DA_HEAD_DIM - ROT_DIM), f32)
    zeros = np.zeros((seq, DA_HEAD_DIM - ROT_DIM), f32)
    zh = np.zeros((seq, half), f32)
    cos_c = np.concatenate([cos, cos, ones], axis=1)
    sa_c = np.concatenate([-sin, zh, zeros], axis=1)
    sb_c = np.concatenate([zh, sin, zeros], axis=1)
    tile2 = lambda a: np.concatenate([a, a], axis=1)
    return tile2(cos_c), tile2(sa_c), tile2(sb_c)


def kernel(x, norm_pre_w, w_in, lambda_q1, lambda_k1, lambda_q2, lambda_k2, subln_w, w_o_attn, b_glu,
           w_dw, b_dw, ln_g, ln_b, w_pw2, b_pw2, w_out, norm_post_w):
    batch, seq, _ = x.shape
    assert norm_pre_w.shape[0] == 1, "single layer"
    assert seq % PROJ_TM == 0 and seq % ATTN_TQ == 0 and seq % CONV_T == 0
    x2 = x.reshape(batch * seq, D_MODEL)
    cos_t, sa_t, sb_t = _rotary_tables(seq)
    lam_params = jnp.concatenate([lambda_q1, lambda_k1, lambda_q2, lambda_k2], axis=0).astype(_F32)

    conv_cols = slice(W_CONV_BLKS[0] * COL_BLK, (W_CONV_BLKS[-1] + 1) * COL_BLK)
    pc, h = _proj_conv(x2, norm_pre_w, w_in[0][:, conv_cols].astype(_BF16))
    u, w_rest_bf, w_o_bf, w_pw2_bf, w_out_bf = _conv(pc, b_glu, w_dw[0], b_dw, ln_g, ln_b, w_in[0],
                                                     w_o_attn[0], w_pw2[0], w_out[0], batch, seq)
    pr = _proj_rest(h, w_rest_bf, cos_t, sa_t, sb_t, seq)
    o = _attn(pr, lam_params, subln_w, batch, seq)
    out = _merge(x2, o, u, pr, w_o_bf, w_pw2_bf, b_pw2, w_out_bf, norm_post_w)
    return out.reshape(batch, seq, D_MODEL)
```

```python
import math

import jax
import jax.numpy as jnp
import numpy as np
from jax import lax
from jax.experimental import pallas as pl
from jax.experimental.pallas import tpu as pltpu

D_MODEL = 2048
DA_HEADS = 8
DA_HEAD_DIM = 64
HEAD_W = 2 * DA_HEAD_DIM
D_ATTN = DA_HEADS * HEAD_W
ROPE_THETA = 500000.0
ROT_DIM = DA_HEAD_DIM // 4
D_CONV = 1024
CONV_WIDTH = 31
NORM_EPS = 1e-6
SUBLN_EPS = 1e-5
LN_EPS = 1e-5
LAMBDA_INIT = 0.8 - 0.6 * math.exp(-0.3 * 0)
D_IN = D_ATTN * 4 + 2 * D_CONV + D_CONV + 2 * D_MODEL

COL_BLK = 1024
W_CONV_BLKS = (4, 5, 6)
W_REST_BLKS = (0, 1, 2, 3, 7, 8, 9, 10)
GLU_A_COL, GLU_G_COL, ZC_COL = 0, 1, 2
Q_COL, K_COL, V_COL, ZA_COL, GA_COL, GC_COL = 0, 1, 2, 3, 4, 6
HEADS_PER_COL_BLK = COL_BLK // HEAD_W

PROJ_CONV_TM = 512
PROJ_REST_TM = 1024
PROJ_TN = COL_BLK
PROJ_ROWS = 256
Q_SCALE = DA_HEAD_DIM ** -0.5 * math.log2(math.e)
ATTN_KT = 256
ATTN_QCH = 8
ATTN_TQ = ATTN_QCH * ATTN_KT
ATTN_DIAG_W = 256
ATTN_PASS_W = 256
CONV_T = 256
CONV_HALO = 32
CONV_ROWS = 64
MERGE_TM = 256
MERGE_ROWS = 256
LANES = 128
SUBLANES = 8
MASK_VALUE = -1e30

VMEM_LIMIT_BYTES = 56 * 1024 * 1024

_BF16 = jnp.bfloat16
_F32 = jnp.float32


def _sigmoid(x):
    return 0.5 * jnp.tanh(0.5 * x) + 0.5


def _rotary(acc, cos, sa, sb):
    half = ROT_DIM // 2
    up = pltpu.roll(acc, LANES - half, 1)
    down = pltpu.roll(acc, half, 1)
    return acc * cos + up * sa + down * sb


def _proj_conv_kernel(x_ref, nw_ref, wf_ref, out_ref, h_ref, w_ref):
    i, j = pl.program_id(0), pl.program_id(1)
    row_chunks = [slice(r, r + PROJ_ROWS) for r in range(0, x_ref.shape[0], PROJ_ROWS)]

    @pl.when(i == 0)
    def _():
        w_ref[j] = wf_ref[...].astype(_BF16)

    def normed(rs):
        x = x_ref[rs, :]
        ms = jnp.mean(x * x, axis=-1, keepdims=True)
        return (x * lax.rsqrt(ms + NORM_EPS) * nw_ref[...]).astype(_BF16)

    @pl.when(j == 0)
    def _():
        for rs in row_chunks:
            h = normed(rs)
            h_ref[rs, :] = h
            out_ref[rs, :] = jnp.dot(h, w_ref[0], preferred_element_type=_F32).astype(_BF16)

    @pl.when(j > 0)
    def _():
        out_ref[...] = jnp.dot(h_ref[...], w_ref[j], preferred_element_type=_F32).astype(_BF16)


def _proj_rest_kernel(h_ref, w_ref, cos_ref, sa_ref, sb_ref, out_ref):
    j = pl.program_id(1)
    row_chunks = [slice(r, r + PROJ_ROWS) for r in range(0, h_ref.shape[0], PROJ_ROWS)]

    def store_rotated(acc, rs, scale):
        cos, sa, sb = cos_ref[rs, :], sa_ref[rs, :], sb_ref[rs, :]
        for hh in range(HEADS_PER_COL_BLK):
            sl = slice(hh * HEAD_W, (hh + 1) * HEAD_W)
            r = _rotary(acc[:, sl], cos, sa, sb)
            if scale != 1.0:
                r = r * scale
            out_ref[rs, sl] = r.astype(_BF16)

    @pl.when(j == Q_COL)
    def _():
        for rs in row_chunks:
            store_rotated(jnp.dot(h_ref[rs, :], w_ref[...], preferred_element_type=_F32), rs, Q_SCALE)

    @pl.when(j == K_COL)
    def _():
        for rs in row_chunks:
            store_rotated(jnp.dot(h_ref[rs, :], w_ref[...], preferred_element_type=_F32), rs, 1.0)

    @pl.when(j > K_COL)
    def _():
        out_ref[...] = jnp.dot(h_ref[...], w_ref[...], preferred_element_type=_F32).astype(_BF16)


_PROJ_PARAMS = pltpu.CompilerParams(dimension_semantics=("arbitrary", "arbitrary"),
                                    vmem_limit_bytes=VMEM_LIMIT_BYTES)


def _proj_conv(x2, norm_w, w_in):
    m = x2.shape[0]
    n_col = len(W_CONV_BLKS)
    tm = PROJ_CONV_TM
    first_blk, last_blk = W_CONV_BLKS[0], W_CONV_BLKS[-1]
    assert W_CONV_BLKS == tuple(range(first_blk, last_blk + 1))
    return pl.pallas_call(
        _proj_conv_kernel,
        out_shape=(jax.ShapeDtypeStruct((m, n_col * PROJ_TN), _BF16),
                   jax.ShapeDtypeStruct((m, D_MODEL), _BF16)),
        grid=(m // tm, n_col),
        in_specs=[
            pl.BlockSpec((tm, D_MODEL), lambda i, j: (i, 0)),
            pl.BlockSpec((1, D_MODEL), lambda i, j: (0, 0)),
            pl.BlockSpec((D_MODEL, PROJ_TN), lambda i, j: (0, jnp.where(i == 0, first_blk + j, last_blk))),
        ],
        out_specs=(pl.BlockSpec((tm, PROJ_TN), lambda i, j: (i, j)),
                   pl.BlockSpec((tm, D_MODEL), lambda i, j: (i, 0))),
        scratch_shapes=[pltpu.VMEM((n_col, D_MODEL, PROJ_TN), _BF16)],
        compiler_params=_PROJ_PARAMS,
        name="proj_conv",
    )(x2, norm_w, w_in)


def _proj_rest(h, w_rest_bf, cos_t, sa_t, sb_t, seq):
    m = h.shape[0]
    n_col = w_rest_bf.shape[1] // PROJ_TN
    tm = PROJ_REST_TM
    assert seq % tm == 0
    seq_blocks = seq // tm
    tab_spec = pl.BlockSpec((tm, HEAD_W), lambda i, j: (i % seq_blocks, 0))
    return pl.pallas_call(
        _proj_rest_kernel,
        out_shape=jax.ShapeDtypeStruct((m, n_col * PROJ_TN), _BF16),
        grid=(m // tm, n_col),
        in_specs=[
            pl.BlockSpec((tm, D_MODEL), lambda i, j: (i, 0)),
            pl.BlockSpec((D_MODEL, PROJ_TN), lambda i, j: (0, j)),
            tab_spec, tab_spec, tab_spec,
        ],
        out_specs=pl.BlockSpec((tm, PROJ_TN), lambda i, j: (i, j)),
        compiler_params=_PROJ_PARAMS,
        name="proj_rest",
    )(h, w_rest_bf, cos_t, sa_t, sb_t)


def _attn_kernel(q_ref, k_ref, v_ref, z_ref, lam_ref, sw_ref, o_ref,
                 qqt_ref, vt_ref, s_ref, m_ref, l_ref, acc_ref):
    tq, kt = ATTN_TQ, ATTN_KT
    nchunk = 2 * ATTN_QCH
    seq = q_ref.shape[0]

    for j in range(vt_ref.shape[0]):
        vt_ref[j] = v_ref[j * kt:(j + 1) * kt, :].astype(_F32).T.astype(_BF16)

    lam_p = lam_ref[...]
    s1 = jnp.sum(lam_p[0:1] * lam_p[1:2], axis=1, keepdims=True)
    s2 = jnp.sum(lam_p[2:3] * lam_p[3:4], axis=1, keepdims=True)
    lam = jnp.exp(s1) - jnp.exp(s2) + LAMBDA_INIT

    def scores(j, slot, col_ranges):
        k_sub = k_ref[j * kt:(j + 1) * kt, :]
        for c0, w in col_ranges:
            s_ref[slot, :, c0:c0 + w] = jnp.dot(k_sub, qqt_ref[:, c0:c0 + w], preferred_element_type=_F32)

    def softmax_pv(j, slot, col_ranges):
        for c0, w, diag in col_ranges:
            cs = slice(c0, c0 + w)
            rows = kt if diag is None else diag + w
            s = s_ref[slot, 0:rows, cs]
            if diag is not None:
                kv = lax.broadcasted_iota(jnp.int32, (rows, w), 0)
                qq = lax.broadcasted_iota(jnp.int32, (rows, w), 1) + diag
                s = jnp.where(kv <= qq, s, MASK_VALUE)
            m_prev = m_ref[:, cs]
            m_new = jnp.maximum(m_prev, jnp.max(s, axis=0, keepdims=True))
            alpha = jnp.exp2(m_prev - m_new)
            p = jnp.exp2(s - m_new)
            l_ref[:, cs] = alpha * l_ref[:, cs] + jnp.sum(p, axis=0, keepdims=True)
            acc_ref[:, cs] = alpha * acc_ref[:, cs] + jnp.dot(vt_ref[j, :, 0:rows], p.astype(_BF16),
                                                              preferred_element_type=_F32)
            m_ref[:, cs] = m_new

    all_cols = [(0, 2 * tq)]
    full = [(c0, ATTN_PASS_W, None) for c0 in range(0, 2 * tq, ATTN_PASS_W)]

    def diag_ranges(d):
        out = []
        for c in range(nchunk):
            if c % ATTN_QCH > d:
                out.append((c * kt, kt, None))
            elif c % ATTN_QCH == d:
                out.extend((c * kt + off, ATTN_DIAG_W, off) for off in range(0, kt, ATTN_DIAG_W))
        return out

    def query_tile(qi):
        rows = slice(qi * tq, (qi + 1) * tq)

        qt = q_ref[rows, :].astype(_F32).T
        row = lax.broadcasted_iota(jnp.int32, (HEAD_W, tq), 0)
        qqt_ref[:, 0:tq] = jnp.where(row < DA_HEAD_DIM, qt, 0.0).astype(_BF16)
        qqt_ref[:, tq:2 * tq] = jnp.where(row >= DA_HEAD_DIM, qt, 0.0).astype(_BF16)
        m_ref[...] = jnp.full(m_ref.shape, MASK_VALUE, _F32)
        l_ref[...] = jnp.zeros(l_ref.shape, _F32)
        acc_ref[...] = jnp.zeros(acc_ref.shape, _F32)

        scores(0, 0, all_cols)

        def block_pair(i):
            j = 2 * i
            for rng in full:
                scores(j + 1, 1, [rng[:2]])
                softmax_pv(j, 0, [rng])
            for rng in full:
                scores(j + 2, 0, [rng[:2]])
                softmax_pv(j + 1, 1, [rng])

        nfull = ATTN_QCH * qi
        for i in range(nfull // 2):
            block_pair(i)
        for d in range(ATTN_QCH):
            nxt = {c * kt for c in range(nchunk) if c % ATTN_QCH >= d + 1} if d + 1 < ATTN_QCH else set()
            for rng in diag_ranges(d):
                if rng[0] in nxt:
                    scores(nfull + d + 1, (d + 1) % 2, [(rng[0], kt)])
                softmax_pv(nfull + d, d % 2, [rng])

        ot = acc_ref[...] / l_ref[...]
        ot = ot[:, 0:tq] - lam * ot[:, tq:2 * tq]
        ot = ot * lax.rsqrt(jnp.mean(ot * ot, axis=0, keepdims=True) + SUBLN_EPS)
        o = ot.T * sw_ref[...] * (1.0 - LAMBDA_INIT)
        z = z_ref[rows, :].astype(_F32)
        o_ref[rows, :] = (o * (z * _sigmoid(z))).astype(_BF16)

    for qi in range(seq // tq):
        query_tile(qi)


def _attn(pr, lam_params, subln_w, batch, seq):
    m = pr.shape[0]
    assert ATTN_TQ == ATTN_QCH * ATTN_KT and seq % ATTN_TQ == 0 and ATTN_QCH % 2 == 0
    hb = HEADS_PER_COL_BLK

    def head_cols(col):
        return pl.BlockSpec((seq, HEAD_W), lambda b, h: (b, col * hb + h))

    return pl.pallas_call(
        _attn_kernel,
        out_shape=jax.ShapeDtypeStruct((m, D_ATTN), _BF16),
        grid=(batch, DA_HEADS),
        in_specs=[
            head_cols(Q_COL), head_cols(K_COL), head_cols(V_COL), head_cols(ZA_COL),
            pl.BlockSpec((4, DA_HEAD_DIM), lambda b, h: (0, 0)),
            pl.BlockSpec((1, HEAD_W), lambda b, h: (0, 0)),
        ],
        out_specs=pl.BlockSpec((seq, HEAD_W), lambda b, h: (b, h)),
        scratch_shapes=[
            pltpu.VMEM((HEAD_W, 2 * ATTN_TQ), _BF16),
            pltpu.VMEM((seq // ATTN_KT, HEAD_W, ATTN_KT), _BF16),
            pltpu.VMEM((2, ATTN_KT, 2 * ATTN_TQ), _F32),
            pltpu.VMEM((1, 2 * ATTN_TQ), _F32),
            pltpu.VMEM((1, 2 * ATTN_TQ), _F32),
            pltpu.VMEM((HEAD_W, 2 * ATTN_TQ), _F32),
        ],
        compiler_params=pltpu.CompilerParams(
            dimension_semantics=("arbitrary", "arbitrary"),
            vmem_limit_bytes=VMEM_LIMIT_BYTES),
        name="attn",
    )(pr, pr, pr, pr, lam_params, subln_w)


def _conv_kernel(a_ref, g_ref, ah_ref, gh_ref, z_ref, bglu_ref, wdw_ref, bdw_ref, lng_ref, lnb_ref, *refs):
    n_rest = len(W_REST_BLKS)
    w_in_refs, (wo_ref, wp_ref, wout_ref) = refs[:n_rest], refs[n_rest:n_rest + 3]
    o_ref, wrest_bf_ref, wo_bf_ref, wp_bf_ref, wout_bf_ref, stage_ref, shift_ref, y_ref = refs[n_rest + 3:]
    for k, w_ref in enumerate(w_in_refs):
        wrest_bf_ref[:, k * COL_BLK:(k + 1) * COL_BLK] = w_ref[...].astype(_BF16)
    wo_bf_ref[...] = wo_ref[...].astype(_BF16)
    wp_bf_ref[...] = wp_ref[...].astype(_BF16)
    wout_bf_ref[...] = wout_ref[...].astype(_BF16)

    t = pl.program_id(1)
    ba = bglu_ref[:, 0:D_CONV]
    bg = bglu_ref[:, D_CONV:2 * D_CONV]

    def glu(a, g):
        return (a.astype(_F32) + ba) * _sigmoid(g.astype(_F32) + bg)

    halo = glu(ah_ref[...], gh_ref[...])
    halo = jnp.where(t > 0, halo, jnp.zeros_like(halo))
    cur = glu(a_ref[...], g_ref[...])
    nblk = D_CONV // LANES
    for c in range(nblk):
        ls = slice(c * LANES, (c + 1) * LANES)
        stage_ref[c, 0:CONV_HALO, :] = halo[:, ls]
        stage_ref[c, CONV_HALO:CONV_HALO + CONV_T, :] = cur[:, ls]

    first = CONV_HALO - (CONV_WIDTH - 1)
    shift_rows = shift_ref.shape[1]
    row_starts = range(0, CONV_T, CONV_ROWS)

    def channel_block(c, carry):
        for s in range(1, SUBLANES):
            shift_ref[s - 1] = stage_ref[c, s:s + shift_rows, :]
        accs = [jnp.broadcast_to(bdw_ref[c], (CONV_ROWS, LANES)) for _ in row_starts]
        for j in range(CONV_WIDTH):
            q, s = divmod(first + j, SUBLANES)
            w_j = jnp.broadcast_to(wdw_ref[c, j:j + 1, :], (CONV_ROWS, LANES))
            for k, r in enumerate(row_starts):
                lo = r + q * SUBLANES
                if s == 0:
                    win = stage_ref[c, lo:lo + CONV_ROWS, :]
                else:
                    win = shift_ref[s - 1, lo:lo + CONV_ROWS, :]
                accs[k] = accs[k] + win * w_j
        for k, r in enumerate(row_starts):
            y_ref[c, r:r + CONV_ROWS, :] = accs[k]
        return carry

    lax.fori_loop(0, nblk, channel_block, 0)

    y = jnp.concatenate([y_ref[c] for c in range(nblk)], axis=1)
    mu = jnp.mean(y, axis=-1, keepdims=True)
    yc = y - mu
    var = jnp.mean(yc * yc, axis=-1, keepdims=True)
    u = yc * lax.rsqrt(var + LN_EPS) * lng_ref[...] + lnb_ref[...]
    z = z_ref[...].astype(_F32)
    o_ref[...] = ((u * _sigmoid(u)) * (z * _sigmoid(z))).astype(_BF16)


def _conv(pc, b_glu, w_dw, b_dw, ln_g, ln_b, w_in, w_o, w_pw2, w_out, batch, seq):
    m = pc.shape[0]
    nt = seq // CONV_T
    steps = batch * nt
    halo_per_tile = CONV_T // CONV_HALO
    nblk = D_CONV // LANES
    w_blk = w_dw.reshape(CONV_WIDTH, nblk, LANES).transpose(1, 0, 2)
    b_blk = b_dw.reshape(nblk, 1, LANES)

    def halo_idx(col):
        return lambda b, t: (jnp.maximum((b * nt + t) * halo_per_tile - 1, 0), col * (COL_BLK // D_CONV))

    def vec(n):
        return pl.BlockSpec((1, n), lambda b, t: (0, 0))

    def row_slab(n_rows, n_cols, col_blk=0):
        rows = n_rows // steps
        assert rows * steps == n_rows and rows % (2 * SUBLANES) == 0
        return pl.BlockSpec((rows, n_cols), lambda b, t: (b * nt + t, col_blk))

    n_rest = len(W_REST_BLKS)
    small = (w_o, w_pw2, w_out)
    return pl.pallas_call(
        _conv_kernel,
        out_shape=(jax.ShapeDtypeStruct((m, D_CONV), _BF16),
                   jax.ShapeDtypeStruct((D_MODEL, n_rest * COL_BLK), _BF16),
                   *[jax.ShapeDtypeStruct(w.shape, _BF16) for w in small]),
        grid=(batch, nt),
        in_specs=[
            pl.BlockSpec((CONV_T, D_CONV), lambda b, t: (b * nt + t, GLU_A_COL)),
            pl.BlockSpec((CONV_T, D_CONV), lambda b, t: (b * nt + t, GLU_G_COL)),
            pl.BlockSpec((CONV_HALO, D_CONV), halo_idx(GLU_A_COL)),
            pl.BlockSpec((CONV_HALO, D_CONV), halo_idx(GLU_G_COL)),
            pl.BlockSpec((CONV_T, D_CONV), lambda b, t: (b * nt + t, ZC_COL)),
            vec(2 * D_CONV),
            pl.BlockSpec((nblk, CONV_WIDTH, LANES), lambda b, t: (0, 0, 0)),
            pl.BlockSpec((nblk, 1, LANES), lambda b, t: (0, 0, 0)),
            vec(D_CONV), vec(D_CONV),
            *[row_slab(D_MODEL, COL_BLK, blk) for blk in W_REST_BLKS],
            *[row_slab(*w.shape) for w in small],
        ],
        out_specs=(pl.BlockSpec((CONV_T, D_CONV), lambda b, t: (b * nt + t, 0)),
                   row_slab(D_MODEL, n_rest * COL_BLK),
                   *[row_slab(*w.shape) for w in small]),
        scratch_shapes=[
            pltpu.VMEM((nblk, CONV_HALO + CONV_T, LANES), _F32),
            pltpu.VMEM((SUBLANES - 1, CONV_HALO + CONV_T - SUBLANES, LANES), _F32),
            pltpu.VMEM((nblk, CONV_T, LANES), _F32),
        ],
        compiler_params=pltpu.CompilerParams(
            dimension_semantics=("arbitrary", "arbitrary"),
            vmem_limit_bytes=VMEM_LIMIT_BYTES),
        name="conv",
    )(pc, pc, pc, pc, pc, b_glu, w_blk, b_blk, ln_g, ln_b, *([w_in] * n_rest), *small)


def _merge_kernel(x_ref, o_ref, u_ref, ga0_ref, ga1_ref, gc0_ref, gc1_ref, wo_ref, wp_ref, bp_ref,
                  wout_ref, nw_ref, out_ref):
    for r in range(0, MERGE_TM, MERGE_ROWS):
        rs = slice(r, r + MERGE_ROWS)
        ya = jnp.dot(o_ref[rs, :], wo_ref[...], preferred_element_type=_F32)
        yc = jnp.dot(u_ref[rs, :], wp_ref[...], preferred_element_type=_F32) + bp_ref[...]
        ga = jnp.concatenate([ga0_ref[rs, :], ga1_ref[rs, :]], axis=1).astype(_F32)
        gc = jnp.concatenate([gc0_ref[rs, :], gc1_ref[rs, :]], axis=1).astype(_F32)
        mix = _sigmoid(ga) * ya + _sigmoid(gc) * yc
        out = jnp.dot(mix.astype(_BF16), wout_ref[...], preferred_element_type=_F32)
        ms = jnp.mean(out * out, axis=-1, keepdims=True)
        out_ref[rs, :] = x_ref[rs, :] + out * lax.rsqrt(ms + NORM_EPS) * nw_ref[...]


def _merge(x2, o, u, proj, w_o_bf, w_pw2_bf, b_pw2, w_out_bf, norm_w):
    m = x2.shape[0]

    def const(shape):
        return pl.BlockSpec(shape, lambda i: (0, 0), pipeline_mode=pl.Buffered(1))

    def gate(col):
        return pl.BlockSpec((MERGE_TM, COL_BLK), lambda i: (i, col))

    return pl.pallas_call(
        _merge_kernel,
        out_shape=jax.ShapeDtypeStruct((m, D_MODEL), _F32),
        grid=(m // MERGE_TM,),
        in_specs=[
            pl.BlockSpec((MERGE_TM, D_MODEL), lambda i: (i, 0)),
            pl.BlockSpec((MERGE_TM, D_ATTN), lambda i: (i, 0)),
            pl.BlockSpec((MERGE_TM, D_CONV), lambda i: (i, 0)),
            gate(GA_COL), gate(GA_COL + 1), gate(GC_COL), gate(GC_COL + 1),
            const((D_ATTN, D_MODEL)),
            const((D_CONV, D_MODEL)),
            const((1, D_MODEL)),
            const((D_MODEL, D_MODEL)),
            const((1, D_MODEL)),
        ],
        out_specs=pl.BlockSpec((MERGE_TM, D_MODEL), lambda i: (i, 0)),
        compiler_params=pltpu.CompilerParams(
            dimension_semantics=("arbitrary",),
            vmem_limit_bytes=VMEM_LIMIT_BYTES),
        name="merge",
    )(x2, o, u, proj, proj, proj, proj, w_o_bf, w_pw2_bf, b_pw2, w_out_bf, norm_w)


def _rotary_tables(seq):
    f32 = np.float32
    half = ROT_DIM // 2
    inv_freq = f32(ROPE_THETA) ** (-np.arange(0, ROT_DIM, 2, dtype=f32) / f32(ROT_DIM))
    ang = np.arange(seq, dtype=f32)[:, None] * inv_freq[None, :]
    cos, sin = np.cos(ang).astype(f32), np.sin(ang).astype(f32)
    ones = np.ones((seq, DA_HEAD_DIM - ROT_DIM), f32)
    zeros = np.zeros((seq, DA_HEAD_DIM - ROT_DIM), f32)
    zh = np.zeros((seq, half), f32)
    cos_c = np.concatenate([cos, cos, ones], axis=1)
    sa_c = np.concatenate([-sin, zh, zeros], axis=1)
    sb_c = np.concatenate([zh, sin, zeros], axis=1)
    tile2 = lambda a: np.concatenate([a, a], axis=1)
    return tile2(cos_c), tile2(sa_c), tile2(sb_c)


def kernel(x, norm_pre_w, w_in, lambda_q1, lambda_k1, lambda_q2, lambda_k2, subln_w, w_o_attn, b_glu,
           w_dw, b_dw, ln_g, ln_b, w_pw2, b_pw2, w_out, norm_post_w):
    batch, seq, _ = x.shape
    assert norm_pre_w.shape[0] == 1, "single layer"
    assert seq % PROJ_CONV_TM == 0 and seq % ATTN_TQ == 0 and seq % CONV_T == 0
    x2 = x.reshape(batch * seq, D_MODEL)
    cos_t, sa_t, sb_t = _rotary_tables(seq)
    lam_params = jnp.concatenate([lambda_q1, lambda_k1, lambda_q2, lambda_k2], axis=0).astype(_F32)

    pc, h = _proj_conv(x2, norm_pre_w, w_in[0])
    u, w_rest_bf, w_o_bf, w_pw2_bf, w_out_bf = _conv(pc, b_glu, w_dw[0], b_dw, ln_g, ln_b, w_in[0],
                                                     w_o_attn[0], w_pw2[0], w_out[0], batch, seq)
    pr = _proj_rest(h, w_rest_bf, cos_t, sa_t, sb_t, seq)
    o = _attn(pr, lam_params, subln_w, batch, seq)
    out = _merge(x2, o, u, pr, w_o_bf, w_pw2_bf, b_pw2, w_out_bf, norm_post_w)
    return out.reshape(batch, seq, D_MODEL)
```

```python
import math

import jax
import jax.numpy as jnp
import numpy as np
from jax import lax
from jax.experimental import pallas as pl
from jax.experimental.pallas import tpu as pltpu

D_MODEL = 2048
DA_HEADS = 8
DA_HEAD_DIM = 64
HEAD_W = 2 * DA_HEAD_DIM
D_ATTN = DA_HEADS * HEAD_W
ROPE_THETA = 500000.0
ROT_DIM = DA_HEAD_DIM // 4
D_CONV = 1024
CONV_WIDTH = 31
NORM_EPS = 1e-6
SUBLN_EPS = 1e-5
LN_EPS = 1e-5
LAMBDA_INIT = 0.8 - 0.6 * math.exp(-0.3 * 0)

COL_BLK = 1024
W_CONV_BLKS = (4, 5, 6)
W_REST_BLKS = (0, 1, 2, 3, 7, 8, 9, 10)
GLU_A_COL, GLU_G_COL, ZC_COL = 0, 1, 2
Q_COL, K_COL, V_COL, ZA_COL, GA_COL, GC_COL = 0, 1, 2, 3, 4, 6
HEADS_PER_COL_BLK = COL_BLK // HEAD_W

PROJ_CONV_TM = 512
PROJ_REST_TM = 1024
PROJ_TN = COL_BLK
PROJ_ROWS = 256
Q_SCALE = DA_HEAD_DIM ** -0.5 * math.log2(math.e)
ATTN_KT = 256
ATTN_QCH = 8
ATTN_TQ = ATTN_QCH * ATTN_KT
ATTN_DIAG_W = 256
ATTN_PASS_W = 256
CONV_T = 256
CONV_HALO = 32
CONV_ROWS = 64
MERGE_TM = 256
LANES = 128
SUBLANES = 8
MASK_VALUE = -1e30

VMEM_LIMIT_BYTES = 56 * 1024 * 1024

_BF16 = jnp.bfloat16
_F32 = jnp.float32


def _sigmoid(x):
    return 0.5 * jnp.tanh(0.5 * x) + 0.5


def _rotary(acc, cos, sa, sb):
    half = ROT_DIM // 2
    up = pltpu.roll(acc, LANES - half, 1)
    down = pltpu.roll(acc, half, 1)
    return acc * cos + up * sa + down * sb


def _proj_conv_kernel(x_ref, nw_ref, wf_ref, out_ref, h_ref, w_ref):
    i, j = pl.program_id(0), pl.program_id(1)
    row_chunks = [slice(r, r + PROJ_ROWS) for r in range(0, x_ref.shape[0], PROJ_ROWS)]

    @pl.when(i == 0)
    def _():
        w_ref[j] = wf_ref[...].astype(_BF16)

    def normed(rs):
        x = x_ref[rs, :]
        ms = jnp.mean(x * x, axis=-1, keepdims=True)
        return (x * lax.rsqrt(ms + NORM_EPS) * nw_ref[...]).astype(_BF16)

    @pl.when(j == 0)
    def _():
        for rs in row_chunks:
            h = normed(rs)
            h_ref[rs, :] = h
            out_ref[rs, :] = jnp.dot(h, w_ref[0], preferred_element_type=_F32).astype(_BF16)

    @pl.when(j > 0)
    def _():
        out_ref[...] = jnp.dot(h_ref[...], w_ref[j], preferred_element_type=_F32).astype(_BF16)


def _proj_rest_kernel(h_ref, w_ref, cos_ref, sa_ref, sb_ref, out_ref):
    j = pl.program_id(1)
    row_chunks = [slice(r, r + PROJ_ROWS) for r in range(0, h_ref.shape[0], PROJ_ROWS)]

    def store_rotated(acc, rs, scale):
        cos, sa, sb = cos_ref[rs, :], sa_ref[rs, :], sb_ref[rs, :]
        for hh in range(HEADS_PER_COL_BLK):
            sl = slice(hh * HEAD_W, (hh + 1) * HEAD_W)
            r = _rotary(acc[:, sl], cos, sa, sb)
            if scale != 1.0:
                r = r * scale
            out_ref[rs, sl] = r.astype(_BF16)

    @pl.when(j == Q_COL)
    def _():
        for rs in row_chunks:
            store_rotated(jnp.dot(h_ref[rs, :], w_ref[...], preferred_element_type=_F32), rs, Q_SCALE)

    @pl.when(j == K_COL)
    def _():
        for rs in row_chunks:
            store_rotated(jnp.dot(h_ref[rs, :], w_ref[...], preferred_element_type=_F32), rs, 1.0)

    @pl.when(j > K_COL)
    def _():
        out_ref[...] = jnp.dot(h_ref[...], w_ref[...], preferred_element_type=_F32).astype(_BF16)


_PROJ_PARAMS = pltpu.CompilerParams(dimension_semantics=("arbitrary", "arbitrary"),
                                    vmem_limit_bytes=VMEM_LIMIT_BYTES)


def _proj_conv(x2, norm_w, w_in):
    m = x2.shape[0]
    n_col = len(W_CONV_BLKS)
    tm = PROJ_CONV_TM
    first_blk, last_blk = W_CONV_BLKS[0], W_CONV_BLKS[-1]
    assert W_CONV_BLKS == tuple(range(first_blk, last_blk + 1))
    return pl.pallas_call(
        _proj_conv_kernel,
        out_shape=(jax.ShapeDtypeStruct((m, n_col * PROJ_TN), _BF16),
                   jax.ShapeDtypeStruct((m, D_MODEL), _BF16)),
        grid=(m // tm, n_col),
        in_specs=[
            pl.BlockSpec((tm, D_MODEL), lambda i, j: (i, 0)),
            pl.BlockSpec((1, D_MODEL), lambda i, j: (0, 0)),
            pl.BlockSpec((D_MODEL, PROJ_TN), lambda i, j: (0, jnp.where(i == 0, first_blk + j, last_blk))),
        ],
        out_specs=(pl.BlockSpec((tm, PROJ_TN), lambda i, j: (i, j)),
                   pl.BlockSpec((tm, D_MODEL), lambda i, j: (i, 0))),
        scratch_shapes=[pltpu.VMEM((n_col, D_MODEL, PROJ_TN), _BF16)],
        compiler_params=_PROJ_PARAMS,
        name="proj_conv",
    )(x2, norm_w, w_in)


def _proj_rest(h, w_rest_bf, cos_t, sa_t, sb_t, seq):
    m = h.shape[0]
    n_col = w_rest_bf.shape[1] // PROJ_TN
    tm = PROJ_REST_TM
    assert seq % tm == 0
    seq_blocks = seq // tm
    tab_spec = pl.BlockSpec((tm, HEAD_W), lambda i, j: (i % seq_blocks, 0))
    return pl.pallas_call(
        _proj_rest_kernel,
        out_shape=jax.ShapeDtypeStruct((m, n_col * PROJ_TN), _BF16),
        grid=(m // tm, n_col),
        in_specs=[
            pl.BlockSpec((tm, D_MODEL), lambda i, j: (i, 0)),
            pl.BlockSpec((D_MODEL, PROJ_TN), lambda i, j: (0, j)),
            tab_spec, tab_spec, tab_spec,
        ],
        out_specs=pl.BlockSpec((tm, PROJ_TN), lambda i, j: (i, j)),
        compiler_params=_PROJ_PARAMS,
        name="proj_rest",
    )(h, w_rest_bf, cos_t, sa_t, sb_t)


def _attn_kernel(q_ref, k_ref, v_ref, z_ref, lam_ref, sw_ref, o_ref,
                 qqt_ref, vt_ref, s_ref, m_ref, l_ref, acc_ref):
    tq, kt = ATTN_TQ, ATTN_KT
    nchunk = 2 * ATTN_QCH
    seq = q_ref.shape[0]

    for j in range(vt_ref.shape[0]):
        vt_ref[j] = v_ref[j * kt:(j + 1) * kt, :].astype(_F32).T.astype(_BF16)

    lam_p = lam_ref[...]
    s1 = jnp.sum(lam_p[0:1] * lam_p[1:2], axis=1, keepdims=True)
    s2 = jnp.sum(lam_p[2:3] * lam_p[3:4], axis=1, keepdims=True)
    lam = jnp.exp(s1) - jnp.exp(s2) + LAMBDA_INIT

    def scores(j, slot, col_ranges):
        k_sub = k_ref[j * kt:(j + 1) * kt, :]
        for c0, w in col_ranges:
            s_ref[slot, :, c0:c0 + w] = jnp.dot(k_sub, qqt_ref[:, c0:c0 + w], preferred_element_type=_F32)

    def softmax_pv(j, slot, col_ranges):
        for c0, w, diag in col_ranges:
            cs = slice(c0, c0 + w)
            rows = kt if diag is None else diag + w
            s = s_ref[slot, 0:rows, cs]
            if diag is not None:
                kv = lax.broadcasted_iota(jnp.int32, (rows, w), 0)
                qq = lax.broadcasted_iota(jnp.int32, (rows, w), 1) + diag
                s = jnp.where(kv <= qq, s, MASK_VALUE)
            m_prev = m_ref[:, cs]
            m_new = jnp.maximum(m_prev, jnp.max(s, axis=0, keepdims=True))
            alpha = jnp.exp2(m_prev - m_new)
            p = jnp.exp2(s - m_new)
            l_ref[:, cs] = alpha * l_ref[:, cs] + jnp.sum(p, axis=0, keepdims=True)
            acc_ref[:, cs] = alpha * acc_ref[:, cs] + jnp.dot(vt_ref[j, :, 0:rows], p.astype(_BF16),
                                                              preferred_element_type=_F32)
            m_ref[:, cs] = m_new

    all_cols = [(0, 2 * tq)]
    full = [(c0, ATTN_PASS_W, None) for c0 in range(0, 2 * tq, ATTN_PASS_W)]

    def diag_ranges(d):
        out = []
        for c in range(nchunk):
            if c % ATTN_QCH > d:
                out.append((c * kt, kt, None))
            elif c % ATTN_QCH == d:
                out.extend((c * kt + off, ATTN_DIAG_W, off) for off in range(0, kt, ATTN_DIAG_W))
        return out

    def query_tile(qi):
        rows = slice(qi * tq, (qi + 1) * tq)

        qt = q_ref[rows, :].astype(_F32).T
        row = lax.broadcasted_iota(jnp.int32, (HEAD_W, tq), 0)
        qqt_ref[:, 0:tq] = jnp.where(row < DA_HEAD_DIM, qt, 0.0).astype(_BF16)
        qqt_ref[:, tq:2 * tq] = jnp.where(row >= DA_HEAD_DIM, qt, 0.0).astype(_BF16)
        m_ref[...] = jnp.full(m_ref.shape, MASK_VALUE, _F32)
        l_ref[...] = jnp.zeros(l_ref.shape, _F32)
        acc_ref[...] = jnp.zeros(acc_ref.shape, _F32)

        scores(0, 0, all_cols)

        def block_pair(i):
            j = 2 * i
            for rng in full:
                scores(j + 1, 1, [rng[:2]])
                softmax_pv(j, 0, [rng])
            for rng in full:
                scores(j + 2, 0, [rng[:2]])
                softmax_pv(j + 1, 1, [rng])

        nfull = ATTN_QCH * qi
        for i in range(nfull // 2):
            block_pair(i)
        for d in range(ATTN_QCH):
            nxt = {c * kt for c in range(nchunk) if c % ATTN_QCH >= d + 1} if d + 1 < ATTN_QCH else set()
            for rng in diag_ranges(d):
                if rng[0] in nxt:
                    scores(nfull + d + 1, (d + 1) % 2, [(rng[0], kt)])
                softmax_pv(nfull + d, d % 2, [rng])

        ot = acc_ref[...] / l_ref[...]
        ot = ot[:, 0:tq] - lam * ot[:, tq:2 * tq]
        ot = ot * lax.rsqrt(jnp.mean(ot * ot, axis=0, keepdims=True) + SUBLN_EPS)
        o = ot.T * sw_ref[...] * (1.0 - LAMBDA_INIT)
        z = z_ref[rows, :].astype(_F32)
        o_ref[rows, :] = (o * (z * _sigmoid(z))).astype(_BF16)

    for qi in range(seq // tq):
        query_tile(qi)


def _attn(pr, lam_params, subln_w, batch, seq):
    m = pr.shape[0]
    assert ATTN_TQ == ATTN_QCH * ATTN_KT and seq % ATTN_TQ == 0 and ATTN_QCH % 2 == 0
    hb = HEADS_PER_COL_BLK

    def head_cols(col):
        return pl.BlockSpec((seq, HEAD_W), lambda b, h: (b, col * hb + h))

    return pl.pallas_call(
        _attn_kernel,
        out_shape=jax.ShapeDtypeStruct((m, D_ATTN), _BF16),
        grid=(batch, DA_HEADS),
        in_specs=[
            head_cols(Q_COL), head_cols(K_COL), head_cols(V_COL), head_cols(ZA_COL),
            pl.BlockSpec((4, DA_HEAD_DIM), lambda b, h: (0, 0)),
            pl.BlockSpec((1, HEAD_W), lambda b, h: (0, 0)),
        ],
        out_specs=pl.BlockSpec((seq, HEAD_W), lambda b, h: (b, h)),
        scratch_shapes=[
            pltpu.VMEM((HEAD_W, 2 * ATTN_TQ), _BF16),
            pltpu.VMEM((seq // ATTN_KT, HEAD_W, ATTN_KT), _BF16),
            pltpu.VMEM((2, ATTN_KT, 2 * ATTN_TQ), _F32),
            pltpu.VMEM((1, 2 * ATTN_TQ), _F32),
            pltpu.VMEM((1, 2 * ATTN_TQ), _F32),
            pltpu.VMEM((HEAD_W, 2 * ATTN_TQ), _F32),
        ],
        compiler_params=pltpu.CompilerParams(
            dimension_semantics=("arbitrary", "arbitrary"),
            vmem_limit_bytes=VMEM_LIMIT_BYTES),
        name="attn",
    )(pr, pr, pr, pr, lam_params, subln_w)


def _conv_kernel(a_ref, g_ref, ah_ref, gh_ref, z_ref, bglu_ref, wdw_ref, bdw_ref, lng_ref, lnb_ref, *refs):
    n_rest = len(W_REST_BLKS)
    w_in_refs, (wo_ref, wp_ref, wout_ref) = refs[:n_rest], refs[n_rest:n_rest + 3]
    o_ref, wrest_bf_ref, wo_bf_ref, wp_bf_ref, wout_bf_ref, stage_ref, shift_ref, y_ref = refs[n_rest + 3:]
    for k, w_ref in enumerate(w_in_refs):
        wrest_bf_ref[:, k * COL_BLK:(k + 1) * COL_BLK] = w_ref[...].astype(_BF16)
    wo_bf_ref[...] = wo_ref[...].astype(_BF16)
    wp_bf_ref[...] = wp_ref[...].astype(_BF16)
    wout_bf_ref[...] = wout_ref[...].astype(_BF16)

    t = pl.program_id(1)
    ba = bglu_ref[:, 0:D_CONV]
    bg = bglu_ref[:, D_CONV:2 * D_CONV]

    def glu(a, g):
        return (a.astype(_F32) + ba) * _sigmoid(g.astype(_F32) + bg)

    halo = glu(ah_ref[...], gh_ref[...])
    halo = jnp.where(t > 0, halo, jnp.zeros_like(halo))
    cur = glu(a_ref[...], g_ref[...])
    nblk = D_CONV // LANES
    for c in range(nblk):
        ls = slice(c * LANES, (c + 1) * LANES)
        stage_ref[c, 0:CONV_HALO, :] = halo[:, ls]
        stage_ref[c, CONV_HALO:CONV_HALO + CONV_T, :] = cur[:, ls]

    first = CONV_HALO - (CONV_WIDTH - 1)
    shift_rows = shift_ref.shape[1]
    row_starts = range(0, CONV_T, CONV_ROWS)

    def channel_block(c, carry):
        for s in range(1, SUBLANES):
            shift_ref[s - 1] = stage_ref[c, s:s + shift_rows, :]
        accs = [jnp.broadcast_to(bdw_ref[c], (CONV_ROWS, LANES)) for _ in row_starts]
        for j in range(CONV_WIDTH):
            q, s = divmod(first + j, SUBLANES)
            w_j = jnp.broadcast_to(wdw_ref[c, j:j + 1, :], (CONV_ROWS, LANES))
            for k, r in enumerate(row_starts):
                lo = r + q * SUBLANES
                if s == 0:
                    win = stage_ref[c, lo:lo + CONV_ROWS, :]
                else:
                    win = shift_ref[s - 1, lo:lo + CONV_ROWS, :]
                accs[k] = accs[k] + win * w_j
        for k, r in enumerate(row_starts):
            y_ref[c, r:r + CONV_ROWS, :] = accs[k]
        return carry

    lax.fori_loop(0, nblk, channel_block, 0)

    y = jnp.concatenate([y_ref[c] for c in range(nblk)], axis=1)
    mu = jnp.mean(y, axis=-1, keepdims=True)
    yc = y - mu
    var = jnp.mean(yc * yc, axis=-1, keepdims=True)
    u = yc * lax.rsqrt(var + LN_EPS) * lng_ref[...] + lnb_ref[...]
    z = z_ref[...].astype(_F32)
    o_ref[...] = ((u * _sigmoid(u)) * (z * _sigmoid(z))).astype(_BF16)


def _conv(pc, b_glu, w_dw, b_dw, ln_g, ln_b, w_in, w_o, w_pw2, w_out, batch, seq):
    m = pc.shape[0]
    nt = seq // CONV_T
    steps = batch * nt
    halo_per_tile = CONV_T // CONV_HALO
    nblk = D_CONV // LANES
    w_blk = w_dw.reshape(CONV_WIDTH, nblk, LANES).transpose(1, 0, 2)
    b_blk = b_dw.reshape(nblk, 1, LANES)

    def halo_idx(col):
        return lambda b, t: (jnp.maximum((b * nt + t) * halo_per_tile - 1, 0), col * (COL_BLK // D_CONV))

    def vec(n):
        return pl.BlockSpec((1, n), lambda b, t: (0, 0))

    def row_slab(n_rows, n_cols, col_blk=0):
        rows = n_rows // steps
        assert rows * steps == n_rows and rows % (2 * SUBLANES) == 0
        return pl.BlockSpec((rows, n_cols), lambda b, t: (b * nt + t, col_blk))

    n_rest = len(W_REST_BLKS)
    small = (w_o, w_pw2, w_out)
    return pl.pallas_call(
        _conv_kernel,
        out_shape=(jax.ShapeDtypeStruct((m, D_CONV), _BF16),
                   jax.ShapeDtypeStruct((D_MODEL, n_rest * COL_BLK), _BF16),
                   *[jax.ShapeDtypeStruct(w.shape, _BF16) for w in small]),
        grid=(batch, nt),
        in_specs=[
            pl.BlockSpec((CONV_T, D_CONV), lambda b, t: (b * nt + t, GLU_A_COL)),
            pl.BlockSpec((CONV_T, D_CONV), lambda b, t: (b * nt + t, GLU_G_COL)),
            pl.BlockSpec((CONV_HALO, D_CONV), halo_idx(GLU_A_COL)),
            pl.BlockSpec((CONV_HALO, D_CONV), halo_idx(GLU_G_COL)),
            pl.BlockSpec((CONV_T, D_CONV), lambda b, t: (b * nt + t, ZC_COL)),
            vec(2 * D_CONV),
            pl.BlockSpec((nblk, CONV_WIDTH, LANES), lambda b, t: (0, 0, 0)),
            pl.BlockSpec((nblk, 1, LANES), lambda b, t: (0, 0, 0)),
            vec(D_CONV), vec(D_CONV),
            *[row_slab(D_MODEL, COL_BLK, blk) for blk in W_REST_BLKS],
            *[row_slab(*w.shape) for w in small],
        ],
        out_specs=(pl.BlockSpec((CONV_T, D_CONV), lambda b, t: (b * nt + t, 0)),
                   row_slab(D_MODEL, n_rest * COL_BLK),
                   *[row_slab(*w.shape) for w in small]),
        scratch_shapes=[
            pltpu.VMEM((nblk, CONV_HALO + CONV_T, LANES), _F32),
            pltpu.VMEM((SUBLANES - 1, CONV_HALO + CONV_T - SUBLANES, LANES), _F32),
            pltpu.VMEM((nblk, CONV_T, LANES), _F32),
        ],
        compiler_params=pltpu.CompilerParams(
            dimension_semantics=("arbitrary", "arbitrary"),
            vmem_limit_bytes=VMEM_LIMIT_BYTES),
        name="conv",
    )(pc, pc, pc, pc, pc, b_glu, w_blk, b_blk, ln_g, ln_b, *([w_in] * n_rest), *small)


def _merge_kernel(x_ref, o_ref, u_ref, ga0_ref, ga1_ref, gc0_ref, gc1_ref, wo_ref, wp_ref, bp_ref,
                  wout_ref, nw_ref, out_ref):
    ya = jnp.dot(o_ref[...], wo_ref[...], preferred_element_type=_F32)
    yc = jnp.dot(u_ref[...], wp_ref[...], preferred_element_type=_F32) + bp_ref[...]
    ga = jnp.concatenate([ga0_ref[...], ga1_ref[...]], axis=1).astype(_F32)
    gc = jnp.concatenate([gc0_ref[...], gc1_ref[...]], axis=1).astype(_F32)
    mix = _sigmoid(ga) * ya + _sigmoid(gc) * yc
    out = jnp.dot(mix.astype(_BF16), wout_ref[...], preferred_element_type=_F32)
    ms = jnp.mean(out * out, axis=-1, keepdims=True)
    out_ref[...] = x_ref[...] + out * lax.rsqrt(ms + NORM_EPS) * nw_ref[...]


def _merge(x2, o, u, proj, w_o_bf, w_pw2_bf, b_pw2, w_out_bf, norm_w):
    m = x2.shape[0]

    def const(shape):
        return pl.BlockSpec(shape, lambda i: (0, 0), pipeline_mode=pl.Buffered(1))

    def gate(col):
        return pl.BlockSpec((MERGE_TM, COL_BLK), lambda i: (i, col))

    return pl.pallas_call(
        _merge_kernel,
        out_shape=jax.ShapeDtypeStruct((m, D_MODEL), _F32),
        grid=(m // MERGE_TM,),
        in_specs=[
            pl.BlockSpec((MERGE_TM, D_MODEL), lambda i: (i, 0)),
            pl.BlockSpec((MERGE_TM, D_ATTN), lambda i: (i, 0)),
            pl.BlockSpec((MERGE_TM, D_CONV), lambda i: (i, 0)),
            gate(GA_COL), gate(GA_COL + 1), gate(GC_COL), gate(GC_COL + 1),
            const((D_ATTN, D_MODEL)),
            const((D_CONV, D_MODEL)),
            const((1, D_MODEL)),
            const((D_MODEL, D_MODEL)),
            const((1, D_MODEL)),
        ],
        out_specs=pl.BlockSpec((MERGE_TM, D_MODEL), lambda i: (i, 0)),
        compiler_params=pltpu.CompilerParams(
            dimension_semantics=("arbitrary",),
            vmem_limit_bytes=VMEM_LIMIT_BYTES),
        name="merge",
    )(x2, o, u, proj, proj, proj, proj, w_o_bf, w_pw2_bf, b_pw2, w_out_bf, norm_w)


def _rotary_tables(seq):
    f32 = np.float32
    half = ROT_DIM // 2
    inv_freq = f32(ROPE_THETA) ** (-np.arange(0, ROT_DIM, 2, dtype=f32) / f32(ROT_DIM))
    ang = np.arange(seq, dtype=f32)[:, None] * inv_freq[None, :]
    cos, sin = np.cos(ang).astype(f32), np.sin(ang).astype(f32)
    ones = np.ones((seq, DA_HEAD_DIM - ROT_DIM), f32)
    zeros = np.zeros((seq, DA_HEAD_DIM - ROT_DIM), f32)
    zh = np.zeros((seq, half), f32)
    cos_c = np.concatenate([cos, cos, ones], axis=1)
    sa_c = np.concatenate([-sin, zh, zeros], axis=1)
    sb_c = np.concatenate([zh, sin, zeros], axis=1)
    tile2 = lambda a: np.concatenate([a, a], axis=1)
    return tile2(cos_c), tile2(sa_c), tile2(sb_c)


def kernel(x, norm_pre_w, w_in, lambda_q1, lambda_k1, lambda_q2, lambda_k2, subln_w, w_o_attn, b_glu,
           w_dw, b_dw, ln_g, ln_b, w_pw2, b_pw2, w_out, norm_post_w):
    batch, seq, _ = x.shape
    assert norm_pre_w.shape[0] == 1, "single layer"
    assert seq % PROJ_CONV_TM == 0 and seq % ATTN_TQ == 0 and seq % CONV_T == 0
    x2 = x.reshape(batch * seq, D_MODEL)
    cos_t, sa_t, sb_t = _rotary_tables(seq)
    lam_params = jnp.concatenate([lambda_q1, lambda_k1, lambda_q2, lambda_k2], axis=0).astype(_F32)

    pc, h = _proj_conv(x2, norm_pre_w, w_in[0])
    u, w_rest_bf, w_o_bf, w_pw2_bf, w_out_bf = _conv(pc, b_glu, w_dw[0], b_dw, ln_g, ln_b, w_in[0],
                                                     w_o_attn[0], w_pw2[0], w_out[0], batch, seq)
    pr = _proj_rest(h, w_rest_bf, cos_t, sa_t, sb_t, seq)
    o = _attn(pr, lam_params, subln_w, batch, seq)
    out = _merge(x2, o, u, pr, w_o_bf, w_pw2_bf, b_pw2, w_out_bf, norm_post_w)
    return out.reshape(batch, seq, D_MODEL)
```

```python
import math

import jax
import jax.numpy as jnp
import numpy as np
from jax import lax
from jax.experimental import pallas as pl
from jax.experimental.pallas import tpu as pltpu

D_MODEL = 2048
DA_HEADS = 8
DA_HEAD_DIM = 64
HEAD_W = 2 * DA_HEAD_DIM
D_ATTN = DA_HEADS * HEAD_W
ROPE_THETA = 500000.0
ROT_DIM = DA_HEAD_DIM // 4
D_CONV = 1024
CONV_WIDTH = 31
NORM_EPS = 1e-6
SUBLN_EPS = 1e-5
LN_EPS = 1e-5
LAMBDA_INIT = 0.8 - 0.6 * math.exp(-0.3 * 0)

COL_BLK = 1024
W_CONV_BLKS = (4, 5, 6)
W_REST_BLKS = (0, 1, 2, 3, 7, 8, 9, 10)
GLU_A_COL, GLU_G_COL, ZC_COL = 0, 1, 2
Q_COL, K_COL, V_COL, ZA_COL, GA_COL, GC_COL = 0, 1, 2, 3, 4, 6
HEADS_PER_COL_BLK = COL_BLK // HEAD_W

PROJ_CONV_TM = 1024
PROJ_REST_TM = 1024
PROJ_TN = COL_BLK
PROJ_ROWS = 256
Q_SCALE = DA_HEAD_DIM ** -0.5 * math.log2(math.e)
ATTN_KT = 256
ATTN_QCH = 8
ATTN_TQ = ATTN_QCH * ATTN_KT
ATTN_DIAG_W = 256
ATTN_PASS_W = 256
CONV_T = 256
CONV_HALO = 32
CONV_ROWS = 64
MERGE_TM = 256
LANES = 128
SUBLANES = 8
MASK_VALUE = -1e30

VMEM_LIMIT_BYTES = 56 * 1024 * 1024

_BF16 = jnp.bfloat16
_F32 = jnp.float32


def _sigmoid(x):
    return 0.5 * jnp.tanh(0.5 * x) + 0.5


def _rotary(acc, cos, sa, sb):
    half = ROT_DIM // 2
    up = pltpu.roll(acc, LANES - half, 1)
    down = pltpu.roll(acc, half, 1)
    return acc * cos + up * sa + down * sb


def _proj_conv_kernel(x_ref, nw_ref, wf_ref, out_ref, h_ref, w_ref):
    i, j = pl.program_id(0), pl.program_id(1)
    row_chunks = [slice(r, r + PROJ_ROWS) for r in range(0, x_ref.shape[0], PROJ_ROWS)]

    @pl.when(i == 0)
    def _():
        w_ref[j] = wf_ref[...].astype(_BF16)

    def normed(rs):
        x = x_ref[rs, :]
        ms = jnp.mean(x * x, axis=-1, keepdims=True)
        return (x * lax.rsqrt(ms + NORM_EPS) * nw_ref[...]).astype(_BF16)

    @pl.when(j == 0)
    def _():
        for rs in row_chunks:
            h = normed(rs)
            h_ref[rs, :] = h
            out_ref[rs, :] = jnp.dot(h, w_ref[0], preferred_element_type=_F32).astype(_BF16)

    @pl.when(j > 0)
    def _():
        out_ref[...] = jnp.dot(h_ref[...], w_ref[j], preferred_element_type=_F32).astype(_BF16)


def _proj_rest_kernel(h_ref, w_ref, cos_ref, sa_ref, sb_ref, out_ref):
    j = pl.program_id(1)
    row_chunks = [slice(r, r + PROJ_ROWS) for r in range(0, h_ref.shape[0], PROJ_ROWS)]

    def store_rotated(acc, rs, scale):
        cos, sa, sb = cos_ref[rs, :], sa_ref[rs, :], sb_ref[rs, :]
        for hh in range(HEADS_PER_COL_BLK):
            sl = slice(hh * HEAD_W, (hh + 1) * HEAD_W)
            r = _rotary(acc[:, sl], cos, sa, sb)
            if scale != 1.0:
                r = r * scale
            out_ref[rs, sl] = r.astype(_BF16)

    @pl.when(j == Q_COL)
    def _():
        for rs in row_chunks:
            store_rotated(jnp.dot(h_ref[rs, :], w_ref[...], preferred_element_type=_F32), rs, Q_SCALE)

    @pl.when(j == K_COL)
    def _():
        for rs in row_chunks:
            store_rotated(jnp.dot(h_ref[rs, :], w_ref[...], preferred_element_type=_F32), rs, 1.0)

    @pl.when(j > K_COL)
    def _():
        out_ref[...] = jnp.dot(h_ref[...], w_ref[...], preferred_element_type=_F32).astype(_BF16)


_PROJ_PARAMS = pltpu.CompilerParams(dimension_semantics=("arbitrary", "arbitrary"),
                                    vmem_limit_bytes=VMEM_LIMIT_BYTES)


def _proj_conv(x2, norm_w, w_in):
    m = x2.shape[0]
    n_col = len(W_CONV_BLKS)
    tm = PROJ_CONV_TM
    first_blk, last_blk = W_CONV_BLKS[0], W_CONV_BLKS[-1]
    assert W_CONV_BLKS == tuple(range(first_blk, last_blk + 1))
    return pl.pallas_call(
        _proj_conv_kernel,
        out_shape=(jax.ShapeDtypeStruct((m, n_col * PROJ_TN), _BF16),
                   jax.ShapeDtypeStruct((m, D_MODEL), _BF16)),
        grid=(m // tm, n_col),
        in_specs=[
            pl.BlockSpec((tm, D_MODEL), lambda i, j: (i, 0)),
            pl.BlockSpec((1, D_MODEL), lambda i, j: (0, 0)),
            pl.BlockSpec((D_MODEL, PROJ_TN), lambda i, j: (0, jnp.where(i == 0, first_blk + j, last_blk)),
                         pipeline_mode=pl.Buffered(1)),
        ],
        out_specs=(pl.BlockSpec((tm, PROJ_TN), lambda i, j: (i, j)),
                   pl.BlockSpec((tm, D_MODEL), lambda i, j: (i, 0))),
        scratch_shapes=[pltpu.VMEM((n_col, D_MODEL, PROJ_TN), _BF16)],
        compiler_params=_PROJ_PARAMS,
        name="proj_conv",
    )(x2, norm_w, w_in)


def _proj_rest(h, w_rest_bf, cos_t, sa_t, sb_t, seq):
    m = h.shape[0]
    n_col = w_rest_bf.shape[1] // PROJ_TN
    tm = PROJ_REST_TM
    assert seq % tm == 0
    seq_blocks = seq // tm
    tab_spec = pl.BlockSpec((tm, HEAD_W), lambda i, j: (i % seq_blocks, 0))
    return pl.pallas_call(
        _proj_rest_kernel,
        out_shape=jax.ShapeDtypeStruct((m, n_col * PROJ_TN), _BF16),
        grid=(m // tm, n_col),
        in_specs=[
            pl.BlockSpec((tm, D_MODEL), lambda i, j: (i, 0)),
            pl.BlockSpec((D_MODEL, PROJ_TN), lambda i, j: (0, j)),
            tab_spec, tab_spec, tab_spec,
        ],
        out_specs=pl.BlockSpec((tm, PROJ_TN), lambda i, j: (i, j)),
        compiler_params=_PROJ_PARAMS,
        name="proj_rest",
    )(h, w_rest_bf, cos_t, sa_t, sb_t)


def _attn_kernel(q_ref, k_ref, v_ref, z_ref, lam_ref, sw_ref, o_ref,
                 qqt_ref, vt_ref, s_ref, m_ref, l_ref, acc_ref):
    tq, kt = ATTN_TQ, ATTN_KT
    nchunk = 2 * ATTN_QCH
    seq = q_ref.shape[0]

    for j in range(vt_ref.shape[0]):
        vt_ref[j] = v_ref[j * kt:(j + 1) * kt, :].astype(_F32).T.astype(_BF16)

    lam_p = lam_ref[...]
    s1 = jnp.sum(lam_p[0:1] * lam_p[1:2], axis=1, keepdims=True)
    s2 = jnp.sum(lam_p[2:3] * lam_p[3:4], axis=1, keepdims=True)
    lam = jnp.exp(s1) - jnp.exp(s2) + LAMBDA_INIT

    def scores(j, slot, col_ranges):
        k_sub = k_ref[j * kt:(j + 1) * kt, :]
        for c0, w in col_ranges:
            s_ref[slot, :, c0:c0 + w] = jnp.dot(k_sub, qqt_ref[:, c0:c0 + w], preferred_element_type=_F32)

    def softmax_pv(j, slot, col_ranges):
        for c0, w, diag in col_ranges:
            cs = slice(c0, c0 + w)
            rows = kt if diag is None else diag + w
            s = s_ref[slot, 0:rows, cs]
            if diag is not None:
                kv = lax.broadcasted_iota(jnp.int32, (rows, w), 0)
                qq = lax.broadcasted_iota(jnp.int32, (rows, w), 1) + diag
                s = jnp.where(kv <= qq, s, MASK_VALUE)
            m_prev = m_ref[:, cs]
            m_new = jnp.maximum(m_prev, jnp.max(s, axis=0, keepdims=True))
            alpha = jnp.exp2(m_prev - m_new)
            p = jnp.exp2(s - m_new)
            l_ref[:, cs] = alpha * l_ref[:, cs] + jnp.sum(p, axis=0, keepdims=True)
            acc_ref[:, cs] = alpha * acc_ref[:, cs] + jnp.dot(vt_ref[j, :, 0:rows], p.astype(_BF16),
                                                              preferred_element_type=_F32)
            m_ref[:, cs] = m_new

    all_cols = [(0, 2 * tq)]
    full = [(c0, ATTN_PASS_W, None) for c0 in range(0, 2 * tq, ATTN_PASS_W)]

    def diag_ranges(d):
        out = []
        for c in range(nchunk):
            if c % ATTN_QCH > d:
                out.append((c * kt, kt, None))
            elif c % ATTN_QCH == d:
                out.extend((c * kt + off, ATTN_DIAG_W, off) for off in range(0, kt, ATTN_DIAG_W))
        return out

    def query_tile(qi):
        rows = slice(qi * tq, (qi + 1) * tq)

        qt = q_ref[rows, :].astype(_F32).T
        row = lax.broadcasted_iota(jnp.int32, (HEAD_W, tq), 0)
        qqt_ref[:, 0:tq] = jnp.where(row < DA_HEAD_DIM, qt, 0.0).astype(_BF16)
        qqt_ref[:, tq:2 * tq] = jnp.where(row >= DA_HEAD_DIM, qt, 0.0).astype(_BF16)
        m_ref[...] = jnp.full(m_ref.shape, MASK_VALUE, _F32)
        l_ref[...] = jnp.zeros(l_ref.shape, _F32)
        acc_ref[...] = jnp.zeros(acc_ref.shape, _F32)

        scores(0, 0, all_cols)

        def block_pair(i):
            j = 2 * i
            for rng in full:
                scores(j + 1, 1, [rng[:2]])
                softmax_pv(j, 0, [rng])
            for rng in full:
                scores(j + 2, 0, [rng[:2]])
                softmax_pv(j + 1, 1, [rng])

        nfull = ATTN_QCH * qi
        for i in range(nfull // 2):
            block_pair(i)
        for d in range(ATTN_QCH):
            nxt = {c * kt for c in range(nchunk) if c % ATTN_QCH >= d + 1} if d + 1 < ATTN_QCH else set()
            for rng in diag_ranges(d):
                if rng[0] in nxt:
                    scores(nfull + d + 1, (d + 1) % 2, [(rng[0], kt)])
                softmax_pv(nfull + d, d % 2, [rng])

        ot = acc_ref[...] / l_ref[...]
        ot = ot[:, 0:tq] - lam * ot[:, tq:2 * tq]
        ot = ot * lax.rsqrt(jnp.mean(ot * ot, axis=0, keepdims=True) + SUBLN_EPS)
        o = ot.T * sw_ref[...] * (1.0 - LAMBDA_INIT)
        z = z_ref[rows, :].astype(_F32)
        o_ref[rows, :] = (o * (z * _sigmoid(z))).astype(_BF16)

    for qi in range(seq // tq):
        query_tile(qi)


def _attn(pr, lam_params, subln_w, batch, seq):
    m = pr.shape[0]
    assert ATTN_TQ == ATTN_QCH * ATTN_KT and seq % ATTN_TQ == 0 and ATTN_QCH % 2 == 0
    hb = HEADS_PER_COL_BLK

    def head_cols(col):
        return pl.BlockSpec((seq, HEAD_W), lambda b, h: (b, col * hb + h))

    return pl.pallas_call(
        _attn_kernel,
        out_shape=jax.ShapeDtypeStruct((m, D_ATTN), _BF16),
        grid=(batch, DA_HEADS),
        in_specs=[
            head_cols(Q_COL), head_cols(K_COL), head_cols(V_COL), head_cols(ZA_COL),
            pl.BlockSpec((4, DA_HEAD_DIM), lambda b, h: (0, 0)),
            pl.BlockSpec((1, HEAD_W), lambda b, h: (0, 0)),
        ],
        out_specs=pl.BlockSpec((seq, HEAD_W), lambda b, h: (b, h)),
        scratch_shapes=[
            pltpu.VMEM((HEAD_W, 2 * ATTN_TQ), _BF16),
            pltpu.VMEM((seq // ATTN_KT, HEAD_W, ATTN_KT), _BF16),
            pltpu.VMEM((2, ATTN_KT, 2 * ATTN_TQ), _F32),
            pltpu.VMEM((1, 2 * ATTN_TQ), _F32),
            pltpu.VMEM((1, 2 * ATTN_TQ), _F32),
            pltpu.VMEM((HEAD_W, 2 * ATTN_TQ), _F32),
        ],
        compiler_params=pltpu.CompilerParams(
            dimension_semantics=("arbitrary", "arbitrary"),
            vmem_limit_bytes=VMEM_LIMIT_BYTES),
        name="attn",
    )(pr, pr, pr, pr, lam_params, subln_w)


def _conv_kernel(a_ref, g_ref, ah_ref, gh_ref, z_ref, bglu_ref, wdw_ref, bdw_ref, lng_ref, lnb_ref, *refs):
    n_rest = len(W_REST_BLKS)
    w_in_refs, (wo_ref, wp_ref, wout_ref) = refs[:n_rest], refs[n_rest:n_rest + 3]
    o_ref, wrest_bf_ref, wo_bf_ref, wp_bf_ref, wout_bf_ref, stage_ref, shift_ref, y_ref = refs[n_rest + 3:]
    for k, w_ref in enumerate(w_in_refs):
        wrest_bf_ref[:, k * COL_BLK:(k + 1) * COL_BLK] = w_ref[...].astype(_BF16)
    wo_bf_ref[...] = wo_ref[...].astype(_BF16)
    wp_bf_ref[...] = wp_ref[...].astype(_BF16)
    wout_bf_ref[...] = wout_ref[...].astype(_BF16)

    t = pl.program_id(1)
    ba = bglu_ref[:, 0:D_CONV]
    bg = bglu_ref[:, D_CONV:2 * D_CONV]

    def glu(a, g):
        return (a.astype(_F32) + ba) * _sigmoid(g.astype(_F32) + bg)

    halo = glu(ah_ref[...], gh_ref[...])
    halo = jnp.where(t > 0, halo, jnp.zeros_like(halo))
    cur = glu(a_ref[...], g_ref[...])
    nblk = D_CONV // LANES
    for c in range(nblk):
        ls = slice(c * LANES, (c + 1) * LANES)
        stage_ref[c, 0:CONV_HALO, :] = halo[:, ls]
        stage_ref[c, CONV_HALO:CONV_HALO + CONV_T, :] = cur[:, ls]

    first = CONV_HALO - (CONV_WIDTH - 1)
    shift_rows = shift_ref.shape[1]
    row_starts = range(0, CONV_T, CONV_ROWS)

    def channel_block(c, carry):
        for s in range(1, SUBLANES):
            shift_ref[s - 1] = stage_ref[c, s:s + shift_rows, :]
        accs = [jnp.broadcast_to(bdw_ref[c], (CONV_ROWS, LANES)) for _ in row_starts]
        for j in range(CONV_WIDTH):
            q, s = divmod(first + j, SUBLANES)
            w_j = jnp.broadcast_to(wdw_ref[c, j:j + 1, :], (CONV_ROWS, LANES))
            for k, r in enumerate(row_starts):
                lo = r + q * SUBLANES
                if s == 0:
                    win = stage_ref[c, lo:lo + CONV_ROWS, :]
                else:
                    win = shift_ref[s - 1, lo:lo + CONV_ROWS, :]
                accs[k] = accs[k] + win * w_j
        for k, r in enumerate(row_starts):
            y_ref[c, r:r + CONV_ROWS, :] = accs[k]
        return carry

    lax.fori_loop(0, nblk, channel_block, 0)

    y = jnp.concatenate([y_ref[c] for c in range(nblk)], axis=1)
    mu = jnp.mean(y, axis=-1, keepdims=True)
    yc = y - mu
    var = jnp.mean(yc * yc, axis=-1, keepdims=True)
    u = yc * lax.rsqrt(var + LN_EPS) * lng_ref[...] + lnb_ref[...]
    z = z_ref[...].astype(_F32)
    o_ref[...] = ((u * _sigmoid(u)) * (z * _sigmoid(z))).astype(_BF16)


def _conv(pc, b_glu, w_dw, b_dw, ln_g, ln_b, w_in, w_o, w_pw2, w_out, batch, seq):
    m = pc.shape[0]
    nt = seq // CONV_T
    steps = batch * nt
    halo_per_tile = CONV_T // CONV_HALO
    nblk = D_CONV // LANES
    w_blk = w_dw.reshape(CONV_WIDTH, nblk, LANES).transpose(1, 0, 2)
    b_blk = b_dw.reshape(nblk, 1, LANES)

    def halo_idx(col):
        return lambda b, t: (jnp.maximum((b * nt + t) * halo_per_tile - 1, 0), col * (COL_BLK // D_CONV))

    def vec(n):
        return pl.BlockSpec((1, n), lambda b, t: (0, 0))

    def row_slab(n_rows, n_cols, col_blk=0):
        rows = n_rows // steps
        assert rows * steps == n_rows and rows % (2 * SUBLANES) == 0
        return pl.BlockSpec((rows, n_cols), lambda b, t: (b * nt + t, col_blk))

    n_rest = len(W_REST_BLKS)
    small = (w_o, w_pw2, w_out)
    return pl.pallas_call(
        _conv_kernel,
        out_shape=(jax.ShapeDtypeStruct((m, D_CONV), _BF16),
                   jax.ShapeDtypeStruct((D_MODEL, n_rest * COL_BLK), _BF16),
                   *[jax.ShapeDtypeStruct(w.shape, _BF16) for w in small]),
        grid=(batch, nt),
        in_specs=[
            pl.BlockSpec((CONV_T, D_CONV), lambda b, t: (b * nt + t, GLU_A_COL)),
            pl.BlockSpec((CONV_T, D_CONV), lambda b, t: (b * nt + t, GLU_G_COL)),
            pl.BlockSpec((CONV_HALO, D_CONV), halo_idx(GLU_A_COL)),
            pl.BlockSpec((CONV_HALO, D_CONV), halo_idx(GLU_G_COL)),
            pl.BlockSpec((CONV_T, D_CONV), lambda b, t: (b * nt + t, ZC_COL)),
            vec(2 * D_CONV),
            pl.BlockSpec((nblk, CONV_WIDTH, LANES), lambda b, t: (0, 0, 0)),
            pl.BlockSpec((nblk, 1, LANES), lambda b, t: (0, 0, 0)),
            vec(D_CONV), vec(D_CONV),
            *[row_slab(D_MODEL, COL_BLK, blk) for blk in W_REST_BLKS],
            *[row_slab(*w.shape) for w in small],
        ],
        out_specs=(pl.BlockSpec((CONV_T, D_CONV), lambda b, t: (b * nt + t, 0)),
                   row_slab(D_MODEL, n_rest * COL_BLK),
                   *[row_slab(*w.shape) for w in small]),
        scratch_shapes=[
            pltpu.VMEM((nblk, CONV_HALO + CONV_T, LANES), _F32),
            pltpu.VMEM((SUBLANES - 1, CONV_HALO + CONV_T - SUBLANES, LANES), _F32),
            pltpu.VMEM((nblk, CONV_T, LANES), _F32),
        ],
        compiler_params=pltpu.CompilerParams(
            dimension_semantics=("arbitrary", "arbitrary"),
            vmem_limit_bytes=VMEM_LIMIT_BYTES),
        name="conv",
    )(pc, pc, pc, pc, pc, b_glu, w_blk, b_blk, ln_g, ln_b, *([w_in] * n_rest), *small)


def _merge_kernel(x_ref, o_ref, u_ref, ga0_ref, ga1_ref, gc0_ref, gc1_ref, wo_ref, wp_ref, bp_ref,
                  wout_ref, nw_ref, out_ref):
    ya = jnp.dot(o_ref[...], wo_ref[...], preferred_element_type=_F32)
    yc = jnp.dot(u_ref[...], wp_ref[...], preferred_element_type=_F32) + bp_ref[...]
    ga = jnp.concatenate([ga0_ref[...], ga1_ref[...]], axis=1).astype(_F32)
    gc = jnp.concatenate([gc0_ref[...], gc1_ref[...]], axis=1).astype(_F32)
    mix = _sigmoid(ga) * ya + _sigmoid(gc) * yc
    out = jnp.dot(mix.astype(_BF16), wout_ref[...], preferred_element_type=_F32)
    ms = jnp.mean(out * out, axis=-1, keepdims=True)
    out_ref[...] = x_ref[...] + out * lax.rsqrt(ms + NORM_EPS) * nw_ref[...]


def _merge(x2, o, u, proj, w_o_bf, w_pw2_bf, b_pw2, w_out_bf, norm_w):
    m = x2.shape[0]

    def const(shape):
        return pl.BlockSpec(shape, lambda i: (0, 0), pipeline_mode=pl.Buffered(1))

    def gate(col):
        return pl.BlockSpec((MERGE_TM, COL_BLK), lambda i: (i, col))

    return pl.pallas_call(
        _merge_kernel,
        out_shape=jax.ShapeDtypeStruct((m, D_MODEL), _F32),
        grid=(m // MERGE_TM,),
        in_specs=[
            pl.BlockSpec((MERGE_TM, D_MODEL), lambda i: (i, 0)),
            pl.BlockSpec((MERGE_TM, D_ATTN), lambda i: (i, 0)),
            pl.BlockSpec((MERGE_TM, D_CONV), lambda i: (i, 0)),
            gate(GA_COL), gate(GA_COL + 1), gate(GC_COL), gate(GC_COL + 1),
            const((D_ATTN, D_MODEL)),
            const((D_CONV, D_MODEL)),
            const((1, D_MODEL)),
            const((D_MODEL, D_MODEL)),
            const((1, D_MODEL)),
        ],
        out_specs=pl.BlockSpec((MERGE_TM, D_MODEL), lambda i: (i, 0)),
        compiler_params=pltpu.CompilerParams(
            dimension_semantics=("arbitrary",),
            vmem_limit_bytes=VMEM_LIMIT_BYTES),
        name="merge",
    )(x2, o, u, proj, proj, proj, proj, w_o_bf, w_pw2_bf, b_pw2, w_out_bf, norm_w)


def _rotary_tables(seq):
    f32 = np.float32
    half = ROT_DIM // 2
    inv_freq = f32(ROPE_THETA) ** (-np.arange(0, ROT_DIM, 2, dtype=f32) / f32(ROT_DIM))
    ang = np.arange(seq, dtype=f32)[:, None] * inv_freq[None, :]
    cos, sin = np.cos(ang).astype(f32), np.sin(ang).astype(f32)
    ones = np.ones((seq, DA_HEAD_DIM - ROT_DIM), f32)
    zeros = np.zeros((seq, DA_HEAD_DIM - ROT_DIM), f32)
    zh = np.zeros((seq, half), f32)
    cos_c = np.concatenate([cos, cos, ones], axis=1)
    sa_c = np.concatenate([-sin, zh, zeros], axis=1)
    sb_c = np.concatenate([zh, sin, zeros], axis=1)
    tile2 = lambda a: np.concatenate([a, a], axis=1)
    return tile2(cos_c), tile2(sa_c), tile2(sb_c)


def kernel(x, norm_pre_w, w_in, lambda_q1, lambda_k1, lambda_q2, lambda_k2, subln_w, w_o_attn, b_glu,
           w_dw, b_dw, ln_g, ln_b, w_pw2, b_pw2, w_out, norm_post_w):
    batch, seq, _ = x.shape
    assert norm_pre_w.shape[0] == 1, "single layer"
    assert seq % PROJ_CONV_TM == 0 and seq % ATTN_TQ == 0 and seq % CONV_T == 0
    x2 = x.reshape(batch * seq, D_MODEL)
    cos_t, sa_t, sb_t = _rotary_tables(seq)
    lam_params = jnp.concatenate([lambda_q1, lambda_k1, lambda_q2, lambda_k2], axis=0).astype(_F32)

    pc, h = _proj_conv(x2, norm_pre_w, w_in[0])
    u, w_rest_bf, w_o_bf, w_pw2_bf, w_out_bf = _conv(pc, b_glu, w_dw[0], b_dw, ln_g, ln_b, w_in[0],
                                                     w_o_attn[0], w_pw2[0], w_out[0], batch, seq)
    pr = _proj_rest(h, w_rest_bf, cos_t, sa_t, sb_t, seq)
    o = _attn(pr, lam_params, subln_w, batch, seq)
    out = _merge(x2, o, u, pr, w_o_bf, w_pw2_bf, b_pw2, w_out_bf, norm_post_w)
    return out.reshape(batch, seq, D_MODEL)
```
